```python
import jax, jax.numpy as jnp
from jax import lax
import numpy as np

D_MODEL = 2048
BATCH = 8
SEQ = 4096
DEPTH = 1
DEC_BATCH = 32
DEC_SEQ = 64
PAST_LEN = 1024

CHUNK = 64
PAST_CHUNKS = 8
KV_REACH = PAST_CHUNKS * CHUNK
N_HEADS = 16
HEAD_DIM = 64
D_ATTN = N_HEADS * HEAD_DIM
MAX_REL = 256
D_RNN = D_MODEL
N_RNN_BLOCKS = 16
RNN_BLOCK = D_RNN // N_RNN_BLOCKS
CONV_WIDTH = 4
LRU_C = 8.0
D_FF = 4 * D_MODEL
N_IN = 3 * D_ATTN + 2 * D_RNN + 2 * D_MODEL
EPS = 1e-6
NEG_INF = -1e30

kernel_name = 'hybrid_chunk_attn_rglru_step'


def rms_norm(x, g):
    xf = x.astype(jnp.float32)
    y = xf * lax.rsqrt(jnp.mean(xf * xf, axis=-1, keepdims=True) + EPS)
    return (y * g.astype(jnp.float32)).astype(x.dtype)


def rel_pos_bias(rel_bias, dist):
    return rel_bias[:, jnp.clip(dist, -MAX_REL, MAX_REL) + MAX_REL]


def attend(q, k, v, bias, valid):
    s = jnp.einsum('bqhd,bkhd->bhqk', q, k).astype(jnp.float32) * (HEAD_DIM ** -0.5)
    s = s + bias.astype(jnp.float32)[None]
    if valid is not None:
        s = jnp.where(valid, s, NEG_INF)
    p = jax.nn.softmax(s, axis=-1).astype(v.dtype)
    return jnp.einsum('bhqk,bkhd->bqhd', p, v)


def chunk_band_attention(q, k, v, rel_bias):
    b, s, h, dh = q.shape
    n_chunks = s // CHUNK
    band = KV_REACH + CHUNK
    pad = jnp.zeros((b, KV_REACH, h, dh), k.dtype)
    kp = jnp.concatenate([pad, k], axis=1)
    vp = jnp.concatenate([pad, v], axis=1)
    qi = jnp.arange(CHUNK)[:, None]
    ku = jnp.arange(band)[None, :]
    bias = rel_pos_bias(rel_bias, qi + KV_REACH - ku)

    def one_chunk(c):
        start = c * CHUNK
        qc = lax.dynamic_slice_in_dim(q, start, CHUNK, axis=1)
        kc = lax.dynamic_slice_in_dim(kp, start, band, axis=1)
        vc = lax.dynamic_slice_in_dim(vp, start, band, axis=1)
        valid = jnp.broadcast_to(ku >= KV_REACH - start, (CHUNK, band))
        return attend(qc, kc, vc, bias, valid)

    out = lax.map(one_chunk, jnp.arange(n_chunks))
    return jnp.swapaxes(out, 0, 1).reshape(b, s, h, dh)


def cached_band_attention(q, k_new, v_new, k_cache, v_cache, rel_bias):
    t = q.shape[1]
    n_cached = k_cache.shape[1]
    k = jnp.concatenate([k_cache.astype(k_new.dtype), k_new], axis=1)
    v = jnp.concatenate([v_cache.astype(v_new.dtype), v_new], axis=1)
    dist = jnp.arange(t)[:, None] + n_cached - jnp.arange(n_cached + t)[None, :]
    return attend(q, k, v, rel_pos_bias(rel_bias, dist), None)


def causal_depthwise_conv(x_ext, w, bias, t):
    out = bias + w[0] * x_ext[:, 0:t]
    for j in range(1, CONV_WIDTH):
        out = out + w[j] * x_ext[:, j:j + t]
    return out


def block_diag_linear(x, w, b):
    xb = x.reshape(x.shape[:-1] + (N_RNN_BLOCKS, RNN_BLOCK))
    return jnp.einsum('btnc,ncd->btnd', xb, w).reshape(x.shape) + b


def rg_lru(x, w_rg, b_rg, w_ig, b_ig, lru_lambda, h0):
    f32 = jnp.float32
    r = jax.nn.sigmoid(block_diag_linear(x, w_rg, b_rg).astype(f32))
    i = jax.nn.sigmoid(block_diag_linear(x, w_ig, b_ig).astype(f32))
    log_a = LRU_C * r * jax.nn.log_sigmoid(lru_lambda.astype(f32))
    a = jnp.exp(log_a)
    u = jnp.sqrt(-jnp.expm1(2.0 * log_a)) * (i * x.astype(f32))
    if h0 is not None:
        u = u.at[:, 0].add(a[:, 0] * h0.astype(f32))

    def combine(left, right):
        a_l, u_l = left
        a_r, u_r = right
        return a_l * a_r, a_r * u_l + u_r

    _, h = lax.associative_scan(combine, (a, u), axis=1)
    return h


def layer(x, cache_k, cache_v, conv_state, h0,
          pre_mix_g, w_in, rel_bias, conv_w, conv_b, w_rg, b_rg, w_ig, b_ig, lru_lambda,
          w_attn_up, w_rnn_up, w_out, post_mix_g, pre_ffn_g, w_ff1, w_ff2, post_ffn_g):
    b, t, _ = x.shape
    xn = rms_norm(x, pre_mix_g)
    z = xn @ w_in
    splits = [D_ATTN, 2 * D_ATTN, 3 * D_ATTN, 3 * D_ATTN + D_RNN,
              3 * D_ATTN + 2 * D_RNN, 3 * D_ATTN + 2 * D_RNN + D_MODEL]
    q, k, v, xr, gr, ga, gb = jnp.split(z, splits, axis=-1)
    q = q.reshape(b, t, N_HEADS, HEAD_DIM)
    k = k.reshape(b, t, N_HEADS, HEAD_DIM)
    v = v.reshape(b, t, N_HEADS, HEAD_DIM)

    if cache_k is None:
        o_a = chunk_band_attention(q, k, v, rel_bias)
        keep = min(KV_REACH, t)
        new_k, new_v = k[:, t - keep:], v[:, t - keep:]
        conv_prefix = jnp.zeros((b, CONV_WIDTH - 1, D_RNN), xr.dtype)
    else:
        o_a = cached_band_attention(q, k, v, cache_k, cache_v, rel_bias)
        new_k, new_v = k, v
        conv_prefix = conv_state.astype(xr.dtype)

    xr_ext = jnp.concatenate([conv_prefix, xr], axis=1)
    xc = causal_depthwise_conv(xr_ext, conv_w, conv_b, t)
    h = rg_lru(xc, w_rg, b_rg, w_ig, b_ig, lru_lambda, h0)
    o_b = (h * jax.nn.gelu(gr.astype(jnp.float32), approximate=True)).astype(x.dtype)

    merged = (jax.nn.sigmoid(ga) * (o_a.reshape(b, t, D_ATTN) @ w_attn_up)
              + jax.nn.sigmoid(gb) * (o_b @ w_rnn_up))
    x = x + rms_norm(merged @ w_out, post_mix_g)

    hidden = jnp.square(jax.nn.relu(rms_norm(x, pre_ffn_g) @ w_ff1))
    x = x + rms_norm(hidden @ w_ff2, post_ffn_g)
    return x, new_k, new_v, xr_ext[:, t:], h[:, -1].astype(x.dtype)


def setup_inputs(seed: int = 0) -> dict:
    key = jax.random.key(seed)
    ks = jax.random.split(key, 24)
    kv_len = min(KV_REACH, PAST_LEN)

    def nrm(k, shape, scale):
        return jax.random.normal(k, shape, jnp.float32) * scale

    def gain(k):
        return 1.0 + nrm(k, (DEPTH, D_MODEL), 0.05)

    a_target = jax.random.uniform(ks[15], (DEPTH, D_RNN), jnp.float32, 0.9, 0.999)
    s = a_target ** (1.0 / LRU_C)
    lru_lambda = jnp.log(s) - jnp.log1p(-s)

    return {
        'x_prompt': nrm(ks[0], (BATCH, SEQ, D_MODEL), 1.0),
        'x_sample': nrm(ks[1], (DEC_BATCH, DEC_SEQ, D_MODEL), 1.0),
        'cache_k': nrm(ks[2], (DEPTH, DEC_BATCH, kv_len, N_HEADS, HEAD_DIM), 1.0),
        'cache_v': nrm(ks[3], (DEPTH, DEC_BATCH, kv_len, N_HEADS, HEAD_DIM), 1.0),
        'state_conv': nrm(ks[4], (DEPTH, DEC_BATCH, CONV_WIDTH - 1, D_RNN), 1.0),
        'state_h': nrm(ks[5], (DEPTH, DEC_BATCH, D_RNN), 1.0),
        'pre_mix_g': gain(ks[6]),
        'w_in': nrm(ks[7], (DEPTH, D_MODEL, N_IN), D_MODEL ** -0.5),
        'rel_bias': nrm(ks[8], (DEPTH, N_HEADS, 2 * MAX_REL + 1), 0.5),
        'conv_w': nrm(ks[9], (DEPTH, CONV_WIDTH, D_RNN), CONV_WIDTH ** -0.5),
        'conv_b': nrm(ks[10], (DEPTH, D_RNN), 0.02),
        'w_rg': nrm(ks[11], (DEPTH, N_RNN_BLOCKS, RNN_BLOCK, RNN_BLOCK), RNN_BLOCK ** -0.5),
        'b_rg': nrm(ks[12], (DEPTH, D_RNN), 0.02),
        'w_ig': nrm(ks[13], (DEPTH, N_RNN_BLOCKS, RNN_BLOCK, RNN_BLOCK), RNN_BLOCK ** -0.5),
        'b_ig': nrm(ks[14], (DEPTH, D_RNN), 0.02),
        'lru_lambda': lru_lambda,
        'w_attn_up': nrm(ks[16], (DEPTH, D_ATTN, D_MODEL), D_ATTN ** -0.5),
        'w_rnn_up': nrm(ks[17], (DEPTH, D_RNN, D_MODEL), D_RNN ** -0.5),
        'w_out': nrm(ks[18], (DEPTH, D_MODEL, D_MODEL), D_MODEL ** -0.5),
        'post_mix_g': gain(ks[19]),
        'pre_ffn_g': gain(ks[20]),
        'w_ff1': nrm(ks[21], (DEPTH, D_MODEL, D_FF), D_MODEL ** -0.5),
        'w_ff2': nrm(ks[22], (DEPTH, D_FF, D_MODEL), D_FF ** -0.5),
        'post_ffn_g': gain(ks[23]),
    }


def reference(x_prompt, x_sample, cache_k, cache_v, state_conv, state_h,
              pre_mix_g, w_in, rel_bias, conv_w, conv_b, w_rg, b_rg, w_ig, b_ig, lru_lambda,
              w_attn_up, w_rnn_up, w_out, post_mix_g, pre_ffn_g, w_ff1, w_ff2, post_ffn_g):
    y_prompt, y_sample = x_prompt, x_sample
    kp_l, vp_l, cp_l, hp_l = [], [], [], []
    ks_l, vs_l, cs_l, hs_l = [], [], [], []
    for l in range(DEPTH):
        lw = (pre_mix_g[l], w_in[l], rel_bias[l], conv_w[l], conv_b[l], w_rg[l], b_rg[l],
              w_ig[l], b_ig[l], lru_lambda[l], w_attn_up[l], w_rnn_up[l], w_out[l],
              post_mix_g[l], pre_ffn_g[l], w_ff1[l], w_ff2[l], post_ffn_g[l])
        y_prompt, kp, vp, cp, hp = layer(y_prompt, None, None, None, None, *lw)
        y_sample, ksm, vsm, csm, hsm = layer(y_sample, cache_k[l], cache_v[l],
                                             state_conv[l], state_h[l], *lw)
        kp_l.append(kp); vp_l.append(vp); cp_l.append(cp); hp_l.append(hp)
        ks_l.append(ksm); vs_l.append(vsm); cs_l.append(csm); hs_l.append(hsm)
    k_prompt = jnp.stack(kp_l)
    v_prompt = jnp.stack(vp_l)
    conv_prompt = jnp.stack(cp_l)
    h_prompt = jnp.stack(hp_l)
    k_sample = jnp.stack(ks_l)
    v_sample = jnp.stack(vs_l)
    conv_sample = jnp.stack(cs_l)
    h_sample = jnp.stack(hs_l)
    return (y_prompt, y_sample, k_prompt, v_prompt, conv_prompt, h_prompt,
            k_sample, v_sample, conv_sample, h_sample)
```

```python
import functools

import jax
import jax.numpy as jnp
from jax import lax
from jax.experimental import pallas as pl
from jax.experimental.pallas import tpu as pltpu

F32 = jnp.float32
BF16 = jnp.bfloat16

CHUNK = 64
PAST_CHUNKS = 8
KV_REACH = PAST_CHUNKS * CHUNK
N_HEADS = 16
HEAD_DIM = 64
D_ATTN = N_HEADS * HEAD_DIM
MAX_REL = 256
RNN_BLOCK = 128
CONV_WIDTH = 4
LRU_C = 8.0
EPS = 1e-6
NEG_INF = -1e30

LANES = 128
SUBLANES = 8
MIB = 1024 * 1024


def _params(n_axes, vmem_mib):
    return pltpu.CompilerParams(
        dimension_semantics=("arbitrary",) * n_axes,
        vmem_limit_bytes=vmem_mib * MIB,
    )


def _rmsnorm_rows(x, g):
    ms = jnp.mean(x * x, axis=-1, keepdims=True)
    return x * lax.rsqrt(ms + EPS) * g


def _in_proj_body(x_ref, g_ref, w_ref, o_ref, xn_ref):
    @pl.when(pl.program_id(1) == 0)
    def _():
        xn_ref[...] = _rmsnorm_rows(x_ref[...], g_ref[...]).astype(BF16)

    o_ref[...] = jnp.dot(xn_ref[...], w_ref[...], preferred_element_type=F32).astype(o_ref.dtype)


def _in_proj(x2d, g, w, *, tm, tn, n_row_tiles, row_tile_of, col_blk0, n_col_tiles, out_dtype, name):
    d = x2d.shape[1]
    return pl.pallas_call(
        _in_proj_body,
        grid=(n_row_tiles, n_col_tiles),
        in_specs=[
            pl.BlockSpec((tm, d), lambda i, j: (row_tile_of(i), 0)),
            pl.BlockSpec((1, d), lambda i, j: (0, 0)),
            pl.BlockSpec((d, tn), lambda i, j: (0, col_blk0 + j)),
        ],
        out_specs=pl.BlockSpec((tm, tn), lambda i, j: (i, j)),
        out_shape=jax.ShapeDtypeStruct((n_row_tiles * tm, n_col_tiles * tn), out_dtype),
        scratch_shapes=[pltpu.VMEM((tm, d), BF16)],
        compiler_params=_params(2, 48),
        name=name,
    )(x2d, g, w)


def _bias_tables(rel_bias, qt, kw, offsets):
    r = jnp.arange(qt)[:, None]
    j = jnp.arange(kw)[None, :]
    tabs = []
    for off in offsets:
        dist = r + off - j
        b = rel_bias[:, jnp.clip(dist, -MAX_REL, MAX_REL) + MAX_REL]
        cq = (r + off) // CHUNK
        ck = j // CHUNK
        valid = (ck <= cq) & (ck >= cq - PAST_CHUNKS)
        tabs.append(jnp.where(valid[None], b, NEG_INF))
    return jnp.stack(tabs).astype(F32)


def _attn_body(q_ref, k_ref, v_ref, b_ref, o_ref, *, qt, kw, heads):
    qb = pl.program_id(2)
    ks = pl.multiple_of(jnp.maximum(qb * qt - KV_REACH, 0), CHUNK)
    q = q_ref[0]
    k = k_ref[0, pl.ds(ks, kw), :]
    v = v_ref[0, pl.ds(ks, kw), :]
    lane = lax.broadcasted_iota(jnp.int32, (1, heads * HEAD_DIM), 1)
    out = None
    for h in range(heads):
        in_h = (lane >= h * HEAD_DIM) & (lane < (h + 1) * HEAD_DIM)
        qh = jnp.where(in_h, q, jnp.zeros_like(q)) * jnp.asarray(HEAD_DIM ** -0.5, q.dtype)
        s = lax.dot_general(qh, k, (((1,), (1,)), ((), ())), preferred_element_type=F32)
        s = s + b_ref[0, h]
        m = jnp.max(s, axis=-1, keepdims=True)
        p = jnp.exp(s - m)
        l = jnp.sum(p, axis=-1, keepdims=True)
        oh = jnp.dot(p.astype(BF16), v, preferred_element_type=F32) * (1.0 / l)
        out = oh if out is None else jnp.where(in_h, oh, out)
    o_ref[0] = out.astype(o_ref.dtype)


def _attention(q_arr, k_arr, v_arr, bias, *, q_col0, k_col0, v_col0, qt, kw, heads, name):
    b, sq, _ = q_arr.shape
    sk = k_arr.shape[1]
    hw = heads * HEAD_DIM
    n_groups = N_HEADS // heads
    n_case = bias.shape[0]
    qc, kc, vc = q_col0 // hw, k_col0 // hw, v_col0 // hw
    return pl.pallas_call(
        functools.partial(_attn_body, qt=qt, kw=kw, heads=heads),
        grid=(n_groups, b, sq // qt),
        in_specs=[
            pl.BlockSpec((1, qt, hw), lambda g, i, t: (i, t, qc + g)),
            pl.BlockSpec((1, sk, hw), lambda g, i, t: (i, 0, kc + g)),
            pl.BlockSpec((1, sk, hw), lambda g, i, t: (i, 0, vc + g)),
            pl.BlockSpec((1, heads, qt, kw), lambda g, i, t: (jnp.minimum(t, n_case - 1), g, 0, 0)),
        ],
        out_specs=pl.BlockSpec((1, qt, hw), lambda g, i, t: (i, t, g)),
        out_shape=jax.ShapeDtypeStruct((b, sq, D_ATTN), BF16),
        compiler_params=_params(3, 40),
        name=name,
    )(q_arr, k_arr, v_arr, bias)


def _log_sigmoid(x):
    return jnp.minimum(x, 0.0) - jnp.log1p(jnp.exp(-jnp.abs(x)))


def _rglru_body(xr_ref, gr_ref, c0_ref, h0_ref, cw_ref, cb_ref, wg_ref, brg_ref, big_ref, lam_ref,
                ob_ref, hl_ref, xbuf, hcar, abuf, ubuf, *, tt, cw):
    pad = SUBLANES
    t = pl.program_id(2)

    @pl.when(t == 0)
    def _():
        hcar[...] = h0_ref[0]
        xbuf[pad - 3:pad, :] = c0_ref[0]

    x = xr_ref[0].astype(F32)
    xbuf[pad:pad + tt, :] = x
    w = cw_ref[...]
    xc = cb_ref[...] + w[0:1] * xbuf[pad - 3:pad - 3 + tt, :]
    xc = xc + w[1:2] * xbuf[pad - 2:pad - 2 + tt, :]
    xc = xc + w[2:3] * xbuf[pad - 1:pad - 1 + tt, :]
    xc = xc + w[3:4] * x
    xbuf[pad - 3:pad, :] = xbuf[pad + tt - 3:pad + tt, :]

    xcb = xc.astype(BF16)
    r_parts, i_parts = [], []
    for n in range(cw // RNN_BLOCK):
        gts = jnp.dot(xcb[:, n * RNN_BLOCK:(n + 1) * RNN_BLOCK], wg_ref[n], preferred_element_type=F32)
        r_parts.append(gts[:, :RNN_BLOCK])
        i_parts.append(gts[:, RNN_BLOCK:])
    r = jax.nn.sigmoid(jnp.concatenate(r_parts, axis=1) + brg_ref[...])
    i = jax.nn.sigmoid(jnp.concatenate(i_parts, axis=1) + big_ref[...])
    log_a = r * (LRU_C * _log_sigmoid(lam_ref[...]))
    abuf[...] = jnp.exp(log_a)
    th = jnp.tanh(log_a)
    ubuf[...] = jnp.sqrt(-2.0 * th / (1.0 - th)) * (i * xc)

    row = lax.broadcasted_iota(jnp.int32, (SUBLANES, cw), 0)

    def group(g, h):
        r0 = pl.multiple_of(g * SUBLANES, SUBLANES)
        a = abuf[pl.ds(r0, SUBLANES), :]
        u = ubuf[pl.ds(r0, SUBLANES), :]
        for d in (1, 2, 4):
            keep = row >= d
            a_prev = jnp.where(keep, pltpu.roll(a, d, 0), 1.0)
            u_prev = jnp.where(keep, pltpu.roll(u, d, 0), 0.0)
            u = a * u_prev + u
            a = a * a_prev
        hg = u + a * h
        ubuf[pl.ds(r0, SUBLANES), :] = hg
        return jnp.broadcast_to(hg[SUBLANES - 1:SUBLANES, :], (SUBLANES, cw))

    h_in = jnp.broadcast_to(hcar[...], (SUBLANES, cw))
    h_out = lax.fori_loop(0, tt // SUBLANES, group, h_in)
    hcar[...] = h_out[0:1, :]
    hl_ref[0] = h_out[0:1, :]
    gate = jax.nn.gelu(gr_ref[0].astype(F32), approximate=True)
    ob_ref[0] = (ubuf[...] * gate).astype(ob_ref.dtype)


def _rglru(z3, conv0, h0, conv_w, conv_b, w_gates, b_rg, b_ig, lam, *, xr_col0, gr_col0, tt, cw, name):
    b, t, _ = z3.shape
    d_rnn = conv_w.shape[1]
    n_cg = d_rnn // cw
    blocks_per_cg = cw // RNN_BLOCK
    xc0, gc0 = xr_col0 // cw, gr_col0 // cw
    vec = pl.BlockSpec((1, cw), lambda c, i, s: (0, c))
    return pl.pallas_call(
        functools.partial(_rglru_body, tt=tt, cw=cw),
        grid=(n_cg, b, t // tt),
        in_specs=[
            pl.BlockSpec((1, tt, cw), lambda c, i, s: (i, s, xc0 + c)),
            pl.BlockSpec((1, tt, cw), lambda c, i, s: (i, s, gc0 + c)),
            pl.BlockSpec((1, CONV_WIDTH - 1, cw), lambda c, i, s: (i, 0, c)),
            pl.BlockSpec((1, 1, cw), lambda c, i, s: (i, 0, c)),
            pl.BlockSpec((CONV_WIDTH, cw), lambda c, i, s: (0, c)),
            vec,
            pl.BlockSpec((blocks_per_cg, RNN_BLOCK, 2 * RNN_BLOCK), lambda c, i, s: (c, 0, 0)),
            vec, vec, vec,
        ],
        out_specs=[
            pl.BlockSpec((1, tt, cw), lambda c, i, s: (i, s, c)),
            pl.BlockSpec((1, 1, cw), lambda c, i, s: (i, 0, c)),
        ],
        out_shape=[
            jax.ShapeDtypeStruct((b, t, d_rnn), BF16),
            jax.ShapeDtypeStruct((b, 1, d_rnn), F32),
        ],
        scratch_shapes=[
            pltpu.VMEM((SUBLANES + tt, cw), F32),
            pltpu.VMEM((1, cw), F32),
            pltpu.VMEM((tt, cw), F32),
            pltpu.VMEM((tt, cw), F32),
        ],
        compiler_params=_params(3, 40),
        name=name,
    )(z3, z3, conv0, h0, conv_w, conv_b, w_gates, b_rg, b_ig, lam)


def _merge_body(oa_ref, ob_ref, ga_ref, gb_ref, wa_ref, wb_ref, o_ref):
    ya = jnp.dot(oa_ref[...], wa_ref[...], preferred_element_type=F32)
    yb = jnp.dot(ob_ref[...], wb_ref[...], preferred_element_type=F32)
    ga = jax.nn.sigmoid(ga_ref[...].astype(F32))
    gb = jax.nn.sigmoid(gb_ref[...].astype(F32))
    o_ref[...] = (ga * ya + gb * yb).astype(o_ref.dtype)


def _gate_merge(o_a, o_b, z2, w_a, w_b, *, ga_col0, gb_col0, tm, tn, name):
    m = o_a.shape[0]
    d = w_a.shape[1]
    ga0, gb0 = ga_col0 // tn, gb_col0 // tn
    return pl.pallas_call(
        _merge_body,
        grid=(m // tm, d // tn),
        in_specs=[
            pl.BlockSpec((tm, o_a.shape[1]), lambda i, j: (i, 0)),
            pl.BlockSpec((tm, o_b.shape[1]), lambda i, j: (i, 0)),
            pl.BlockSpec((tm, tn), lambda i, j: (i, ga0 + j)),
            pl.BlockSpec((tm, tn), lambda i, j: (i, gb0 + j)),
            pl.BlockSpec((w_a.shape[0], tn), lambda i, j: (0, j)),
            pl.BlockSpec((w_b.shape[0], tn), lambda i, j: (0, j)),
        ],
        out_specs=pl.BlockSpec((tm, tn), lambda i, j: (i, j)),
        out_shape=jax.ShapeDtypeStruct((m, d), BF16),
        compiler_params=_params(2, 48),
        name=name,
    )(o_a, o_b, z2, z2, w_a, w_b)


def _out_norm_body(m_ref, w_ref, x_ref, g_ref, o_ref):
    y = jnp.dot(m_ref[...], w_ref[...], preferred_element_type=F32)
    o_ref[...] = x_ref[...] + _rmsnorm_rows(y, g_ref[...])


def _out_norm(merged, w_out, x2d, g, *, tm, name):
    m, d = x2d.shape
    return pl.pallas_call(
        _out_norm_body,
        grid=(m // tm,),
        in_specs=[
            pl.BlockSpec((tm, d), lambda i: (i, 0)),
            pl.BlockSpec((d, d), lambda i: (0, 0)),
            pl.BlockSpec((tm, d), lambda i: (i, 0)),
            pl.BlockSpec((1, d), lambda i: (0, 0)),
        ],
        out_specs=pl.BlockSpec((tm, d), lambda i: (i, 0)),
        out_shape=jax.ShapeDtypeStruct((m, d), F32),
        compiler_params=_params(1, 48),
        name=name,
    )(merged, w_out, x2d, g)


def _ffn_body(x_ref, g1_ref, w1_ref, w2_ref, g2_ref, o_ref, xn_ref):
    f = pl.program_id(1)

    @pl.when(f == 0)
    def _():
        xn_ref[...] = _rmsnorm_rows(x_ref[...], g1_ref[...]).astype(BF16)

    hid = jnp.dot(xn_ref[...], w1_ref[...], preferred_element_type=F32)
    hid = jnp.square(jnp.maximum(hid, 0.0)).astype(BF16)
    part = jnp.dot(hid, w2_ref[...], preferred_element_type=F32)

    @pl.when(f == 0)
    def _():
        o_ref[...] = part

    @pl.when(f > 0)
    def _():
        o_ref[...] += part

    @pl.when(f == pl.num_programs(1) - 1)
    def _():
        o_ref[...] = x_ref[...] + _rmsnorm_rows(o_ref[...], g2_ref[...])


def _ffn(x2d, g1, w1, w2, g2, *, tm, tf, name):
    m, d = x2d.shape
    d_ff = w1.shape[1]
    return pl.pallas_call(
        _ffn_body,
        grid=(m // tm, d_ff // tf),
        in_specs=[
            pl.BlockSpec((tm, d), lambda i, f: (i, 0)),
            pl.BlockSpec((1, d), lambda i, f: (0, 0)),
            pl.BlockSpec((d, tf), lambda i, f: (0, f)),
            pl.BlockSpec((tf, d), lambda i, f: (f, 0)),
            pl.BlockSpec((1, d), lambda i, f: (0, 0)),
        ],
        out_specs=pl.BlockSpec((tm, d), lambda i, f: (i, 0)),
        out_shape=jax.ShapeDtypeStruct((m, d), F32),
        scratch_shapes=[pltpu.VMEM((tm, d), BF16)],
        compiler_params=_params(2, 48),
        name=name,
    )(x2d, g1, w1, w2, g2)


def _pick_tile(n, target):
    t = min(n, target)
    while n % t:
        t //= 2
    return t


def _layer(x, cache_k, cache_v, conv_state, h_state, wts, tag):
    b, t, d = x.shape
    m = b * t
    x2d = x.reshape(m, d)
    d_rnn = wts["conv_w"].shape[1]
    n_in = wts["w_in"].shape[1]
    col_k, col_v, col_xr = D_ATTN, 2 * D_ATTN, 3 * D_ATTN
    col_gr = col_xr + d_rnn
    col_ga = col_gr + d_rnn
    col_gb = col_ga + d

    tm = _pick_tile(m, 1024)
    tn = 512
    z2 = _in_proj(x2d, wts["pre_mix_g"], wts["w_in"], tm=tm, tn=tn, n_row_tiles=m // tm,
                  row_tile_of=lambda i: i, col_blk0=0, n_col_tiles=n_in // tn, out_dtype=BF16,
                  name=f"in_proj_{tag}")
    z3 = z2.reshape(b, t, n_in)

    keep = min(KV_REACH, t)
    if keep == t:
        ts, n_state_tiles, state_tile_of = _pick_tile(m, 512), m // _pick_tile(m, 512), lambda i: i
    else:
        tiles_per_seq = t // keep
        ts, n_state_tiles = keep, b
        state_tile_of = lambda i: i * tiles_per_seq + tiles_per_seq - 1
    state = _in_proj(x2d, wts["pre_mix_g"], wts["w_in"], tm=ts, tn=tn, n_row_tiles=n_state_tiles,
                     row_tile_of=state_tile_of,
                     col_blk0=col_k // tn, n_col_tiles=(col_gr - col_k) // tn, out_dtype=F32,
                     name=f"state_proj_{tag}")
    state = state.reshape(b, keep, col_gr - col_k)
    new_k = state[:, :, :D_ATTN].reshape(b, keep, N_HEADS, HEAD_DIM)
    new_v = state[:, :, D_ATTN:2 * D_ATTN].reshape(b, keep, N_HEADS, HEAD_DIM)
    conv_tail = state[:, keep - (CONV_WIDTH - 1):, 2 * D_ATTN:]

    heads = 2
    if cache_k is None:
        qt = _pick_tile(t, 4 * CHUNK)
        kw = qt + KV_REACH
        offsets = [min(i * qt, KV_REACH) for i in range(KV_REACH // qt + 1)]
        bias = _bias_tables(wts["rel_bias"], qt, kw, offsets)
        o_a = _attention(z3, z3, z3, bias, q_col0=0, k_col0=col_k, v_col0=col_v, qt=qt, kw=kw,
                         heads=heads, name=f"attn_{tag}")
        conv0 = jnp.zeros((b, CONV_WIDTH - 1, d_rnn), F32)
        h0 = jnp.zeros((b, 1, d_rnn), F32)
    else:
        n_cached = cache_k.shape[1]
        k_all = jnp.concatenate([cache_k.reshape(b, n_cached, D_ATTN).astype(BF16),
                                 z3[:, :, col_k:col_k + D_ATTN]], axis=1)
        v_all = jnp.concatenate([cache_v.reshape(b, n_cached, D_ATTN).astype(BF16),
                                 z3[:, :, col_v:col_v + D_ATTN]], axis=1)
        bias = _bias_tables(wts["rel_bias"], t, n_cached + t, [n_cached])
        o_a = _attention(z3, k_all, v_all, bias, q_col0=0, k_col0=0, v_col0=0, qt=t, kw=n_cached + t,
                         heads=heads, name=f"attn_{tag}")
        conv0 = conv_state.astype(F32)
        h0 = h_state.astype(F32).reshape(b, 1, d_rnn)

    o_b, h_last = _rglru(z3, conv0, h0, wts["conv_w"], wts["conv_b"], wts["w_gates"], wts["b_rg"],
                         wts["b_ig"], wts["lru_lambda"], xr_col0=col_xr, gr_col0=col_gr,
                         tt=_pick_tile(t, 256), cw=1024, name=f"rglru_{tag}")

    merged = _gate_merge(o_a.reshape(m, D_ATTN), o_b.reshape(m, d_rnn), z2, wts["w_attn_up"],
                         wts["w_rnn_up"], ga_col0=col_ga, gb_col0=col_gb, tm=tm, tn=tn,
                         name=f"gate_merge_{tag}")
    tm2 = _pick_tile(m, 512)
    x1 = _out_norm(merged, wts["w_out"], x2d, wts["post_mix_g"], tm=tm2, name=f"out_norm_{tag}")
    y = _ffn(x1, wts["pre_ffn_g"], wts["w_ff1"], wts["w_ff2"], wts["post_ffn_g"], tm=tm2, tf=1024,
             name=f"ffn_{tag}")
    return y.reshape(b, t, d), new_k, new_v, conv_tail, h_last.reshape(b, d_rnn)


def kernel(x_prompt, x_sample, cache_k, cache_v, state_conv, state_h, pre_mix_g, w_in, rel_bias, conv_w, conv_b, w_rg, b_rg, w_ig, b_ig, lru_lambda, w_attn_up, w_rnn_up, w_out, post_mix_g, pre_ffn_g, w_ff1, w_ff2, post_ffn_g):
    depth = w_in.shape[0]
    y_p, y_s = x_prompt, x_sample
    outs_p, outs_s = [], []
    for l in range(depth):
        wts = {
            "pre_mix_g": pre_mix_g[l][None], "post_mix_g": post_mix_g[l][None],
            "pre_ffn_g": pre_ffn_g[l][None], "post_ffn_g": post_ffn_g[l][None],
            "w_in": w_in[l].astype(BF16), "rel_bias": rel_bias[l],
            "conv_w": conv_w[l], "conv_b": conv_b[l][None],
            "w_gates": jnp.concatenate([w_rg[l], w_ig[l]], axis=-1).astype(BF16),
            "b_rg": b_rg[l][None], "b_ig": b_ig[l][None], "lru_lambda": lru_lambda[l][None],
            "w_attn_up": w_attn_up[l].astype(BF16), "w_rnn_up": w_rnn_up[l].astype(BF16),
            "w_out": w_out[l].astype(BF16), "w_ff1": w_ff1[l].astype(BF16), "w_ff2": w_ff2[l].astype(BF16),
        }
        y_p, *st_p = _layer(y_p, None, None, None, None, wts, f"p{l}")
        y_s, *st_s = _layer(y_s, cache_k[l], cache_v[l], state_conv[l], state_h[l], wts, f"s{l}")
        outs_p.append(st_p)
        outs_s.append(st_s)
    stack = lambda outs, i: jnp.stack([o[i] for o in outs])
    return (y_p, y_s,
            stack(outs_p, 0), stack(outs_p, 1), stack(outs_p, 2), stack(outs_p, 3),
            stack(outs_s, 0), stack(outs_s, 1), stack(outs_s, 2), stack(outs_s, 3))
```

```python
import functools

import jax
import jax.numpy as jnp
from jax import lax
from jax.experimental import pallas as pl
from jax.experimental.pallas import tpu as pltpu

F32 = jnp.float32
BF16 = jnp.bfloat16

CHUNK = 64
PAST_CHUNKS = 8
KV_REACH = PAST_CHUNKS * CHUNK
N_HEADS = 16
HEAD_DIM = 64
D_ATTN = N_HEADS * HEAD_DIM
MAX_REL = 256
RNN_BLOCK = 128
CONV_WIDTH = 4
LRU_C = 8.0
EPS = 1e-6
NEG_INF = -1e30

LANES = 128
SUBLANES = 8
MIB = 1024 * 1024


def _params(n_axes, vmem_mib):
    return pltpu.CompilerParams(
        dimension_semantics=("arbitrary",) * n_axes,
        vmem_limit_bytes=vmem_mib * MIB,
    )


def _rmsnorm_rows(x, g):
    ms = jnp.mean(x * x, axis=-1, keepdims=True)
    return x * lax.rsqrt(ms + EPS) * g


def _in_proj_body(x_ref, g_ref, w_ref, o_ref, xn_ref):
    @pl.when(pl.program_id(1) == 0)
    def _():
        xn_ref[...] = _rmsnorm_rows(x_ref[...], g_ref[...]).astype(BF16)

    o_ref[...] = jnp.dot(xn_ref[...], w_ref[...], preferred_element_type=F32).astype(o_ref.dtype)


def _in_proj(x2d, g, w, *, tm, tn, n_row_tiles, row_tile_of, col_blk0, n_col_tiles, out_dtype, name):
    d = x2d.shape[1]
    return pl.pallas_call(
        _in_proj_body,
        grid=(n_row_tiles, n_col_tiles),
        in_specs=[
            pl.BlockSpec((tm, d), lambda i, j: (row_tile_of(i), 0)),
            pl.BlockSpec((1, d), lambda i, j: (0, 0)),
            pl.BlockSpec((d, tn), lambda i, j: (0, col_blk0 + j)),
        ],
        out_specs=pl.BlockSpec((tm, tn), lambda i, j: (i, j)),
        out_shape=jax.ShapeDtypeStruct((n_row_tiles * tm, n_col_tiles * tn), out_dtype),
        scratch_shapes=[pltpu.VMEM((tm, d), BF16)],
        compiler_params=_params(2, 48),
        name=name,
    )(x2d, g, w)


def _bias_diagonals(rel_bias, width, offsets, heads):
    x = jnp.arange(width)
    rows = [rel_bias[:, jnp.clip(off + CHUNK - 1 - x, -MAX_REL, MAX_REL) + MAX_REL] for off in offsets]
    g = jnp.stack(rows, axis=1).astype(F32)
    n_groups = rel_bias.shape[0] // heads
    g = g.reshape(n_groups, heads, len(offsets), width)
    return jnp.swapaxes(g, 1, 2).reshape(n_groups, len(offsets) * heads, width)


def _attn_body(q_ref, k_ref, v_ref, g_ref, o_ref, tab_ref, *, bb, n_chunks, kw, heads, offsets):
    width = g_ref.shape[2]
    hw = heads * HEAD_DIM

    @pl.when(pl.program_id(1) == 0)
    def _():
        r = lax.broadcasted_iota(jnp.int32, (CHUNK, kw), 0)
        j = lax.broadcasted_iota(jnp.int32, (CHUNK, kw), 1)
        for c, off in enumerate(offsets):
            cq = (r + off) // CHUNK
            ck = j // CHUNK
            valid = (ck <= cq) & (ck >= cq - PAST_CHUNKS)
            for h in range(heads):
                g = jnp.broadcast_to(g_ref[0, c * heads + h:c * heads + h + 1, :], (CHUNK, width))
                t = pltpu.roll(g, width - (CHUNK - 1), 1, stride=1, stride_axis=0)
                tab_ref[c, h * CHUNK:(h + 1) * CHUNK, :] = jnp.where(valid, t[:, :kw], NEG_INF)

    lane = lax.broadcasted_iota(jnp.int32, (1, hw), 1)
    head_lanes = [(lane >= h * HEAD_DIM) & (lane < (h + 1) * HEAD_DIM) for h in range(heads)]

    def chunk(idx, carry):
        bi, c = (0, idx) if bb == 1 else (idx // n_chunks, idx % n_chunks)
        case = jnp.minimum(c, len(offsets) - 1)
        q0 = pl.multiple_of(c * CHUNK, CHUNK)
        ks = pl.multiple_of(jnp.maximum(c * CHUNK - KV_REACH, 0), CHUNK)
        q = q_ref[bi, pl.ds(q0, CHUNK), :]
        qs = jnp.concatenate([jnp.where(m, q, jnp.zeros_like(q)) for m in head_lanes], axis=0)
        qs = qs * jnp.asarray(HEAD_DIM ** -0.5, q.dtype)
        kb = k_ref[bi, pl.ds(ks, kw), :]
        vb = v_ref[bi, pl.ds(ks, kw), :]
        s = lax.dot_general(qs, kb, (((1,), (1,)), ((), ())), preferred_element_type=F32)
        s = s + tab_ref[case]
        p = jnp.exp(s - jnp.max(s, axis=-1, keepdims=True))
        l = jnp.sum(p, axis=-1, keepdims=True)
        o_all = jnp.dot(p.astype(BF16), vb, preferred_element_type=F32) * (1.0 / l)
        o = o_all[0:CHUNK]
        for h in range(1, heads):
            o = jnp.where(head_lanes[h], o_all[h * CHUNK:(h + 1) * CHUNK], o)
        o_ref[bi, pl.ds(q0, CHUNK), :] = o.astype(o_ref.dtype)
        return carry

    lax.fori_loop(0, bb * n_chunks, chunk, 0, unroll=2 if (bb * n_chunks) % 2 == 0 else 1)


def _attention(q_arr, k_arr, v_arr, rel_bias, *, q_col0, k_col0, v_col0, bb, heads, offsets, name):
    b, sq, _ = q_arr.shape
    sk = k_arr.shape[1]
    kw = KV_REACH + CHUNK
    assert sk >= kw and sq % CHUNK == 0
    hw = heads * HEAD_DIM
    n_groups = N_HEADS // heads
    n_case = len(offsets)
    width = -(-(kw + CHUNK - 1) // LANES) * LANES
    diag = _bias_diagonals(rel_bias, width, offsets, heads)
    qc, kc, vc = q_col0 // hw, k_col0 // hw, v_col0 // hw
    return pl.pallas_call(
        functools.partial(_attn_body, bb=bb, n_chunks=sq // CHUNK, kw=kw, heads=heads,
                          offsets=tuple(offsets)),
        grid=(n_groups, b // bb),
        in_specs=[
            pl.BlockSpec((bb, sq, hw), lambda g, i: (i, 0, qc + g)),
            pl.BlockSpec((bb, sk, hw), lambda g, i: (i, 0, kc + g)),
            pl.BlockSpec((bb, sk, hw), lambda g, i: (i, 0, vc + g)),
            pl.BlockSpec((1, n_case * heads, width), lambda g, i: (g, 0, 0)),
        ],
        out_specs=pl.BlockSpec((bb, sq, hw), lambda g, i: (i, 0, g)),
        out_shape=jax.ShapeDtypeStruct((b, sq, D_ATTN), BF16),
        scratch_shapes=[pltpu.VMEM((n_case, heads * CHUNK, kw), F32)],
        compiler_params=_params(2, 40),
        name=name,
    )(q_arr, k_arr, v_arr, diag)


def _log_sigmoid(x):
    return jnp.minimum(x, 0.0) - jnp.log1p(jnp.exp(-jnp.abs(x)))


def _rglru_body(xr_ref, gr_ref, c0_ref, h0_ref, cw_ref, cb_ref, wg_ref, brg_ref, big_ref, lam_ref,
                ob_ref, hl_ref, xbuf, hcar, abuf, ubuf, *, tt, cw):
    pad = SUBLANES
    t = pl.program_id(2)

    @pl.when(t == 0)
    def _():
        hcar[...] = h0_ref[0]
        xbuf[pad - 3:pad, :] = c0_ref[0]

    x = xr_ref[0].astype(F32)
    xbuf[pad:pad + tt, :] = x
    w = cw_ref[...]
    xc = cb_ref[...] + w[0:1] * xbuf[pad - 3:pad - 3 + tt, :]
    xc = xc + w[1:2] * xbuf[pad - 2:pad - 2 + tt, :]
    xc = xc + w[2:3] * xbuf[pad - 1:pad - 1 + tt, :]
    xc = xc + w[3:4] * x
    xbuf[pad - 3:pad, :] = xbuf[pad + tt - 3:pad + tt, :]

    xcb = xc.astype(BF16)
    r_parts, i_parts = [], []
    for n in range(cw // RNN_BLOCK):
        gts = jnp.dot(xcb[:, n * RNN_BLOCK:(n + 1) * RNN_BLOCK], wg_ref[n], preferred_element_type=F32)
        r_parts.append(gts[:, :RNN_BLOCK])
        i_parts.append(gts[:, RNN_BLOCK:])
    r = jax.nn.sigmoid(jnp.concatenate(r_parts, axis=1) + brg_ref[...])
    i = jax.nn.sigmoid(jnp.concatenate(i_parts, axis=1) + big_ref[...])
    log_a = r * (LRU_C * _log_sigmoid(lam_ref[...]))
    abuf[...] = jnp.exp(log_a)
    th = jnp.tanh(log_a)
    ubuf[...] = jnp.sqrt(-2.0 * th / (1.0 - th)) * (i * xc)

    row = lax.broadcasted_iota(jnp.int32, (SUBLANES, cw), 0)

    def group(g, h):
        r0 = pl.multiple_of(g * SUBLANES, SUBLANES)
        a = abuf[pl.ds(r0, SUBLANES), :]
        u = ubuf[pl.ds(r0, SUBLANES), :]
        for d in (1, 2, 4):
            keep = row >= d
            a_prev = jnp.where(keep, pltpu.roll(a, d, 0), 1.0)
            u_prev = jnp.where(keep, pltpu.roll(u, d, 0), 0.0)
            u = a * u_prev + u
            a = a * a_prev
        hg = u + a * h
        ubuf[pl.ds(r0, SUBLANES), :] = hg
        return jnp.broadcast_to(hg[SUBLANES - 1:SUBLANES, :], (SUBLANES, cw))

    h_in = jnp.broadcast_to(hcar[...], (SUBLANES, cw))
    h_out = lax.fori_loop(0, tt // SUBLANES, group, h_in)
    hcar[...] = h_out[0:1, :]
    hl_ref[0] = h_out[0:1, :]
    gate = jax.nn.gelu(gr_ref[0].astype(F32), approximate=True)
    ob_ref[0] = (ubuf[...] * gate).astype(ob_ref.dtype)


def _rglru(z3, conv0, h0, conv_w, conv_b, w_gates, b_rg, b_ig, lam, *, xr_col0, gr_col0, tt, cw, name):
    b, t, _ = z3.shape
    d_rnn = conv_w.shape[1]
    n_cg = d_rnn // cw
    blocks_per_cg = cw // RNN_BLOCK
    xc0, gc0 = xr_col0 // cw, gr_col0 // cw
    vec = pl.BlockSpec((1, cw), lambda c, i, s: (0, c))
    return pl.pallas_call(
        functools.partial(_rglru_body, tt=tt, cw=cw),
        grid=(n_cg, b, t // tt),
        in_specs=[
            pl.BlockSpec((1, tt, cw), lambda c, i, s: (i, s, xc0 + c)),
            pl.BlockSpec((1, tt, cw), lambda c, i, s: (i, s, gc0 + c)),
            pl.BlockSpec((1, CONV_WIDTH - 1, cw), lambda c, i, s: (i, 0, c)),
            pl.BlockSpec((1, 1, cw), lambda c, i, s: (i, 0, c)),
            pl.BlockSpec((CONV_WIDTH, cw), lambda c, i, s: (0, c)),
            vec,
            pl.BlockSpec((blocks_per_cg, RNN_BLOCK, 2 * RNN_BLOCK), lambda c, i, s: (c, 0, 0)),
            vec, vec, vec,
        ],
        out_specs=[
            pl.BlockSpec((1, tt, cw), lambda c, i, s: (i, s, c)),
            pl.BlockSpec((1, 1, cw), lambda c, i, s: (i, 0, c)),
        ],
        out_shape=[
            jax.ShapeDtypeStruct((b, t, d_rnn), BF16),
            jax.ShapeDtypeStruct((b, 1, d_rnn), F32),
        ],
        scratch_shapes=[
            pltpu.VMEM((SUBLANES + tt, cw), F32),
            pltpu.VMEM((1, cw), F32),
            pltpu.VMEM((tt, cw), F32),
            pltpu.VMEM((tt, cw), F32),
        ],
        compiler_params=_params(3, 40),
        name=name,
    )(z3, z3, conv0, h0, conv_w, conv_b, w_gates, b_rg, b_ig, lam)


def _merge_body(oa_ref, ob_ref, ga_ref, gb_ref, wa_ref, wb_ref, o_ref):
    ya = jnp.dot(oa_ref[...], wa_ref[...], preferred_element_type=F32)
    yb = jnp.dot(ob_ref[...], wb_ref[...], preferred_element_type=F32)
    ga = jax.nn.sigmoid(ga_ref[...].astype(F32))
    gb = jax.nn.sigmoid(gb_ref[...].astype(F32))
    o_ref[...] = (ga * ya + gb * yb).astype(o_ref.dtype)


def _gate_merge(o_a, o_b, z2, w_a, w_b, *, ga_col0, gb_col0, tm, tn, name):
    m = o_a.shape[0]
    d = w_a.shape[1]
    ga0, gb0 = ga_col0 // tn, gb_col0 // tn
    return pl.pallas_call(
        _merge_body,
        grid=(m // tm, d // tn),
        in_specs=[
            pl.BlockSpec((tm, o_a.shape[1]), lambda i, j: (i, 0)),
            pl.BlockSpec((tm, o_b.shape[1]), lambda i, j: (i, 0)),
            pl.BlockSpec((tm, tn), lambda i, j: (i, ga0 + j)),
            pl.BlockSpec((tm, tn), lambda i, j: (i, gb0 + j)),
            pl.BlockSpec((w_a.shape[0], tn), lambda i, j: (0, j)),
            pl.BlockSpec((w_b.shape[0], tn), lambda i, j: (0, j)),
        ],
        out_specs=pl.BlockSpec((tm, tn), lambda i, j: (i, j)),
        out_shape=jax.ShapeDtypeStruct((m, d), BF16),
        compiler_params=_params(2, 48),
        name=name,
    )(o_a, o_b, z2, z2, w_a, w_b)


def _out_norm_body(m_ref, w_ref, x_ref, g_ref, o_ref):
    y = jnp.dot(m_ref[...], w_ref[...], preferred_element_type=F32)
    o_ref[...] = x_ref[...] + _rmsnorm_rows(y, g_ref[...])


def _out_norm(merged, w_out, x2d, g, *, tm, name):
    m, d = x2d.shape
    return pl.pallas_call(
        _out_norm_body,
        grid=(m // tm,),
        in_specs=[
            pl.BlockSpec((tm, d), lambda i: (i, 0)),
            pl.BlockSpec((d, d), lambda i: (0, 0)),
            pl.BlockSpec((tm, d), lambda i: (i, 0)),
            pl.BlockSpec((1, d), lambda i: (0, 0)),
        ],
        out_specs=pl.BlockSpec((tm, d), lambda i: (i, 0)),
        out_shape=jax.ShapeDtypeStruct((m, d), F32),
        compiler_params=_params(1, 48),
        name=name,
    )(merged, w_out, x2d, g)


def _ffn_body(x_ref, g1_ref, w1_ref, w2_ref, g2_ref, o_ref, xn_ref):
    f = pl.program_id(1)

    @pl.when(f == 0)
    def _():
        xn_ref[...] = _rmsnorm_rows(x_ref[...], g1_ref[...]).astype(BF16)

    hid = jnp.dot(xn_ref[...], w1_ref[...], preferred_element_type=F32)
    hid = jnp.square(jnp.maximum(hid, 0.0)).astype(BF16)
    part = jnp.dot(hid, w2_ref[...], preferred_element_type=F32)

    @pl.when(f == 0)
    def _():
        o_ref[...] = part

    @pl.when(f > 0)
    def _():
        o_ref[...] += part

    @pl.when(f == pl.num_programs(1) - 1)
    def _():
        o_ref[...] = x_ref[...] + _rmsnorm_rows(o_ref[...], g2_ref[...])


def _ffn(x2d, g1, w1, w2, g2, *, tm, tf, name):
    m, d = x2d.shape
    d_ff = w1.shape[1]
    return pl.pallas_call(
        _ffn_body,
        grid=(m // tm, d_ff // tf),
        in_specs=[
            pl.BlockSpec((tm, d), lambda i, f: (i, 0)),
            pl.BlockSpec((1, d), lambda i, f: (0, 0)),
            pl.BlockSpec((d, tf), lambda i, f: (0, f)),
            pl.BlockSpec((tf, d), lambda i, f: (f, 0)),
            pl.BlockSpec((1, d), lambda i, f: (0, 0)),
        ],
        out_specs=pl.BlockSpec((tm, d), lambda i, f: (i, 0)),
        out_shape=jax.ShapeDtypeStruct((m, d), F32),
        scratch_shapes=[pltpu.VMEM((tm, d), BF16)],
        compiler_params=_params(2, 48),
        name=name,
    )(x2d, g1, w1, w2, g2)


def _pick_tile(n, target):
    t = min(n, target)
    while n % t:
        t //= 2
    return t


def _layer(x, cache_k, cache_v, conv_state, h_state, wts, tag):
    b, t, d = x.shape
    m = b * t
    x2d = x.reshape(m, d)
    d_rnn = wts["conv_w"].shape[1]
    n_in = wts["w_in"].shape[1]
    col_k, col_v, col_xr = D_ATTN, 2 * D_ATTN, 3 * D_ATTN
    col_gr = col_xr + d_rnn
    col_ga = col_gr + d_rnn
    col_gb = col_ga + d

    tm = _pick_tile(m, 1024)
    tn = 512
    z2 = _in_proj(x2d, wts["pre_mix_g"], wts["w_in"], tm=tm, tn=tn, n_row_tiles=m // tm,
                  row_tile_of=lambda i: i, col_blk0=0, n_col_tiles=n_in // tn, out_dtype=BF16,
                  name=f"in_proj_{tag}")
    z3 = z2.reshape(b, t, n_in)

    keep = min(KV_REACH, t)
    if keep == t:
        ts, n_state_tiles, state_tile_of = _pick_tile(m, 512), m // _pick_tile(m, 512), lambda i: i
    else:
        tiles_per_seq = t // keep
        ts, n_state_tiles = keep, b
        state_tile_of = lambda i: i * tiles_per_seq + tiles_per_seq - 1
    state = _in_proj(x2d, wts["pre_mix_g"], wts["w_in"], tm=ts, tn=tn, n_row_tiles=n_state_tiles,
                     row_tile_of=state_tile_of,
                     col_blk0=col_k // tn, n_col_tiles=(col_gr - col_k) // tn, out_dtype=F32,
                     name=f"state_proj_{tag}")
    state = state.reshape(b, keep, col_gr - col_k)
    new_k = state[:, :, :D_ATTN].reshape(b, keep, N_HEADS, HEAD_DIM)
    new_v = state[:, :, D_ATTN:2 * D_ATTN].reshape(b, keep, N_HEADS, HEAD_DIM)
    conv_tail = state[:, keep - (CONV_WIDTH - 1):, 2 * D_ATTN:]

    heads = 2
    if cache_k is None:
        offsets = [i * CHUNK for i in range(PAST_CHUNKS + 1)]
        o_a = _attention(z3, z3, z3, wts["rel_bias"], q_col0=0, k_col0=col_k, v_col0=col_v, bb=1,
                         heads=heads, offsets=offsets, name=f"attn_{tag}")
        conv0 = jnp.zeros((b, CONV_WIDTH - 1, d_rnn), F32)
        h0 = jnp.zeros((b, 1, d_rnn), F32)
    else:
        n_cached = cache_k.shape[1]
        k_all = jnp.concatenate([cache_k.reshape(b, n_cached, D_ATTN).astype(BF16),
                                 z3[:, :, col_k:col_k + D_ATTN]], axis=1)
        v_all = jnp.concatenate([cache_v.reshape(b, n_cached, D_ATTN).astype(BF16),
                                 z3[:, :, col_v:col_v + D_ATTN]], axis=1)
        assert t == CHUNK and n_cached == KV_REACH
        o_a = _attention(z3, k_all, v_all, wts["rel_bias"], q_col0=0, k_col0=0, v_col0=0,
                         bb=_pick_tile(b, 8), heads=heads, offsets=[n_cached], name=f"attn_{tag}")
        conv0 = conv_state.astype(F32)
        h0 = h_state.astype(F32).reshape(b, 1, d_rnn)

    o_b, h_last = _rglru(z3, conv0, h0, wts["conv_w"], wts["conv_b"], wts["w_gates"], wts["b_rg"],
                         wts["b_ig"], wts["lru_lambda"], xr_col0=col_xr, gr_col0=col_gr,
                         tt=_pick_tile(t, 256), cw=1024, name=f"rglru_{tag}")

    merged = _gate_merge(o_a.reshape(m, D_ATTN), o_b.reshape(m, d_rnn), z2, wts["w_attn_up"],
                         wts["w_rnn_up"], ga_col0=col_ga, gb_col0=col_gb, tm=tm, tn=tn,
                         name=f"gate_merge_{tag}")
    tm2 = _pick_tile(m, 512)
    x1 = _out_norm(merged, wts["w_out"], x2d, wts["post_mix_g"], tm=tm2, name=f"out_norm_{tag}")
    y = _ffn(x1, wts["pre_ffn_g"], wts["w_ff1"], wts["w_ff2"], wts["post_ffn_g"], tm=tm2, tf=1024,
             name=f"ffn_{tag}")
    return y.reshape(b, t, d), new_k, new_v, conv_tail, h_last.reshape(b, d_rnn)


def kernel(x_prompt, x_sample, cache_k, cache_v, state_conv, state_h, pre_mix_g, w_in, rel_bias, conv_w, conv_b, w_rg, b_rg, w_ig, b_ig, lru_lambda, w_attn_up, w_rnn_up, w_out, post_mix_g, pre_ffn_g, w_ff1, w_ff2, post_ffn_g):
    depth = w_in.shape[0]
    y_p, y_s = x_prompt, x_sample
    outs_p, outs_s = [], []
    for l in range(depth):
        wts = {
            "pre_mix_g": pre_mix_g[l][None], "post_mix_g": post_mix_g[l][None],
            "pre_ffn_g": pre_ffn_g[l][None], "post_ffn_g": post_ffn_g[l][None],
            "w_in": w_in[l].astype(BF16), "rel_bias": rel_bias[l],
            "conv_w": conv_w[l], "conv_b": conv_b[l][None],
            "w_gates": jnp.concatenate([w_rg[l], w_ig[l]], axis=-1).astype(BF16),
            "b_rg": b_rg[l][None], "b_ig": b_ig[l][None], "lru_lambda": lru_lambda[l][None],
            "w_attn_up": w_attn_up[l].astype(BF16), "w_rnn_up": w_rnn_up[l].astype(BF16),
            "w_out": w_out[l].astype(BF16), "w_ff1": w_ff1[l].astype(BF16), "w_ff2": w_ff2[l].astype(BF16),
        }
        y_p, *st_p = _layer(y_p, None, None, None, None, wts, f"p{l}")
        y_s, *st_s = _layer(y_s, cache_k[l], cache_v[l], state_conv[l], state_h[l], wts, f"s{l}")
        outs_p.append(st_p)
        outs_s.append(st_s)
    stack = lambda outs, i: jnp.stack([o[i] for o in outs])
    return (y_p, y_s,
            stack(outs_p, 0), stack(outs_p, 1), stack(outs_p, 2), stack(outs_p, 3),
            stack(outs_s, 0), stack(outs_s, 1), stack(outs_s, 2), stack(outs_s, 3))
```

```python
import functools

import jax
import jax.numpy as jnp
from jax import lax
from jax.experimental import pallas as pl
from jax.experimental.pallas import tpu as pltpu

F32 = jnp.float32
BF16 = jnp.bfloat16

CHUNK = 64
PAST_CHUNKS = 8
KV_REACH = PAST_CHUNKS * CHUNK
N_HEADS = 16
HEAD_DIM = 64
D_ATTN = N_HEADS * HEAD_DIM
MAX_REL = 256
RNN_BLOCK = 128
CONV_WIDTH = 4
LRU_C = 8.0
EPS = 1e-6
NEG_INF = -1e30

LANES = 128
SUBLANES = 8
MIB = 1024 * 1024


def _params(n_axes, vmem_mib):
    return pltpu.CompilerParams(
        dimension_semantics=("arbitrary",) * n_axes,
        vmem_limit_bytes=vmem_mib * MIB,
    )


def _rmsnorm_rows(x, g):
    ms = jnp.mean(x * x, axis=-1, keepdims=True)
    return x * lax.rsqrt(ms + EPS) * g


def _in_proj_body(x_ref, g_ref, w_ref, o_ref, xn_ref):
    @pl.when(pl.program_id(1) == 0)
    def _():
        xn_ref[...] = _rmsnorm_rows(x_ref[...], g_ref[...]).astype(BF16)

    o_ref[...] = jnp.dot(xn_ref[...], w_ref[...], preferred_element_type=F32).astype(o_ref.dtype)


def _in_proj(x2d, g, w, *, tm, tn, n_row_tiles, row_tile_of, col_blk0, n_col_tiles, out_dtype, name):
    d = x2d.shape[1]
    return pl.pallas_call(
        _in_proj_body,
        grid=(n_row_tiles, n_col_tiles),
        in_specs=[
            pl.BlockSpec((tm, d), lambda i, j: (row_tile_of(i), 0)),
            pl.BlockSpec((1, d), lambda i, j: (0, 0)),
            pl.BlockSpec((d, tn), lambda i, j: (0, col_blk0 + j)),
        ],
        out_specs=pl.BlockSpec((tm, tn), lambda i, j: (i, j)),
        out_shape=jax.ShapeDtypeStruct((n_row_tiles * tm, n_col_tiles * tn), out_dtype),
        scratch_shapes=[pltpu.VMEM((tm, d), BF16)],
        compiler_params=_params(2, 48),
        name=name,
    )(x2d, g, w)


def _bias_diagonals(rel_bias, width, offsets, heads):
    x = jnp.arange(width)
    rows = [rel_bias[:, jnp.clip(off + CHUNK - 1 - x, -MAX_REL, MAX_REL) + MAX_REL] for off in offsets]
    g = jnp.stack(rows, axis=1).astype(F32)
    n_groups = rel_bias.shape[0] // heads
    g = g.reshape(n_groups, heads, len(offsets), width)
    return jnp.swapaxes(g, 1, 2).reshape(n_groups, len(offsets) * heads, width)


def _attn_body(q_ref, k_ref, v_ref, g_ref, o_ref, tab_ref, s_ref, p_ref, l_ref, *,
               bb, n_chunks, kw, heads, offsets, unit):
    width = g_ref.shape[2]
    hw = heads * HEAD_DIM

    @pl.when(pl.program_id(1) == 0)
    def _():
        r = lax.broadcasted_iota(jnp.int32, (CHUNK, kw), 0)
        j = lax.broadcasted_iota(jnp.int32, (CHUNK, kw), 1)
        for c, off in enumerate(offsets):
            cq = (r + off) // CHUNK
            ck = j // CHUNK
            valid = (ck <= cq) & (ck >= cq - PAST_CHUNKS)
            for h in range(heads):
                g = jnp.broadcast_to(g_ref[0, c * heads + h:c * heads + h + 1, :], (CHUNK, width))
                t = pltpu.roll(g, width - (CHUNK - 1), 1, stride=1, stride_axis=0)
                tab_ref[c, h * CHUNK:(h + 1) * CHUNK, :] = jnp.where(valid, t[:, :kw], NEG_INF)

    lane = lax.broadcasted_iota(jnp.int32, (1, hw), 1)
    head_lanes = [(lane >= h * HEAD_DIM) & (lane < (h + 1) * HEAD_DIM) for h in range(heads)]
    rows = heads * CHUNK
    n_units = bb * n_chunks // unit

    def locate(u, j):
        idx = u * unit + j
        bi, c = (0, idx) if bb == 1 else (idx // n_chunks, idx % n_chunks)
        q0 = pl.multiple_of(c * CHUNK, CHUNK)
        ks = pl.multiple_of(jnp.maximum(c * CHUNK - KV_REACH, 0), CHUNK)
        return bi, c, q0, ks

    def scores(u, slot):
        for j in range(unit):
            bi, c, q0, ks = locate(u, j)
            q = q_ref[bi, pl.ds(q0, CHUNK), :]
            qs = jnp.concatenate([jnp.where(m, q, jnp.zeros_like(q)) for m in head_lanes], axis=0)
            qs = qs * jnp.asarray(HEAD_DIM ** -0.5, q.dtype)
            kb = k_ref[bi, pl.ds(ks, kw), :]
            s = lax.dot_general(qs, kb, (((1,), (1,)), ((), ())), preferred_element_type=F32)
            s_ref[slot, j * rows:(j + 1) * rows, :] = s + tab_ref[jnp.minimum(c, len(offsets) - 1)]

    def softmax(slot):
        s = s_ref[slot]
        p = jnp.exp(s - jnp.max(s, axis=-1, keepdims=True))
        p_ref[slot] = p.astype(p_ref.dtype)
        l_ref[slot] = 1.0 / jnp.sum(p, axis=-1, keepdims=True)

    def output(u, slot):
        for j in range(unit):
            bi, c, q0, ks = locate(u, j)
            vb = v_ref[bi, pl.ds(ks, kw), :]
            o_all = jnp.dot(p_ref[slot, j * rows:(j + 1) * rows, :], vb, preferred_element_type=F32)
            o_all = o_all * l_ref[slot, j * rows:(j + 1) * rows, :]
            o = o_all[0:CHUNK]
            for h in range(1, heads):
                o = jnp.where(head_lanes[h], o_all[h * CHUNK:(h + 1) * CHUNK], o)
            o_ref[bi, pl.ds(q0, CHUNK), :] = o.astype(o_ref.dtype)

    def step(i, slot, do_scores, do_softmax, do_output):
        if do_scores:
            scores(i, slot)
        if do_output:
            output(i - 2, slot)
        if do_softmax:
            softmax(1 - slot)

    for i in range(2):
        step(i, i % 2, i < n_units, 1 <= i <= n_units, False)
    n_pairs = max(n_units - 2, 0) // 2
    if n_pairs:
        def body(t, carry):
            i = 2 + 2 * t
            step(i, 0, True, True, True)
            step(i + 1, 1, True, True, True)
            return carry
        lax.fori_loop(0, n_pairs, body, 0)
    for i in range(2 + 2 * n_pairs, n_units + 2):
        step(i, i % 2, i < n_units, 1 <= i <= n_units, i >= 2)


def _attention(q_arr, k_arr, v_arr, rel_bias, *, q_col0, k_col0, v_col0, bb, heads, offsets, name):
    b, sq, _ = q_arr.shape
    sk = k_arr.shape[1]
    kw = KV_REACH + CHUNK
    assert sk >= kw and sq % CHUNK == 0
    hw = heads * HEAD_DIM
    n_groups = N_HEADS // heads
    n_case = len(offsets)
    width = -(-(kw + CHUNK - 1) // LANES) * LANES
    diag = _bias_diagonals(rel_bias, width, offsets, heads)
    qc, kc, vc = q_col0 // hw, k_col0 // hw, v_col0 // hw
    unit = 2
    assert (bb * sq // CHUNK) % unit == 0
    rows = unit * heads * CHUNK
    return pl.pallas_call(
        functools.partial(_attn_body, bb=bb, n_chunks=sq // CHUNK, kw=kw, heads=heads,
                          offsets=tuple(offsets), unit=unit),
        grid=(n_groups, b // bb),
        in_specs=[
            pl.BlockSpec((bb, sq, hw), lambda g, i: (i, 0, qc + g)),
            pl.BlockSpec((bb, sk, hw), lambda g, i: (i, 0, kc + g)),
            pl.BlockSpec((bb, sk, hw), lambda g, i: (i, 0, vc + g)),
            pl.BlockSpec((1, n_case * heads, width), lambda g, i: (g, 0, 0)),
        ],
        out_specs=pl.BlockSpec((bb, sq, hw), lambda g, i: (i, 0, g)),
        out_shape=jax.ShapeDtypeStruct((b, sq, D_ATTN), BF16),
        scratch_shapes=[
            pltpu.VMEM((n_case, heads * CHUNK, kw), F32),
            pltpu.VMEM((2, rows, kw), F32),
            pltpu.VMEM((2, rows, kw), BF16),
            pltpu.VMEM((2, rows, 1), F32),
        ],
        compiler_params=_params(2, 40),
        name=name,
    )(q_arr, k_arr, v_arr, diag)


def _log_sigmoid(x):
    return jnp.minimum(x, 0.0) - jnp.log1p(jnp.exp(-jnp.abs(x)))


def _sigmoid(x):
    return 0.5 * jnp.tanh(0.5 * x) + 0.5


def _rglru_body(xr_ref, gr_ref, c0_ref, h0_ref, cw_ref, cb_ref, wg_ref, brg_ref, big_ref, lam_ref,
                ob_ref, hl_ref, xbuf, hcar, abuf, ubuf, *, tt, cw):
    pad = SUBLANES
    t = pl.program_id(2)

    @pl.when(t == 0)
    def _():
        hcar[...] = h0_ref[0]
        xbuf[pad - 3:pad, :] = c0_ref[0]

    x = xr_ref[0].astype(F32)
    xbuf[pad:pad + tt, :] = x
    w = cw_ref[...]
    xc = cb_ref[...] + w[0:1] * xbuf[pad - 3:pad - 3 + tt, :]
    xc = xc + w[1:2] * xbuf[pad - 2:pad - 2 + tt, :]
    xc = xc + w[2:3] * xbuf[pad - 1:pad - 1 + tt, :]
    xc = xc + w[3:4] * x
    xbuf[pad - 3:pad, :] = xbuf[pad + tt - 3:pad + tt, :]

    xcb = xc.astype(BF16)
    r_parts, i_parts = [], []
    for n in range(cw // RNN_BLOCK):
        gts = jnp.dot(xcb[:, n * RNN_BLOCK:(n + 1) * RNN_BLOCK], wg_ref[n], preferred_element_type=F32)
        r_parts.append(gts[:, :RNN_BLOCK])
        i_parts.append(gts[:, RNN_BLOCK:])
    r = _sigmoid(jnp.concatenate(r_parts, axis=1) + brg_ref[...])
    i = _sigmoid(jnp.concatenate(i_parts, axis=1) + big_ref[...])
    log_a = r * (LRU_C * _log_sigmoid(lam_ref[...]))
    abuf[...] = jnp.exp(log_a)
    w = -jnp.tanh(log_a)
    one_minus_a2 = 2.0 * w / (1.0 + w)
    root = jnp.where(one_minus_a2 > 0.0, one_minus_a2 * lax.rsqrt(one_minus_a2), 0.0)
    ubuf[...] = root * (i * xc)

    row = lax.broadcasted_iota(jnp.int32, (SUBLANES, cw), 0)

    def group(g, h):
        r0 = pl.multiple_of(g * SUBLANES, SUBLANES)
        a = abuf[pl.ds(r0, SUBLANES), :]
        u = ubuf[pl.ds(r0, SUBLANES), :]
        for d in (1, 2, 4):
            keep = row >= d
            a_prev = jnp.where(keep, pltpu.roll(a, d, 0), 1.0)
            u_prev = jnp.where(keep, pltpu.roll(u, d, 0), 0.0)
            u = a * u_prev + u
            a = a * a_prev
        hg = u + a * h
        ubuf[pl.ds(r0, SUBLANES), :] = hg
        return jnp.broadcast_to(hg[SUBLANES - 1:SUBLANES, :], (SUBLANES, cw))

    h_in = jnp.broadcast_to(hcar[...], (SUBLANES, cw))
    h_out = lax.fori_loop(0, tt // SUBLANES, group, h_in)
    hcar[...] = h_out[0:1, :]
    hl_ref[0] = h_out[0:1, :]
    gate = jax.nn.gelu(gr_ref[0].astype(F32), approximate=True)
    ob_ref[0] = (ubuf[...] * gate).astype(ob_ref.dtype)


def _rglru(z3, conv0, h0, conv_w, conv_b, w_gates, b_rg, b_ig, lam, *, xr_col0, gr_col0, tt, cw, name):
    b, t, _ = z3.shape
    d_rnn = conv_w.shape[1]
    n_cg = d_rnn // cw
    blocks_per_cg = cw // RNN_BLOCK
    xc0, gc0 = xr_col0 // cw, gr_col0 // cw
    vec = pl.BlockSpec((1, cw), lambda c, i, s: (0, c))
    return pl.pallas_call(
        functools.partial(_rglru_body, tt=tt, cw=cw),
        grid=(n_cg, b, t // tt),
        in_specs=[
            pl.BlockSpec((1, tt, cw), lambda c, i, s: (i, s, xc0 + c)),
            pl.BlockSpec((1, tt, cw), lambda c, i, s: (i, s, gc0 + c)),
            pl.BlockSpec((1, CONV_WIDTH - 1, cw), lambda c, i, s: (i, 0, c)),
            pl.BlockSpec((1, 1, cw), lambda c, i, s: (i, 0, c)),
            pl.BlockSpec((CONV_WIDTH, cw), lambda c, i, s: (0, c)),
            vec,
            pl.BlockSpec((blocks_per_cg, RNN_BLOCK, 2 * RNN_BLOCK), lambda c, i, s: (c, 0, 0)),
            vec, vec, vec,
        ],
        out_specs=[
            pl.BlockSpec((1, tt, cw), lambda c, i, s: (i, s, c)),
            pl.BlockSpec((1, 1, cw), lambda c, i, s: (i, 0, c)),
        ],
        out_shape=[
            jax.ShapeDtypeStruct((b, t, d_rnn), BF16),
            jax.ShapeDtypeStruct((b, 1, d_rnn), F32),
        ],
        scratch_shapes=[
            pltpu.VMEM((SUBLANES + tt, cw), F32),
            pltpu.VMEM((1, cw), F32),
            pltpu.VMEM((tt, cw), F32),
            pltpu.VMEM((tt, cw), F32),
        ],
        compiler_params=_params(3, 40),
        name=name,
    )(z3, z3, conv0, h0, conv_w, conv_b, w_gates, b_rg, b_ig, lam)


def _merge_body(oa_ref, ob_ref, ga_ref, gb_ref, wa_ref, wb_ref, o_ref):
    ya = jnp.dot(oa_ref[...], wa_ref[...], preferred_element_type=F32)
    yb = jnp.dot(ob_ref[...], wb_ref[...], preferred_element_type=F32)
    ga = jax.nn.sigmoid(ga_ref[...].astype(F32))
    gb = jax.nn.sigmoid(gb_ref[...].astype(F32))
    o_ref[...] = (ga * ya + gb * yb).astype(o_ref.dtype)


def _gate_merge(o_a, o_b, z2, w_a, w_b, *, ga_col0, gb_col0, tm, tn, name):
    m = o_a.shape[0]
    d = w_a.shape[1]
    ga0, gb0 = ga_col0 // tn, gb_col0 // tn
    return pl.pallas_call(
        _merge_body,
        grid=(m // tm, d // tn),
        in_specs=[
            pl.BlockSpec((tm, o_a.shape[1]), lambda i, j: (i, 0)),
            pl.BlockSpec((tm, o_b.shape[1]), lambda i, j: (i, 0)),
            pl.BlockSpec((tm, tn), lambda i, j: (i, ga0 + j)),
            pl.BlockSpec((tm, tn), lambda i, j: (i, gb0 + j)),
            pl.BlockSpec((w_a.shape[0], tn), lambda i, j: (0, j)),
            pl.BlockSpec((w_b.shape[0], tn), lambda i, j: (0, j)),
        ],
        out_specs=pl.BlockSpec((tm, tn), lambda i, j: (i, j)),
        out_shape=jax.ShapeDtypeStruct((m, d), BF16),
        compiler_params=_params(2, 48),
        name=name,
    )(o_a, o_b, z2, z2, w_a, w_b)


def _out_norm_body(m_ref, w_ref, x_ref, g_ref, o_ref):
    y = jnp.dot(m_ref[...], w_ref[...], preferred_element_type=F32)
    o_ref[...] = x_ref[...] + _rmsnorm_rows(y, g_ref[...])


def _out_norm(merged, w_out, x2d, g, *, tm, name):
    m, d = x2d.shape
    return pl.pallas_call(
        _out_norm_body,
        grid=(m // tm,),
        in_specs=[
            pl.BlockSpec((tm, d), lambda i: (i, 0)),
            pl.BlockSpec((d, d), lambda i: (0, 0)),
            pl.BlockSpec((tm, d), lambda i: (i, 0)),
            pl.BlockSpec((1, d), lambda i: (0, 0)),
        ],
        out_specs=pl.BlockSpec((tm, d), lambda i: (i, 0)),
        out_shape=jax.ShapeDtypeStruct((m, d), F32),
        compiler_params=_params(1, 48),
        name=name,
    )(merged, w_out, x2d, g)


def _ffn_body(x_ref, g1_ref, w1_ref, w2_ref, g2_ref, o_ref, xn_ref):
    f = pl.program_id(1)

    @pl.when(f == 0)
    def _():
        xn_ref[...] = _rmsnorm_rows(x_ref[...], g1_ref[...]).astype(BF16)

    hid = jnp.dot(xn_ref[...], w1_ref[...], preferred_element_type=F32)
    hid = jnp.square(jnp.maximum(hid, 0.0)).astype(BF16)
    part = jnp.dot(hid, w2_ref[...], preferred_element_type=F32)

    @pl.when(f == 0)
    def _():
        o_ref[...] = part

    @pl.when(f > 0)
    def _():
        o_ref[...] += part

    @pl.when(f == pl.num_programs(1) - 1)
    def _():
        o_ref[...] = x_ref[...] + _rmsnorm_rows(o_ref[...], g2_ref[...])


def _ffn(x2d, g1, w1, w2, g2, *, tm, tf, name):
    m, d = x2d.shape
    d_ff = w1.shape[1]
    return pl.pallas_call(
        _ffn_body,
        grid=(m // tm, d_ff // tf),
        in_specs=[
            pl.BlockSpec((tm, d), lambda i, f: (i, 0)),
            pl.BlockSpec((1, d), lambda i, f: (0, 0)),
            pl.BlockSpec((d, tf), lambda i, f: (0, f)),
            pl.BlockSpec((tf, d), lambda i, f: (f, 0)),
            pl.BlockSpec((1, d), lambda i, f: (0, 0)),
        ],
        out_specs=pl.BlockSpec((tm, d), lambda i, f: (i, 0)),
        out_shape=jax.ShapeDtypeStruct((m, d), F32),
        scratch_shapes=[pltpu.VMEM((tm, d), BF16)],
        compiler_params=_params(2, 48),
        name=name,
    )(x2d, g1, w1, w2, g2)


def _pick_tile(n, target):
    t = min(n, target)
    while n % t:
        t //= 2
    return t


def _layer(x, cache_k, cache_v, conv_state, h_state, wts, tag):
    b, t, d = x.shape
    m = b * t
    x2d = x.reshape(m, d)
    d_rnn = wts["conv_w"].shape[1]
    n_in = wts["w_in"].shape[1]
    col_k, col_v, col_xr = D_ATTN, 2 * D_ATTN, 3 * D_ATTN
    col_gr = col_xr + d_rnn
    col_ga = col_gr + d_rnn
    col_gb = col_ga + d

    tm = _pick_tile(m, 1024)
    tn = 512
    z2 = _in_proj(x2d, wts["pre_mix_g"], wts["w_in"], tm=tm, tn=tn, n_row_tiles=m // tm,
                  row_tile_of=lambda i: i, col_blk0=0, n_col_tiles=n_in // tn, out_dtype=BF16,
                  name=f"in_proj_{tag}")
    z3 = z2.reshape(b, t, n_in)

    keep = min(KV_REACH, t)
    if keep == t:
        ts, n_state_tiles, state_tile_of = _pick_tile(m, 512), m // _pick_tile(m, 512), lambda i: i
    else:
        tiles_per_seq = t // keep
        ts, n_state_tiles = keep, b
        state_tile_of = lambda i: i * tiles_per_seq + tiles_per_seq - 1
    state = _in_proj(x2d, wts["pre_mix_g"], wts["w_in"], tm=ts, tn=tn, n_row_tiles=n_state_tiles,
                     row_tile_of=state_tile_of,
                     col_blk0=col_k // tn, n_col_tiles=(col_gr - col_k) // tn, out_dtype=F32,
                     name=f"state_proj_{tag}")
    state = state.reshape(b, keep, col_gr - col_k)
    new_k = state[:, :, :D_ATTN].reshape(b, keep, N_HEADS, HEAD_DIM)
    new_v = state[:, :, D_ATTN:2 * D_ATTN].reshape(b, keep, N_HEADS, HEAD_DIM)
    conv_tail = state[:, keep - (CONV_WIDTH - 1):, 2 * D_ATTN:]

    heads = 2
    if cache_k is None:
        offsets = [i * CHUNK for i in range(PAST_CHUNKS + 1)]
        o_a = _attention(z3, z3, z3, wts["rel_bias"], q_col0=0, k_col0=col_k, v_col0=col_v, bb=1,
                         heads=heads, offsets=offsets, name=f"attn_{tag}")
        conv0 = jnp.zeros((b, CONV_WIDTH - 1, d_rnn), F32)
        h0 = jnp.zeros((b, 1, d_rnn), F32)
    else:
        n_cached = cache_k.shape[1]
        k_all = jnp.concatenate([cache_k.reshape(b, n_cached, D_ATTN).astype(BF16),
                                 z3[:, :, col_k:col_k + D_ATTN]], axis=1)
        v_all = jnp.concatenate([cache_v.reshape(b, n_cached, D_ATTN).astype(BF16),
                                 z3[:, :, col_v:col_v + D_ATTN]], axis=1)
        assert t == CHUNK and n_cached == KV_REACH
        o_a = _attention(z3, k_all, v_all, wts["rel_bias"], q_col0=0, k_col0=0, v_col0=0,
                         bb=_pick_tile(b, 8), heads=heads, offsets=[n_cached], name=f"attn_{tag}")
        conv0 = conv_state.astype(F32)
        h0 = h_state.astype(F32).reshape(b, 1, d_rnn)

    o_b, h_last = _rglru(z3, conv0, h0, wts["conv_w"], wts["conv_b"], wts["w_gates"], wts["b_rg"],
                         wts["b_ig"], wts["lru_lambda"], xr_col0=col_xr, gr_col0=col_gr,
                         tt=_pick_tile(t, 256), cw=1024, name=f"rglru_{tag}")

    merged = _gate_merge(o_a.reshape(m, D_ATTN), o_b.reshape(m, d_rnn), z2, wts["w_attn_up"],
                         wts["w_rnn_up"], ga_col0=col_ga, gb_col0=col_gb, tm=tm, tn=tn,
                         name=f"gate_merge_{tag}")
    tm2 = _pick_tile(m, 512)
    x1 = _out_norm(merged, wts["w_out"], x2d, wts["post_mix_g"], tm=tm2, name=f"out_norm_{tag}")
    y = _ffn(x1, wts["pre_ffn_g"], wts["w_ff1"], wts["w_ff2"], wts["post_ffn_g"], tm=tm2, tf=1024,
             name=f"ffn_{tag}")
    return y.reshape(b, t, d), new_k, new_v, conv_tail, h_last.reshape(b, d_rnn)


def kernel(x_prompt, x_sample, cache_k, cache_v, state_conv, state_h, pre_mix_g, w_in, rel_bias, conv_w, conv_b, w_rg, b_rg, w_ig, b_ig, lru_lambda, w_attn_up, w_rnn_up, w_out, post_mix_g, pre_ffn_g, w_ff1, w_ff2, post_ffn_g):
    depth = w_in.shape[0]
    y_p, y_s = x_prompt, x_sample
    outs_p, outs_s = [], []
    for l in range(depth):
        wts = {
            "pre_mix_g": pre_mix_g[l][None], "post_mix_g": post_mix_g[l][None],
            "pre_ffn_g": pre_ffn_g[l][None], "post_ffn_g": post_ffn_g[l][None],
            "w_in": w_in[l].astype(BF16), "rel_bias": rel_bias[l],
            "conv_w": conv_w[l], "conv_b": conv_b[l][None],
            "w_gates": jnp.concatenate([w_rg[l], w_ig[l]], axis=-1).astype(BF16),
            "b_rg": b_rg[l][None], "b_ig": b_ig[l][None], "lru_lambda": lru_lambda[l][None],
            "w_attn_up": w_attn_up[l].astype(BF16), "w_rnn_up": w_rnn_up[l].astype(BF16),
            "w_out": w_out[l].astype(BF16), "w_ff1": w_ff1[l].astype(BF16), "w_ff2": w_ff2[l].astype(BF16),
        }
        y_p, *st_p = _layer(y_p, None, None, None, None, wts, f"p{l}")
        y_s, *st_s = _layer(y_s, cache_k[l], cache_v[l], state_conv[l], state_h[l], wts, f"s{l}")
        outs_p.append(st_p)
        outs_s.append(st_s)
    stack = lambda outs, i: jnp.stack([o[i] for o in outs])
    return (y_p, y_s,
            stack(outs_p, 0), stack(outs_p, 1), stack(outs_p, 2), stack(outs_p, 3),
            stack(outs_s, 0), stack(outs_s, 1), stack(outs_s, 2), stack(outs_s, 3))
```

```python
import functools

import jax
import jax.numpy as jnp
from jax import lax
from jax.experimental import pallas as pl
from jax.experimental.pallas import tpu as pltpu

F32 = jnp.float32
BF16 = jnp.bfloat16

CHUNK = 64
PAST_CHUNKS = 8
KV_REACH = PAST_CHUNKS * CHUNK
N_HEADS = 16
HEAD_DIM = 64
D_ATTN = N_HEADS * HEAD_DIM
MAX_REL = 256
RNN_BLOCK = 128
CONV_WIDTH = 4
LRU_C = 8.0
EPS = 1e-6
NEG_INF = -1e30

LANES = 128
SUBLANES = 8
MIB = 1024 * 1024


def _params(n_axes, vmem_mib):
    return pltpu.CompilerParams(
        dimension_semantics=("arbitrary",) * n_axes,
        vmem_limit_bytes=vmem_mib * MIB,
    )


def _rmsnorm_rows(x, g):
    ms = jnp.mean(x * x, axis=-1, keepdims=True)
    return x * lax.rsqrt(ms + EPS) * g


def _in_proj_body(x_ref, g_ref, w_ref, o_ref, xn_ref):
    @pl.when(pl.program_id(1) == 0)
    def _():
        xn_ref[...] = _rmsnorm_rows(x_ref[...], g_ref[...]).astype(BF16)

    o_ref[...] = jnp.dot(xn_ref[...], w_ref[...], preferred_element_type=F32).astype(o_ref.dtype)


def _in_proj(x2d, g, w, *, tm, tn, n_row_tiles, row_tile_of, col_blk0, n_col_tiles, out_dtype, name):
    d = x2d.shape[1]
    return pl.pallas_call(
        _in_proj_body,
        grid=(n_row_tiles, n_col_tiles),
        in_specs=[
            pl.BlockSpec((tm, d), lambda i, j: (row_tile_of(i), 0)),
            pl.BlockSpec((1, d), lambda i, j: (0, 0)),
            pl.BlockSpec((d, tn), lambda i, j: (0, col_blk0 + j)),
        ],
        out_specs=pl.BlockSpec((tm, tn), lambda i, j: (i, j)),
        out_shape=jax.ShapeDtypeStruct((n_row_tiles * tm, n_col_tiles * tn), out_dtype),
        scratch_shapes=[pltpu.VMEM((tm, d), BF16)],
        compiler_params=_params(2, 48),
        name=name,
    )(x2d, g, w)


def _bias_diagonals(rel_bias, width, offsets, heads):
    x = jnp.arange(width)
    rows = [rel_bias[:, jnp.clip(off + CHUNK - 1 - x, -MAX_REL, MAX_REL) + MAX_REL] for off in offsets]
    g = jnp.stack(rows, axis=1).astype(F32)
    n_groups = rel_bias.shape[0] // heads
    g = g.reshape(n_groups, heads, len(offsets), width)
    return jnp.swapaxes(g, 1, 2).reshape(n_groups, len(offsets) * heads, width)


def _attn_body(*refs, bb, n_chunks, kw, heads, offsets, unit, cached):
    if cached:
        q_ref, kc_ref, vc_ref, k_ref, v_ref, g_ref, o_ref, tab_ref, s_ref, p_ref, l_ref = refs
        band = lambda c_ref, n_ref, bi, ks: jnp.concatenate([c_ref[bi], n_ref[bi]], axis=0)
    else:
        q_ref, k_ref, v_ref, g_ref, o_ref, tab_ref, s_ref, p_ref, l_ref = refs
        kc_ref = vc_ref = None
        band = lambda c_ref, n_ref, bi, ks: n_ref[bi, pl.ds(ks, kw), :]
    width = g_ref.shape[2]
    hw = heads * HEAD_DIM

    @pl.when(pl.program_id(1) == 0)
    def _():
        r = lax.broadcasted_iota(jnp.int32, (CHUNK, kw), 0)
        j = lax.broadcasted_iota(jnp.int32, (CHUNK, kw), 1)
        for c, off in enumerate(offsets):
            cq = (r + off) // CHUNK
            ck = j // CHUNK
            valid = (ck <= cq) & (ck >= cq - PAST_CHUNKS)
            for h in range(heads):
                g = jnp.broadcast_to(g_ref[0, c * heads + h:c * heads + h + 1, :], (CHUNK, width))
                t = pltpu.roll(g, width - (CHUNK - 1), 1, stride=1, stride_axis=0)
                tab_ref[c, h * CHUNK:(h + 1) * CHUNK, :] = jnp.where(valid, t[:, :kw], NEG_INF)

    lane = lax.broadcasted_iota(jnp.int32, (1, hw), 1)
    head_lanes = [(lane >= h * HEAD_DIM) & (lane < (h + 1) * HEAD_DIM) for h in range(heads)]
    rows = heads * CHUNK
    n_units = bb * n_chunks // unit

    def locate(u, j):
        idx = u * unit + j
        bi, c = (0, idx) if bb == 1 else (idx // n_chunks, idx % n_chunks)
        q0 = pl.multiple_of(c * CHUNK, CHUNK)
        ks = pl.multiple_of(jnp.maximum(c * CHUNK - KV_REACH, 0), CHUNK)
        return bi, c, q0, ks

    def scores(u, slot):
        for j in range(unit):
            bi, c, q0, ks = locate(u, j)
            q = q_ref[bi, pl.ds(q0, CHUNK), :]
            qs = jnp.concatenate([jnp.where(m, q, jnp.zeros_like(q)) for m in head_lanes], axis=0)
            qs = qs * jnp.asarray(HEAD_DIM ** -0.5, q.dtype)
            kb = band(kc_ref, k_ref, bi, ks)
            s = lax.dot_general(qs, kb, (((1,), (1,)), ((), ())), preferred_element_type=F32)
            s_ref[slot, j * rows:(j + 1) * rows, :] = s + tab_ref[jnp.minimum(c, len(offsets) - 1)]

    def softmax(slot):
        s = s_ref[slot]
        p = jnp.exp(s - jnp.max(s, axis=-1, keepdims=True))
        p_ref[slot] = p.astype(p_ref.dtype)
        l_ref[slot] = 1.0 / jnp.sum(p, axis=-1, keepdims=True)

    def output(u, slot):
        for j in range(unit):
            bi, c, q0, ks = locate(u, j)
            vb = band(vc_ref, v_ref, bi, ks)
            o_all = jnp.dot(p_ref[slot, j * rows:(j + 1) * rows, :], vb, preferred_element_type=F32)
            o_all = o_all * l_ref[slot, j * rows:(j + 1) * rows, :]
            o = o_all[0:CHUNK]
            for h in range(1, heads):
                o = jnp.where(head_lanes[h], o_all[h * CHUNK:(h + 1) * CHUNK], o)
            o_ref[bi, pl.ds(q0, CHUNK), :] = o.astype(o_ref.dtype)

    def step(i, slot, do_scores, do_softmax, do_output):
        if do_scores:
            scores(i, slot)
        if do_output:
            output(i - 2, slot)
        if do_softmax:
            softmax(1 - slot)

    for i in range(2):
        step(i, i % 2, i < n_units, 1 <= i <= n_units, False)
    n_pairs = max(n_units - 2, 0) // 2
    if n_pairs:
        def body(t, carry):
            i = 2 + 2 * t
            step(i, 0, True, True, True)
            step(i + 1, 1, True, True, True)
            return carry
        lax.fori_loop(0, n_pairs, body, 0)
    for i in range(2 + 2 * n_pairs, n_units + 2):
        step(i, i % 2, i < n_units, 1 <= i <= n_units, i >= 2)


def _attention(q_arr, k_arr, v_arr, rel_bias, *, q_col0, k_col0, v_col0, bb, heads, offsets, name,
               k_cache=None, v_cache=None):
    b, sq, _ = q_arr.shape
    sk = k_arr.shape[1]
    kw = KV_REACH + CHUNK
    cached = k_cache is not None
    if cached:
        assert sq == CHUNK and sk == CHUNK and k_cache.shape[1] == KV_REACH
    else:
        assert sk >= kw and sq % CHUNK == 0
    hw = heads * HEAD_DIM
    n_groups = N_HEADS // heads
    n_case = len(offsets)
    width = -(-(kw + CHUNK - 1) // LANES) * LANES
    diag = _bias_diagonals(rel_bias, width, offsets, heads)
    qc, kc, vc = q_col0 // hw, k_col0 // hw, v_col0 // hw
    unit = 2
    assert (bb * sq // CHUNK) % unit == 0
    rows = unit * heads * CHUNK
    cache_specs = [pl.BlockSpec((bb, KV_REACH, hw), lambda g, i: (i, 0, g))] * 2 if cached else []
    cache_args = [k_cache, v_cache] if cached else []
    return pl.pallas_call(
        functools.partial(_attn_body, bb=bb, n_chunks=sq // CHUNK, kw=kw, heads=heads,
                          offsets=tuple(offsets), unit=unit, cached=cached),
        grid=(n_groups, b // bb),
        in_specs=[pl.BlockSpec((bb, sq, hw), lambda g, i: (i, 0, qc + g))] + cache_specs + [
            pl.BlockSpec((bb, sk, hw), lambda g, i: (i, 0, kc + g)),
            pl.BlockSpec((bb, sk, hw), lambda g, i: (i, 0, vc + g)),
            pl.BlockSpec((1, n_case * heads, width), lambda g, i: (g, 0, 0)),
        ],
        out_specs=pl.BlockSpec((bb, sq, hw), lambda g, i: (i, 0, g)),
        out_shape=jax.ShapeDtypeStruct((b, sq, D_ATTN), BF16),
        scratch_shapes=[
            pltpu.VMEM((n_case, heads * CHUNK, kw), F32),
            pltpu.VMEM((2, rows, kw), F32),
            pltpu.VMEM((2, rows, kw), BF16),
            pltpu.VMEM((2, rows, 1), F32),
        ],
        compiler_params=_params(2, 40),
        name=name,
    )(q_arr, *cache_args, k_arr, v_arr, diag)


def _log_sigmoid(x):
    return jnp.minimum(x, 0.0) - jnp.log1p(jnp.exp(-jnp.abs(x)))


def _sigmoid(x):
    return 0.5 * jnp.tanh(0.5 * x) + 0.5


def _rglru_body(xr_ref, gr_ref, c0_ref, h0_ref, cw_ref, cb_ref, wg_ref, brg_ref, big_ref, lam_ref,
                ob_ref, hl_ref, xbuf, hcar, abuf, ubuf, *, tt, cw):
    pad = SUBLANES
    t = pl.program_id(2)

    @pl.when(t == 0)
    def _():
        hcar[...] = h0_ref[0]
        xbuf[pad - 3:pad, :] = c0_ref[0]

    x = xr_ref[0].astype(F32)
    xbuf[pad:pad + tt, :] = x
    w = cw_ref[...]
    xc = cb_ref[...] + w[0:1] * xbuf[pad - 3:pad - 3 + tt, :]
    xc = xc + w[1:2] * xbuf[pad - 2:pad - 2 + tt, :]
    xc = xc + w[2:3] * xbuf[pad - 1:pad - 1 + tt, :]
    xc = xc + w[3:4] * x
    xbuf[pad - 3:pad, :] = xbuf[pad + tt - 3:pad + tt, :]

    xcb = xc.astype(BF16)
    r_parts, i_parts = [], []
    for n in range(cw // RNN_BLOCK):
        gts = jnp.dot(xcb[:, n * RNN_BLOCK:(n + 1) * RNN_BLOCK], wg_ref[n], preferred_element_type=F32)
        r_parts.append(gts[:, :RNN_BLOCK])
        i_parts.append(gts[:, RNN_BLOCK:])
    r = _sigmoid(jnp.concatenate(r_parts, axis=1) + brg_ref[...])
    i = _sigmoid(jnp.concatenate(i_parts, axis=1) + big_ref[...])
    log_a = r * (LRU_C * _log_sigmoid(lam_ref[...]))
    abuf[...] = jnp.exp(log_a)
    w = -jnp.tanh(log_a)
    one_minus_a2 = 2.0 * w / (1.0 + w)
    root = jnp.where(one_minus_a2 > 0.0, one_minus_a2 * lax.rsqrt(one_minus_a2), 0.0)
    ubuf[...] = root * (i * xc)

    row = lax.broadcasted_iota(jnp.int32, (SUBLANES, cw), 0)

    def group(g, h):
        r0 = pl.multiple_of(g * SUBLANES, SUBLANES)
        a = abuf[pl.ds(r0, SUBLANES), :]
        u = ubuf[pl.ds(r0, SUBLANES), :]
        for d in (1, 2, 4):
            keep = row >= d
            a_prev = jnp.where(keep, pltpu.roll(a, d, 0), 1.0)
            u_prev = jnp.where(keep, pltpu.roll(u, d, 0), 0.0)
            u = a * u_prev + u
            a = a * a_prev
        hg = u + a * h
        ubuf[pl.ds(r0, SUBLANES), :] = hg
        return jnp.broadcast_to(hg[SUBLANES - 1:SUBLANES, :], (SUBLANES, cw))

    h_in = jnp.broadcast_to(hcar[...], (SUBLANES, cw))
    h_out = lax.fori_loop(0, tt // SUBLANES, group, h_in)
    hcar[...] = h_out[0:1, :]
    hl_ref[0] = h_out[0:1, :]
    gate = jax.nn.gelu(gr_ref[0].astype(F32), approximate=True)
    ob_ref[0] = (ubuf[...] * gate).astype(ob_ref.dtype)


def _rglru(z3, conv0, h0, conv_w, conv_b, w_gates, b_rg, b_ig, lam, *, xr_col0, gr_col0, tt, cw, name):
    b, t, _ = z3.shape
    d_rnn = conv_w.shape[1]
    n_cg = d_rnn // cw
    blocks_per_cg = cw // RNN_BLOCK
    xc0, gc0 = xr_col0 // cw, gr_col0 // cw
    vec = pl.BlockSpec((1, cw), lambda c, i, s: (0, c))
    return pl.pallas_call(
        functools.partial(_rglru_body, tt=tt, cw=cw),
        grid=(n_cg, b, t // tt),
        in_specs=[
            pl.BlockSpec((1, tt, cw), lambda c, i, s: (i, s, xc0 + c)),
            pl.BlockSpec((1, tt, cw), lambda c, i, s: (i, s, gc0 + c)),
            pl.BlockSpec((1, CONV_WIDTH - 1, cw), lambda c, i, s: (i, 0, c)),
            pl.BlockSpec((1, 1, cw), lambda c, i, s: (i, 0, c)),
            pl.BlockSpec((CONV_WIDTH, cw), lambda c, i, s: (0, c)),
            vec,
            pl.BlockSpec((blocks_per_cg, RNN_BLOCK, 2 * RNN_BLOCK), lambda c, i, s: (c, 0, 0)),
            vec, vec, vec,
        ],
        out_specs=[
            pl.BlockSpec((1, tt, cw), lambda c, i, s: (i, s, c)),
            pl.BlockSpec((1, 1, cw), lambda c, i, s: (i, 0, c)),
        ],
        out_shape=[
            jax.ShapeDtypeStruct((b, t, d_rnn), BF16),
            jax.ShapeDtypeStruct((b, 1, d_rnn), F32),
        ],
        scratch_shapes=[
            pltpu.VMEM((SUBLANES + tt, cw), F32),
            pltpu.VMEM((1, cw), F32),
            pltpu.VMEM((tt, cw), F32),
            pltpu.VMEM((tt, cw), F32),
        ],
        compiler_params=_params(3, 40),
        name=name,
    )(z3, z3, conv0, h0, conv_w, conv_b, w_gates, b_rg, b_ig, lam)


def _merge_body(oa_ref, ob_ref, ga_ref, gb_ref, wa_ref, wb_ref, o_ref):
    ya = jnp.dot(oa_ref[...], wa_ref[...], preferred_element_type=F32)
    yb = jnp.dot(ob_ref[...], wb_ref[...], preferred_element_type=F32)
    ga = jax.nn.sigmoid(ga_ref[...].astype(F32))
    gb = jax.nn.sigmoid(gb_ref[...].astype(F32))
    o_ref[...] = (ga * ya + gb * yb).astype(o_ref.dtype)


def _gate_merge(o_a, o_b, z2, w_a, w_b, *, ga_col0, gb_col0, tm, tn, name):
    m = o_a.shape[0]
    d = w_a.shape[1]
    ga0, gb0 = ga_col0 // tn, gb_col0 // tn
    return pl.pallas_call(
        _merge_body,
        grid=(m // tm, d // tn),
        in_specs=[
            pl.BlockSpec((tm, o_a.shape[1]), lambda i, j: (i, 0)),
            pl.BlockSpec((tm, o_b.shape[1]), lambda i, j: (i, 0)),
            pl.BlockSpec((tm, tn), lambda i, j: (i, ga0 + j)),
            pl.BlockSpec((tm, tn), lambda i, j: (i, gb0 + j)),
            pl.BlockSpec((w_a.shape[0], tn), lambda i, j: (0, j)),
            pl.BlockSpec((w_b.shape[0], tn), lambda i, j: (0, j)),
        ],
        out_specs=pl.BlockSpec((tm, tn), lambda i, j: (i, j)),
        out_shape=jax.ShapeDtypeStruct((m, d), BF16),
        compiler_params=_params(2, 48),
        name=name,
    )(o_a, o_b, z2, z2, w_a, w_b)


def _out_norm_body(m_ref, w_ref, x_ref, g_ref, o_ref):
    y = jnp.dot(m_ref[...], w_ref[...], preferred_element_type=F32)
    o_ref[...] = x_ref[...] + _rmsnorm_rows(y, g_ref[...])


def _out_norm(merged, w_out, x2d, g, *, tm, name):
    m, d = x2d.shape
    return pl.pallas_call(
        _out_norm_body,
        grid=(m // tm,),
        in_specs=[
            pl.BlockSpec((tm, d), lambda i: (i, 0)),
            pl.BlockSpec((d, d), lambda i: (0, 0)),
            pl.BlockSpec((tm, d), lambda i: (i, 0)),
            pl.BlockSpec((1, d), lambda i: (0, 0)),
        ],
        out_specs=pl.BlockSpec((tm, d), lambda i: (i, 0)),
        out_shape=jax.ShapeDtypeStruct((m, d), F32),
        compiler_params=_params(1, 48),
        name=name,
    )(merged, w_out, x2d, g)


def _ffn_body(x_ref, g1_ref, w1_ref, w2_ref, g2_ref, o_ref, xn_ref):
    f = pl.program_id(1)

    @pl.when(f == 0)
    def _():
        xn_ref[...] = _rmsnorm_rows(x_ref[...], g1_ref[...]).astype(BF16)
        o_ref[...] = jnp.zeros_like(o_ref)

    hid = jnp.dot(xn_ref[...], w1_ref[...], preferred_element_type=F32)
    hid = jnp.square(jnp.maximum(hid, 0.0)).astype(BF16)
    o_ref[...] += jnp.dot(hid, w2_ref[...], preferred_element_type=F32)

    @pl.when(f == pl.num_programs(1) - 1)
    def _():
        o_ref[...] = x_ref[...] + _rmsnorm_rows(o_ref[...], g2_ref[...])


def _ffn(x2d, g1, w1, w2, g2, *, tm, tf, name):
    m, d = x2d.shape
    d_ff = w1.shape[1]
    return pl.pallas_call(
        _ffn_body,
        grid=(m // tm, d_ff // tf),
        in_specs=[
            pl.BlockSpec((tm, d), lambda i, f: (i, 0)),
            pl.BlockSpec((1, d), lambda i, f: (0, 0)),
            pl.BlockSpec((d, tf), lambda i, f: (0, f)),
            pl.BlockSpec((tf, d), lambda i, f: (f, 0)),
            pl.BlockSpec((1, d), lambda i, f: (0, 0)),
        ],
        out_specs=pl.BlockSpec((tm, d), lambda i, f: (i, 0)),
        out_shape=jax.ShapeDtypeStruct((m, d), F32),
        scratch_shapes=[pltpu.VMEM((tm, d), BF16)],
        compiler_params=_params(2, 48),
        name=name,
    )(x2d, g1, w1, w2, g2)


def _pick_tile(n, target):
    t = min(n, target)
    while n % t:
        t //= 2
    return t


def _layer(x, cache_k, cache_v, conv_state, h_state, wts, tag):
    b, t, d = x.shape
    m = b * t
    x2d = x.reshape(m, d)
    d_rnn = wts["conv_w"].shape[1]
    n_in = wts["w_in"].shape[1]
    col_k, col_v, col_xr = D_ATTN, 2 * D_ATTN, 3 * D_ATTN
    col_gr = col_xr + d_rnn
    col_ga = col_gr + d_rnn
    col_gb = col_ga + d

    tm = _pick_tile(m, 1024)
    tn = 1024
    z2 = _in_proj(x2d, wts["pre_mix_g"], wts["w_in"], tm=tm, tn=tn, n_row_tiles=m // tm,
                  row_tile_of=lambda i: i, col_blk0=0, n_col_tiles=n_in // tn, out_dtype=BF16,
                  name=f"in_proj_{tag}")
    z3 = z2.reshape(b, t, n_in)

    keep = min(KV_REACH, t)
    if keep == t:
        ts, n_state_tiles, state_tile_of = _pick_tile(m, 512), m // _pick_tile(m, 512), lambda i: i
    else:
        tiles_per_seq = t // keep
        ts, n_state_tiles = keep, b
        state_tile_of = lambda i: i * tiles_per_seq + tiles_per_seq - 1
    state = _in_proj(x2d, wts["pre_mix_g"], wts["w_in"], tm=ts, tn=tn, n_row_tiles=n_state_tiles,
                     row_tile_of=state_tile_of,
                     col_blk0=col_k // tn, n_col_tiles=(col_gr - col_k) // tn, out_dtype=F32,
                     name=f"state_proj_{tag}")
    state = state.reshape(b, keep, col_gr - col_k)
    new_k = state[:, :, :D_ATTN].reshape(b, keep, N_HEADS, HEAD_DIM)
    new_v = state[:, :, D_ATTN:2 * D_ATTN].reshape(b, keep, N_HEADS, HEAD_DIM)
    conv_tail = state[:, keep - (CONV_WIDTH - 1):, 2 * D_ATTN:]

    heads = 2
    if cache_k is None:
        offsets = [i * CHUNK for i in range(PAST_CHUNKS + 1)]
        o_a = _attention(z3, z3, z3, wts["rel_bias"], q_col0=0, k_col0=col_k, v_col0=col_v, bb=1,
                         heads=heads, offsets=offsets, name=f"attn_{tag}")
        conv0 = jnp.zeros((b, CONV_WIDTH - 1, d_rnn), F32)
        h0 = jnp.zeros((b, 1, d_rnn), F32)
    else:
        n_cached = cache_k.shape[1]
        assert t == CHUNK and n_cached == KV_REACH
        o_a = _attention(z3, z3, z3, wts["rel_bias"], q_col0=0, k_col0=col_k, v_col0=col_v,
                         bb=_pick_tile(b, 8), heads=heads, offsets=[n_cached], name=f"attn_{tag}",
                         k_cache=cache_k.reshape(b, n_cached, D_ATTN).astype(BF16),
                         v_cache=cache_v.reshape(b, n_cached, D_ATTN).astype(BF16))
        conv0 = conv_state.astype(F32)
        h0 = h_state.astype(F32).reshape(b, 1, d_rnn)

    o_b, h_last = _rglru(z3, conv0, h0, wts["conv_w"], wts["conv_b"], wts["w_gates"], wts["b_rg"],
                         wts["b_ig"], wts["lru_lambda"], xr_col0=col_xr, gr_col0=col_gr,
                         tt=_pick_tile(t, 256), cw=1024, name=f"rglru_{tag}")

    merged = _gate_merge(o_a.reshape(m, D_ATTN), o_b.reshape(m, d_rnn), z2, wts["w_attn_up"],
                         wts["w_rnn_up"], ga_col0=col_ga, gb_col0=col_gb, tm=tm, tn=tn,
                         name=f"gate_merge_{tag}")
    tm2 = _pick_tile(m, 512)
    x1 = _out_norm(merged, wts["w_out"], x2d, wts["post_mix_g"], tm=tm2, name=f"out_norm_{tag}")
    y = _ffn(x1, wts["pre_ffn_g"], wts["w_ff1"], wts["w_ff2"], wts["post_ffn_g"], tm=tm2, tf=1024,
             name=f"ffn_{tag}")
    return y.reshape(b, t, d), new_k, new_v, conv_tail, h_last.reshape(b, d_rnn)


def kernel(x_prompt, x_sample, cache_k, cache_v, state_conv, state_h, pre_mix_g, w_in, rel_bias, conv_w, conv_b, w_rg, b_rg, w_ig, b_ig, lru_lambda, w_attn_up, w_rnn_up, w_out, post_mix_g, pre_ffn_g, w_ff1, w_ff2, post_ffn_g):
    depth = w_in.shape[0]
    y_p, y_s = x_prompt, x_sample
    outs_p, outs_s = [], []
    for l in range(depth):
        wts = {
            "pre_mix_g": pre_mix_g[l][None], "post_mix_g": post_mix_g[l][None],
            "pre_ffn_g": pre_ffn_g[l][None], "post_ffn_g": post_ffn_g[l][None],
            "w_in": w_in[l].astype(BF16), "rel_bias": rel_bias[l],
            "conv_w": conv_w[l], "conv_b": conv_b[l][None],
            "w_gates": jnp.concatenate([w_rg[l], w_ig[l]], axis=-1).astype(BF16),
            "b_rg": b_rg[l][None], "b_ig": b_ig[l][None], "lru_lambda": lru_lambda[l][None],
            "w_attn_up": w_attn_up[l].astype(BF16), "w_rnn_up": w_rnn_up[l].astype(BF16),
            "w_out": w_out[l].astype(BF16), "w_ff1": w_ff1[l].astype(BF16), "w_ff2": w_ff2[l].astype(BF16),
        }
        y_p, *st_p = _layer(y_p, None, None, None, None, wts, f"p{l}")
        y_s, *st_s = _layer(y_s, cache_k[l], cache_v[l], state_conv[l], state_h[l], wts, f"s{l}")
        outs_p.append(st_p)
        outs_s.append(st_s)
    stack = lambda outs, i: jnp.stack([o[i] for o in outs])
    return (y_p, y_s,
            stack(outs_p, 0), stack(outs_p, 1), stack(outs_p, 2), stack(outs_p, 3),
            stack(outs_s, 0), stack(outs_s, 1), stack(outs_s, 2), stack(outs_s, 3))
```

```python
import functools

import jax
import jax.numpy as jnp
from jax import lax
from jax.experimental import pallas as pl
from jax.experimental.pallas import tpu as pltpu

F32 = jnp.float32
BF16 = jnp.bfloat16

CHUNK = 64
PAST_CHUNKS = 8
KV_REACH = PAST_CHUNKS * CHUNK
N_HEADS = 16
HEAD_DIM = 64
D_ATTN = N_HEADS * HEAD_DIM
MAX_REL = 256
RNN_BLOCK = 128
CONV_WIDTH = 4
LRU_C = 8.0
EPS = 1e-6
NEG_INF = -1e30

LANES = 128
SUBLANES = 8
MIB = 1024 * 1024


def _params(n_axes, vmem_mib):
    return pltpu.CompilerParams(
        dimension_semantics=("arbitrary",) * n_axes,
        vmem_limit_bytes=vmem_mib * MIB,
    )


def _rmsnorm_rows(x, g):
    ms = jnp.mean(x * x, axis=-1, keepdims=True)
    return x * lax.rsqrt(ms + EPS) * g


def _in_proj_body(x_ref, g_ref, w_ref, o_ref, s_ref, xn_ref, *, period, j0, j1):
    i, j = pl.program_id(0), pl.program_id(1)

    @pl.when(j == 0)
    def _():
        xn_ref[...] = _rmsnorm_rows(x_ref[...], g_ref[...]).astype(BF16)

    keep_f32 = (i % period == period - 1) & (j >= j0) & (j < j1)

    @pl.when(keep_f32)
    def _():
        acc = jnp.dot(xn_ref[...], w_ref[...], preferred_element_type=F32)
        o_ref[...] = acc.astype(o_ref.dtype)
        s_ref[...] = acc[acc.shape[0] - s_ref.shape[0]:, :]

    @pl.when(jnp.logical_not(keep_f32))
    def _():
        o_ref[...] = jnp.dot(xn_ref[...], w_ref[...], preferred_element_type=F32).astype(o_ref.dtype)


def _in_proj(x2d, g, w, *, tm, tn, period, state_rows, state_col0, state_cols, name):
    m, d = x2d.shape
    n = w.shape[1]
    j0, j1 = state_col0 // tn, (state_col0 + state_cols) // tn
    n_state_tiles = m // tm // period

    def state_block(i, j):
        live = i % period == period - 1
        return i // period, jnp.where(live, jnp.clip(j - j0, 0, j1 - j0 - 1), 0)

    return pl.pallas_call(
        functools.partial(_in_proj_body, period=period, j0=j0, j1=j1),
        grid=(m // tm, n // tn),
        in_specs=[
            pl.BlockSpec((tm, d), lambda i, j: (i, 0)),
            pl.BlockSpec((1, d), lambda i, j: (0, 0)),
            pl.BlockSpec((d, tn), lambda i, j: (0, j)),
        ],
        out_specs=[
            pl.BlockSpec((tm, tn), lambda i, j: (i, j)),
            pl.BlockSpec((state_rows, tn), state_block),
        ],
        out_shape=[
            jax.ShapeDtypeStruct((m, n), BF16),
            jax.ShapeDtypeStruct((n_state_tiles * state_rows, state_cols), F32),
        ],
        scratch_shapes=[pltpu.VMEM((tm, d), BF16)],
        compiler_params=_params(2, 52),
        name=name,
    )(x2d, g, w)


def _bias_diagonals(rel_bias, width, offsets, heads):
    x = jnp.arange(width)
    rows = [rel_bias[:, jnp.clip(off + CHUNK - 1 - x, -MAX_REL, MAX_REL) + MAX_REL] for off in offsets]
    g = jnp.stack(rows, axis=1).astype(F32)
    n_groups = rel_bias.shape[0] // heads
    g = g.reshape(n_groups, heads, len(offsets), width)
    return jnp.swapaxes(g, 1, 2).reshape(n_groups, len(offsets) * heads, width)


def _attn_body(*refs, bb, n_chunks, kw, heads, offsets, unit, cached):
    if cached:
        q_ref, kc_ref, vc_ref, k_ref, v_ref, g_ref, o_ref, tab_ref, s_ref, p_ref, l_ref = refs
        band = lambda c_ref, n_ref, bi, ks: jnp.concatenate([c_ref[bi], n_ref[bi]], axis=0)
    else:
        q_ref, k_ref, v_ref, g_ref, o_ref, tab_ref, s_ref, p_ref, l_ref = refs
        kc_ref = vc_ref = None
        band = lambda c_ref, n_ref, bi, ks: n_ref[bi, pl.ds(ks, kw), :]
    width = g_ref.shape[2]
    hw = heads * HEAD_DIM

    @pl.when(pl.program_id(1) == 0)
    def _():
        r = lax.broadcasted_iota(jnp.int32, (CHUNK, kw), 0)
        j = lax.broadcasted_iota(jnp.int32, (CHUNK, kw), 1)
        for c, off in enumerate(offsets):
            cq = (r + off) // CHUNK
            ck = j // CHUNK
            valid = (ck <= cq) & (ck >= cq - PAST_CHUNKS)
            for h in range(heads):
                g = jnp.broadcast_to(g_ref[0, c * heads + h:c * heads + h + 1, :], (CHUNK, width))
                t = pltpu.roll(g, width - (CHUNK - 1), 1, stride=1, stride_axis=0)
                tab_ref[c, h * CHUNK:(h + 1) * CHUNK, :] = jnp.where(valid, t[:, :kw], NEG_INF)

    lane = lax.broadcasted_iota(jnp.int32, (1, hw), 1)
    head_lanes = [(lane >= h * HEAD_DIM) & (lane < (h + 1) * HEAD_DIM) for h in range(heads)]
    rows = heads * CHUNK
    n_units = bb * n_chunks // unit

    def locate(u, j):
        idx = u * unit + j
        bi, c = (0, idx) if bb == 1 else (idx // n_chunks, idx % n_chunks)
        q0 = pl.multiple_of(c * CHUNK, CHUNK)
        ks = pl.multiple_of(jnp.maximum(c * CHUNK - KV_REACH, 0), CHUNK)
        return bi, c, q0, ks

    def scores(u, slot):
        for j in range(unit):
            bi, c, q0, ks = locate(u, j)
            q = q_ref[bi, pl.ds(q0, CHUNK), :]
            qs = jnp.concatenate([jnp.where(m, q, jnp.zeros_like(q)) for m in head_lanes], axis=0)
            qs = qs * jnp.asarray(HEAD_DIM ** -0.5, q.dtype)
            kb = band(kc_ref, k_ref, bi, ks)
            s = lax.dot_general(qs, kb, (((1,), (1,)), ((), ())), preferred_element_type=F32)
            s_ref[slot, j * rows:(j + 1) * rows, :] = s + tab_ref[jnp.minimum(c, len(offsets) - 1)]

    def softmax(slot):
        s = s_ref[slot]
        p = jnp.exp(s - jnp.max(s, axis=-1, keepdims=True))
        p_ref[slot] = p.astype(p_ref.dtype)
        l_ref[slot] = 1.0 / jnp.sum(p, axis=-1, keepdims=True)

    def output(u, slot):
        for j in range(unit):
            bi, c, q0, ks = locate(u, j)
            vb = band(vc_ref, v_ref, bi, ks)
            o_all = jnp.dot(p_ref[slot, j * rows:(j + 1) * rows, :], vb, preferred_element_type=F32)
            o_all = o_all * l_ref[slot, j * rows:(j + 1) * rows, :]
            o = o_all[0:CHUNK]
            for h in range(1, heads):
                o = jnp.where(head_lanes[h], o_all[h * CHUNK:(h + 1) * CHUNK], o)
            o_ref[bi, pl.ds(q0, CHUNK), :] = o.astype(o_ref.dtype)

    def step(i, slot, do_scores, do_softmax, do_output):
        if do_scores:
            scores(i, slot)
        if do_output:
            output(i - 2, slot)
        if do_softmax:
            softmax(1 - slot)

    for i in range(2):
        step(i, i % 2, i < n_units, 1 <= i <= n_units, False)
    n_pairs = max(n_units - 2, 0) // 2
    if n_pairs:
        def body(t, carry):
            i = 2 + 2 * t
            step(i, 0, True, True, True)
            step(i + 1, 1, True, True, True)
            return carry
        lax.fori_loop(0, n_pairs, body, 0)
    for i in range(2 + 2 * n_pairs, n_units + 2):
        step(i, i % 2, i < n_units, 1 <= i <= n_units, i >= 2)


def _attention(q_arr, k_arr, v_arr, rel_bias, *, q_col0, k_col0, v_col0, bb, heads, offsets, name,
               k_cache=None, v_cache=None):
    b, sq, _ = q_arr.shape
    sk = k_arr.shape[1]
    kw = KV_REACH + CHUNK
    cached = k_cache is not None
    if cached:
        assert sq == CHUNK and sk == CHUNK and k_cache.shape[1] == KV_REACH
    else:
        assert sk >= kw and sq % CHUNK == 0
    hw = heads * HEAD_DIM
    n_groups = N_HEADS // heads
    n_case = len(offsets)
    width = -(-(kw + CHUNK - 1) // LANES) * LANES
    diag = _bias_diagonals(rel_bias, width, offsets, heads)
    qc, kc, vc = q_col0 // hw, k_col0 // hw, v_col0 // hw
    unit = 2
    assert (bb * sq // CHUNK) % unit == 0
    rows = unit * heads * CHUNK
    cache_specs = [pl.BlockSpec((bb, KV_REACH, hw), lambda g, i: (i, 0, g))] * 2 if cached else []
    cache_args = [k_cache, v_cache] if cached else []
    return pl.pallas_call(
        functools.partial(_attn_body, bb=bb, n_chunks=sq // CHUNK, kw=kw, heads=heads,
                          offsets=tuple(offsets), unit=unit, cached=cached),
        grid=(n_groups, b // bb),
        in_specs=[pl.BlockSpec((bb, sq, hw), lambda g, i: (i, 0, qc + g))] + cache_specs + [
            pl.BlockSpec((bb, sk, hw), lambda g, i: (i, 0, kc + g)),
            pl.BlockSpec((bb, sk, hw), lambda g, i: (i, 0, vc + g)),
            pl.BlockSpec((1, n_case * heads, width), lambda g, i: (g, 0, 0)),
        ],
        out_specs=pl.BlockSpec((bb, sq, hw), lambda g, i: (i, 0, g)),
        out_shape=jax.ShapeDtypeStruct((b, sq, D_ATTN), BF16),
        scratch_shapes=[
            pltpu.VMEM((n_case, heads * CHUNK, kw), F32),
            pltpu.VMEM((2, rows, kw), F32),
            pltpu.VMEM((2, rows, kw), BF16),
            pltpu.VMEM((2, rows, 1), F32),
        ],
        compiler_params=_params(2, 40),
        name=name,
    )(q_arr, *cache_args, k_arr, v_arr, diag)


def _log_sigmoid(x):
    return jnp.minimum(x, 0.0) - jnp.log1p(jnp.exp(-jnp.abs(x)))


def _shift_matrix(tt):
    t = jnp.arange(tt)
    blocks = [(t[:, None] - d) == t[None, :] for d in range(CONV_WIDTH - 1, 0, -1)]
    return jnp.concatenate(blocks, axis=0).astype(BF16)


def _rglru_piece(x_bf, gr_bf, prev8, h_in, shift, conv_w, conv_b, wg_ref, b_rg, b_ig, lam):
    tt, cw = x_bf.shape
    taps = CONV_WIDTH - 1
    sh = jnp.dot(shift, x_bf, preferred_element_type=F32)
    x = x_bf.astype(F32)
    xc = conv_b + conv_w[0:1] * sh[0:tt]
    for k in range(1, taps):
        xc = xc + conv_w[k:k + 1] * sh[k * tt:(k + 1) * tt]
    xc = xc + conv_w[taps:taps + 1] * x
    row = lax.broadcasted_iota(jnp.int32, (SUBLANES, cw), 0)
    head = xc[0:SUBLANES]
    for k in range(taps):
        d = taps - k
        head = head + conv_w[k:k + 1] * jnp.where(row < d, pltpu.roll(prev8, d, 0), 0.0)
    xc = jnp.concatenate([head, xc[SUBLANES:]], axis=0)

    xcb = xc.astype(BF16)
    r_parts, i_parts = [], []
    for n in range(cw // RNN_BLOCK):
        gts = jnp.dot(xcb[:, n * RNN_BLOCK:(n + 1) * RNN_BLOCK], wg_ref[n], preferred_element_type=F32)
        r_parts.append(gts[:, :RNN_BLOCK])
        i_parts.append(gts[:, RNN_BLOCK:])
    r = jax.nn.sigmoid(jnp.concatenate(r_parts, axis=1) + b_rg)
    i = jax.nn.sigmoid(jnp.concatenate(i_parts, axis=1) + b_ig)
    log_a = r * (LRU_C * _log_sigmoid(lam))
    a = jnp.exp(log_a)
    w = -jnp.tanh(log_a)
    one_minus_a2 = 2.0 * w / (1.0 + w)
    root = jnp.where(one_minus_a2 > 0.0, one_minus_a2 * lax.rsqrt(one_minus_a2), 0.0)
    u = root * (i * xc)

    h = h_in
    hs = []
    for g in range(tt // SUBLANES):
        ag = a[g * SUBLANES:(g + 1) * SUBLANES]
        ug = u[g * SUBLANES:(g + 1) * SUBLANES]
        for d in (1, 2, 4):
            keep = row >= d
            a_prev = jnp.where(keep, pltpu.roll(ag, d, 0), 1.0)
            u_prev = jnp.where(keep, pltpu.roll(ug, d, 0), 0.0)
            ug = ag * u_prev + ug
            ag = ag * a_prev
        hg = ug + ag * h
        hs.append(hg)
        h = jnp.broadcast_to(hg[SUBLANES - 1:SUBLANES, :], (SUBLANES, cw))
    gate = jax.nn.gelu(gr_bf.astype(F32), approximate=True)
    out = (jnp.concatenate(hs, axis=0) * gate).astype(BF16)
    return out, x[tt - SUBLANES:tt], h


def _rglru_body(xr_ref, gr_ref, c0_ref, h0_ref, sh_ref, cw_ref, cb_ref, wg_ref, brg_ref, big_ref, lam_ref,
                ob_ref, hl_ref, prev_ref, hcar_ref):
    @pl.when(pl.program_id(2) == 0)
    def _():
        hcar_ref[...] = jnp.broadcast_to(h0_ref[0], hcar_ref.shape)
        prev_ref[...] = c0_ref[0]

    out, prev8, h = _rglru_piece(xr_ref[0], gr_ref[0], prev_ref[...], hcar_ref[...], sh_ref[...],
                                 cw_ref[...], cb_ref[...], wg_ref, brg_ref[...], big_ref[...], lam_ref[...])
    ob_ref[0] = out
    prev_ref[...] = prev8
    hcar_ref[...] = h
    hl_ref[0] = h[0:1, :]


def _rglru(z3, conv0, h0, conv_w, conv_b, w_gates, b_rg, b_ig, lam, *, xr_col0, gr_col0, tt, cw, name):
    b, t, _ = z3.shape
    d_rnn = conv_w.shape[1]
    n_cg = d_rnn // cw
    blocks_per_cg = cw // RNN_BLOCK
    xc0, gc0 = xr_col0 // cw, gr_col0 // cw
    vec = pl.BlockSpec((1, cw), lambda c, i, s: (0, c))
    return pl.pallas_call(
        _rglru_body,
        grid=(n_cg, b, t // tt),
        in_specs=[
            pl.BlockSpec((1, tt, cw), lambda c, i, s: (i, s, xc0 + c)),
            pl.BlockSpec((1, tt, cw), lambda c, i, s: (i, s, gc0 + c)),
            pl.BlockSpec((1, SUBLANES, cw), lambda c, i, s: (i, 0, c)),
            pl.BlockSpec((1, 1, cw), lambda c, i, s: (i, 0, c)),
            pl.BlockSpec(((CONV_WIDTH - 1) * tt, tt), lambda c, i, s: (0, 0)),
            pl.BlockSpec((CONV_WIDTH, cw), lambda c, i, s: (0, c)),
            vec,
            pl.BlockSpec((blocks_per_cg, RNN_BLOCK, 2 * RNN_BLOCK), lambda c, i, s: (c, 0, 0)),
            vec, vec, vec,
        ],
        out_specs=[
            pl.BlockSpec((1, tt, cw), lambda c, i, s: (i, s, c)),
            pl.BlockSpec((1, 1, cw), lambda c, i, s: (i, 0, c)),
        ],
        out_shape=[
            jax.ShapeDtypeStruct((b, t, d_rnn), BF16),
            jax.ShapeDtypeStruct((b, 1, d_rnn), F32),
        ],
        scratch_shapes=[
            pltpu.VMEM((SUBLANES, cw), F32),
            pltpu.VMEM((SUBLANES, cw), F32),
        ],
        compiler_params=_params(3, 40),
        name=name,
    )(z3, z3, conv0, h0, _shift_matrix(tt), conv_w, conv_b, w_gates, b_rg, b_ig, lam)


def _merge_body(oa_ref, ob_ref, ga_ref, gb_ref, wa_ref, wb_ref, o_ref):
    ya = jnp.dot(oa_ref[...], wa_ref[...], preferred_element_type=F32)
    yb = jnp.dot(ob_ref[...], wb_ref[...], preferred_element_type=F32)
    ga = jax.nn.sigmoid(ga_ref[...].astype(F32))
    gb = jax.nn.sigmoid(gb_ref[...].astype(F32))
    o_ref[...] = (ga * ya + gb * yb).astype(o_ref.dtype)


def _gate_merge(o_a, o_b, z2, w_a, w_b, *, ga_col0, gb_col0, tm, tn, name):
    m = o_a.shape[0]
    d = w_a.shape[1]
    ga0, gb0 = ga_col0 // tn, gb_col0 // tn
    return pl.pallas_call(
        _merge_body,
        grid=(m // tm, d // tn),
        in_specs=[
            pl.BlockSpec((tm, o_a.shape[1]), lambda i, j: (i, 0)),
            pl.BlockSpec((tm, o_b.shape[1]), lambda i, j: (i, 0)),
            pl.BlockSpec((tm, tn), lambda i, j: (i, ga0 + j)),
            pl.BlockSpec((tm, tn), lambda i, j: (i, gb0 + j)),
            pl.BlockSpec((w_a.shape[0], tn), lambda i, j: (0, j)),
            pl.BlockSpec((w_b.shape[0], tn), lambda i, j: (0, j)),
        ],
        out_specs=pl.BlockSpec((tm, tn), lambda i, j: (i, j)),
        out_shape=jax.ShapeDtypeStruct((m, d), BF16),
        compiler_params=_params(2, 48),
        name=name,
    )(o_a, o_b, z2, z2, w_a, w_b)


def _out_norm_body(m_ref, w_ref, x_ref, g_ref, o_ref):
    y = jnp.dot(m_ref[...], w_ref[...], preferred_element_type=F32)
    o_ref[...] = x_ref[...] + _rmsnorm_rows(y, g_ref[...])


def _out_norm(merged, w_out, x2d, g, *, tm, name):
    m, d = x2d.shape
    return pl.pallas_call(
        _out_norm_body,
        grid=(m // tm,),
        in_specs=[
            pl.BlockSpec((tm, d), lambda i: (i, 0)),
            pl.BlockSpec((d, d), lambda i: (0, 0)),
            pl.BlockSpec((tm, d), lambda i: (i, 0)),
            pl.BlockSpec((1, d), lambda i: (0, 0)),
        ],
        out_specs=pl.BlockSpec((tm, d), lambda i: (i, 0)),
        out_shape=jax.ShapeDtypeStruct((m, d), F32),
        compiler_params=_params(1, 48),
        name=name,
    )(merged, w_out, x2d, g)


def _ffn_body(x_ref, g1_ref, w1_ref, w2_ref, g2_ref, o_ref, xn_ref):
    f = pl.program_id(1)

    @pl.when(f == 0)
    def _():
        xn_ref[...] = _rmsnorm_rows(x_ref[...], g1_ref[...]).astype(BF16)
        o_ref[...] = jnp.zeros_like(o_ref)

    hid = jnp.dot(xn_ref[...], w1_ref[...], preferred_element_type=F32)
    hid = jnp.square(jnp.maximum(hid, 0.0)).astype(BF16)
    o_ref[...] += jnp.dot(hid, w2_ref[...], preferred_element_type=F32)

    @pl.when(f == pl.num_programs(1) - 1)
    def _():
        o_ref[...] = x_ref[...] + _rmsnorm_rows(o_ref[...], g2_ref[...])


def _ffn(x2d, g1, w1, w2, g2, *, tm, tf, name):
    m, d = x2d.shape
    d_ff = w1.shape[1]
    return pl.pallas_call(
        _ffn_body,
        grid=(m // tm, d_ff // tf),
        in_specs=[
            pl.BlockSpec((tm, d), lambda i, f: (i, 0)),
            pl.BlockSpec((1, d), lambda i, f: (0, 0)),
            pl.BlockSpec((d, tf), lambda i, f: (0, f)),
            pl.BlockSpec((tf, d), lambda i, f: (f, 0)),
            pl.BlockSpec((1, d), lambda i, f: (0, 0)),
        ],
        out_specs=pl.BlockSpec((tm, d), lambda i, f: (i, 0)),
        out_shape=jax.ShapeDtypeStruct((m, d), F32),
        scratch_shapes=[pltpu.VMEM((tm, d), BF16)],
        compiler_params=_params(2, 48),
        name=name,
    )(x2d, g1, w1, w2, g2)


def _pick_tile(n, target):
    t = min(n, target)
    while n % t:
        t //= 2
    return t


def _layer(x, cache_k, cache_v, conv_state, h_state, wts, tag):
    b, t, d = x.shape
    m = b * t
    x2d = x.reshape(m, d)
    d_rnn = wts["conv_w"].shape[1]
    n_in = wts["w_in"].shape[1]
    col_k, col_v, col_xr = D_ATTN, 2 * D_ATTN, 3 * D_ATTN
    col_gr = col_xr + d_rnn
    col_ga = col_gr + d_rnn
    col_gb = col_ga + d

    tm = _pick_tile(m, 1024)
    tn = 1024
    keep = min(KV_REACH, t)
    if keep == t:
        period, state_rows = 1, tm
    else:
        assert t % tm == 0 and keep <= tm
        period, state_rows = t // tm, keep
    z2, state = _in_proj(x2d, wts["pre_mix_g"], wts["w_in"], tm=tm, tn=tn, period=period,
                         state_rows=state_rows, state_col0=col_k, state_cols=col_gr - col_k,
                         name=f"in_proj_{tag}")
    z3 = z2.reshape(b, t, n_in)
    state = state.reshape(b, keep, col_gr - col_k)
    new_k = state[:, :, :D_ATTN].reshape(b, keep, N_HEADS, HEAD_DIM)
    new_v = state[:, :, D_ATTN:2 * D_ATTN].reshape(b, keep, N_HEADS, HEAD_DIM)
    conv_tail = state[:, keep - (CONV_WIDTH - 1):, 2 * D_ATTN:]

    heads = 2
    if cache_k is None:
        offsets = [i * CHUNK for i in range(PAST_CHUNKS + 1)]
        o_a = _attention(z3, z3, z3, wts["rel_bias"], q_col0=0, k_col0=col_k, v_col0=col_v, bb=1,
                         heads=heads, offsets=offsets, name=f"attn_{tag}")
        conv0 = jnp.zeros((b, SUBLANES, d_rnn), F32)
        h0 = jnp.zeros((b, 1, d_rnn), F32)
    else:
        n_cached = cache_k.shape[1]
        assert t == CHUNK and n_cached == KV_REACH
        o_a = _attention(z3, z3, z3, wts["rel_bias"], q_col0=0, k_col0=col_k, v_col0=col_v,
                         bb=_pick_tile(b, 8), heads=heads, offsets=[n_cached], name=f"attn_{tag}",
                         k_cache=cache_k.astype(BF16).reshape(b, n_cached, D_ATTN),
                         v_cache=cache_v.astype(BF16).reshape(b, n_cached, D_ATTN))
        conv0 = jnp.pad(conv_state.astype(F32), ((0, 0), (SUBLANES - (CONV_WIDTH - 1), 0), (0, 0)))
        h0 = h_state.astype(F32).reshape(b, 1, d_rnn)

    o_b, h_last = _rglru(z3, conv0, h0, wts["conv_w"], wts["conv_b"], wts["w_gates"], wts["b_rg"],
                         wts["b_ig"], wts["lru_lambda"], xr_col0=col_xr, gr_col0=col_gr,
                         tt=_pick_tile(t, 256), cw=1024, name=f"rglru_{tag}")

    merged = _gate_merge(o_a.reshape(m, D_ATTN), o_b.reshape(m, d_rnn), z2, wts["w_attn_up"],
                         wts["w_rnn_up"], ga_col0=col_ga, gb_col0=col_gb, tm=tm, tn=tn,
                         name=f"gate_merge_{tag}")
    tm2 = _pick_tile(m, 512)
    x1 = _out_norm(merged, wts["w_out"], x2d, wts["post_mix_g"], tm=tm2, name=f"out_norm_{tag}")
    y = _ffn(x1, wts["pre_ffn_g"], wts["w_ff1"], wts["w_ff2"], wts["post_ffn_g"], tm=tm2, tf=1024,
             name=f"ffn_{tag}")
    return y.reshape(b, t, d), new_k, new_v, conv_tail, h_last.reshape(b, d_rnn)


def kernel(x_prompt, x_sample, cache_k, cache_v, state_conv, state_h, pre_mix_g, w_in, rel_bias, conv_w, conv_b, w_rg, b_rg, w_ig, b_ig, lru_lambda, w_attn_up, w_rnn_up, w_out, post_mix_g, pre_ffn_g, w_ff1, w_ff2, post_ffn_g):
    depth = w_in.shape[0]
    y_p, y_s = x_prompt, x_sample
    outs_p, outs_s = [], []
    for l in range(depth):
        wts = {
            "pre_mix_g": pre_mix_g[l][None], "post_mix_g": post_mix_g[l][None],
            "pre_ffn_g": pre_ffn_g[l][None], "post_ffn_g": post_ffn_g[l][None],
            "w_in": w_in[l].astype(BF16), "rel_bias": rel_bias[l],
            "conv_w": conv_w[l], "conv_b": conv_b[l][None],
            "w_gates": jnp.concatenate([w_rg[l], w_ig[l]], axis=-1).astype(BF16),
            "b_rg": b_rg[l][None], "b_ig": b_ig[l][None], "lru_lambda": lru_lambda[l][None],
            "w_attn_up": w_attn_up[l].astype(BF16), "w_rnn_up": w_rnn_up[l].astype(BF16),
            "w_out": w_out[l].astype(BF16), "w_ff1": w_ff1[l].astype(BF16), "w_ff2": w_ff2[l].astype(BF16),
        }
        y_p, *st_p = _layer(y_p, None, None, None, None, wts, f"p{l}")
        y_s, *st_s = _layer(y_s, cache_k[l], cache_v[l], state_conv[l], state_h[l], wts, f"s{l}")
        outs_p.append(st_p)
        outs_s.append(st_s)
    stack = lambda outs, i: jnp.stack([o[i] for o in outs])
    return (y_p, y_s,
            stack(outs_p, 0), stack(outs_p, 1), stack(outs_p, 2), stack(outs_p, 3),
            stack(outs_s, 0), stack(outs_s, 1), stack(outs_s, 2), stack(outs_s, 3))
```

```python
import functools

import jax
import jax.numpy as jnp
from jax import lax
from jax.experimental import pallas as pl
from jax.experimental.pallas import tpu as pltpu

F32 = jnp.float32
BF16 = jnp.bfloat16

CHUNK = 64
PAST_CHUNKS = 8
KV_REACH = PAST_CHUNKS * CHUNK
N_HEADS = 16
HEAD_DIM = 64
D_ATTN = N_HEADS * HEAD_DIM
MAX_REL = 256
RNN_BLOCK = 128
CONV_WIDTH = 4
LRU_C = 8.0
EPS = 1e-6
NEG_INF = -1e30

LANES = 128
SUBLANES = 8
MIB = 1024 * 1024


def _params(n_axes, vmem_mib):
    return pltpu.CompilerParams(
        dimension_semantics=("arbitrary",) * n_axes,
        vmem_limit_bytes=vmem_mib * MIB,
    )


def _rmsnorm_rows(x, g):
    ms = jnp.mean(x * x, axis=-1, keepdims=True)
    return x * lax.rsqrt(ms + EPS) * g


def _in_proj_body(x_ref, g_ref, w_ref, o_ref, s_ref, xn_ref, *, period, j0, j1):
    i, j = pl.program_id(0), pl.program_id(1)

    @pl.when(j == 0)
    def _():
        xn_ref[...] = _rmsnorm_rows(x_ref[...], g_ref[...]).astype(BF16)

    keep_f32 = (i % period == period - 1) & (j >= j0) & (j < j1)

    @pl.when(keep_f32)
    def _():
        acc = jnp.dot(xn_ref[...], w_ref[...], preferred_element_type=F32)
        o_ref[...] = acc.astype(o_ref.dtype)
        s_ref[...] = acc[acc.shape[0] - s_ref.shape[0]:, :]

    @pl.when(jnp.logical_not(keep_f32))
    def _():
        o_ref[...] = jnp.dot(xn_ref[...], w_ref[...], preferred_element_type=F32).astype(o_ref.dtype)


def _in_proj(x2d, g, w, *, tm, tn, period, state_rows, state_col0, state_cols, name):
    m, d = x2d.shape
    n = w.shape[1]
    j0, j1 = state_col0 // tn, (state_col0 + state_cols) // tn
    n_state_tiles = m // tm // period

    def state_block(i, j):
        live = i % period == period - 1
        return i // period, jnp.where(live, jnp.clip(j - j0, 0, j1 - j0 - 1), 0)

    return pl.pallas_call(
        functools.partial(_in_proj_body, period=period, j0=j0, j1=j1),
        grid=(m // tm, n // tn),
        in_specs=[
            pl.BlockSpec((tm, d), lambda i, j: (i, 0)),
            pl.BlockSpec((1, d), lambda i, j: (0, 0)),
            pl.BlockSpec((d, tn), lambda i, j: (0, j)),
        ],
        out_specs=[
            pl.BlockSpec((tm, tn), lambda i, j: (i, j)),
            pl.BlockSpec((state_rows, tn), state_block),
        ],
        out_shape=[
            jax.ShapeDtypeStruct((m, n), BF16),
            jax.ShapeDtypeStruct((n_state_tiles * state_rows, state_cols), F32),
        ],
        scratch_shapes=[pltpu.VMEM((tm, d), BF16)],
        compiler_params=_params(2, 52),
        name=name,
    )(x2d, g, w)


def _bias_diagonals(rel_bias, width, offsets, heads):
    x = jnp.arange(width)
    rows = [rel_bias[:, jnp.clip(off + CHUNK - 1 - x, -MAX_REL, MAX_REL) + MAX_REL] for off in offsets]
    g = jnp.stack(rows, axis=1).astype(F32)
    n_groups = rel_bias.shape[0] // heads
    g = g.reshape(n_groups, heads, len(offsets), width)
    return jnp.swapaxes(g, 1, 2).reshape(n_groups, len(offsets) * heads, width)


def _attn_body(*refs, bb, n_chunks, kw, heads, offsets, unit, cached):
    if cached:
        q_ref, kc_ref, vc_ref, k_ref, v_ref, g_ref, o_ref, tab_ref, s_ref, p_ref, l_ref = refs
        band = lambda c_ref, n_ref, bi, ks: jnp.concatenate([c_ref[bi], n_ref[bi]], axis=0)
    else:
        q_ref, k_ref, v_ref, g_ref, o_ref, tab_ref, s_ref, p_ref, l_ref = refs
        kc_ref = vc_ref = None
        band = lambda c_ref, n_ref, bi, ks: n_ref[bi, pl.ds(ks, kw), :]
    width = g_ref.shape[2]
    hw = heads * HEAD_DIM

    @pl.when(pl.program_id(1) == 0)
    def _():
        r = lax.broadcasted_iota(jnp.int32, (CHUNK, kw), 0)
        j = lax.broadcasted_iota(jnp.int32, (CHUNK, kw), 1)
        for c, off in enumerate(offsets):
            cq = (r + off) // CHUNK
            ck = j // CHUNK
            valid = (ck <= cq) & (ck >= cq - PAST_CHUNKS)
            for h in range(heads):
                g = jnp.broadcast_to(g_ref[0, c * heads + h:c * heads + h + 1, :], (CHUNK, width))
                t = pltpu.roll(g, width - (CHUNK - 1), 1, stride=1, stride_axis=0)
                tab_ref[c, h * CHUNK:(h + 1) * CHUNK, :] = jnp.where(valid, t[:, :kw], NEG_INF)

    lane = lax.broadcasted_iota(jnp.int32, (1, hw), 1)
    head_lanes = [(lane >= h * HEAD_DIM) & (lane < (h + 1) * HEAD_DIM) for h in range(heads)]
    rows = heads * CHUNK
    n_units = bb * n_chunks // unit

    def locate(u, j):
        idx = u * unit + j
        bi, c = (0, idx) if bb == 1 else (idx // n_chunks, idx % n_chunks)
        q0 = pl.multiple_of(c * CHUNK, CHUNK)
        ks = pl.multiple_of(jnp.maximum(c * CHUNK - KV_REACH, 0), CHUNK)
        return bi, c, q0, ks

    def scores(u, slot):
        for j in range(unit):
            bi, c, q0, ks = locate(u, j)
            q = q_ref[bi, pl.ds(q0, CHUNK), :]
            qs = jnp.concatenate([jnp.where(m, q, jnp.zeros_like(q)) for m in head_lanes], axis=0)
            qs = qs * jnp.asarray(HEAD_DIM ** -0.5, q.dtype)
            kb = band(kc_ref, k_ref, bi, ks)
            s = lax.dot_general(qs, kb, (((1,), (1,)), ((), ())), preferred_element_type=F32)
            s_ref[slot, j * rows:(j + 1) * rows, :] = s + tab_ref[jnp.minimum(c, len(offsets) - 1)]

    def softmax(slot):
        s = s_ref[slot]
        p = jnp.exp(s - jnp.max(s, axis=-1, keepdims=True))
        p_ref[slot] = p.astype(p_ref.dtype)
        l_ref[slot] = 1.0 / jnp.sum(p, axis=-1, keepdims=True)

    def output(u, slot):
        for j in range(unit):
            bi, c, q0, ks = locate(u, j)
            vb = band(vc_ref, v_ref, bi, ks)
            o_all = jnp.dot(p_ref[slot, j * rows:(j + 1) * rows, :], vb, preferred_element_type=F32)
            o_all = o_all * l_ref[slot, j * rows:(j + 1) * rows, :]
            o = o_all[0:CHUNK]
            for h in range(1, heads):
                o = jnp.where(head_lanes[h], o_all[h * CHUNK:(h + 1) * CHUNK], o)
            o_ref[bi, pl.ds(q0, CHUNK), :] = o.astype(o_ref.dtype)

    def step(i, slot, do_scores, do_softmax, do_output):
        if do_scores:
            scores(i, slot)
        if do_output:
            output(i - 2, slot)
        if do_softmax:
            softmax(1 - slot)

    for i in range(2):
        step(i, i % 2, i < n_units, 1 <= i <= n_units, False)
    n_pairs = max(n_units - 2, 0) // 2
    if n_pairs:
        def body(t, carry):
            i = 2 + 2 * t
            step(i, 0, True, True, True)
            step(i + 1, 1, True, True, True)
            return carry
        lax.fori_loop(0, n_pairs, body, 0)
    for i in range(2 + 2 * n_pairs, n_units + 2):
        step(i, i % 2, i < n_units, 1 <= i <= n_units, i >= 2)


def _attention(q_arr, k_arr, v_arr, rel_bias, *, q_col0, k_col0, v_col0, bb, heads, offsets, name,
               k_cache=None, v_cache=None):
    b, sq, _ = q_arr.shape
    sk = k_arr.shape[1]
    kw = KV_REACH + CHUNK
    cached = k_cache is not None
    if cached:
        assert sq == CHUNK and sk == CHUNK and k_cache.shape[1] == KV_REACH
    else:
        assert sk >= kw and sq % CHUNK == 0
    hw = heads * HEAD_DIM
    n_groups = N_HEADS // heads
    n_case = len(offsets)
    width = -(-(kw + CHUNK - 1) // LANES) * LANES
    diag = _bias_diagonals(rel_bias, width, offsets, heads)
    qc, kc, vc = q_col0 // hw, k_col0 // hw, v_col0 // hw
    unit = 2
    assert (bb * sq // CHUNK) % unit == 0
    rows = unit * heads * CHUNK
    cache_specs = [pl.BlockSpec((bb, KV_REACH, hw), lambda g, i: (i, 0, g))] * 2 if cached else []
    cache_args = [k_cache, v_cache] if cached else []
    return pl.pallas_call(
        functools.partial(_attn_body, bb=bb, n_chunks=sq // CHUNK, kw=kw, heads=heads,
                          offsets=tuple(offsets), unit=unit, cached=cached),
        grid=(n_groups, b // bb),
        in_specs=[pl.BlockSpec((bb, sq, hw), lambda g, i: (i, 0, qc + g))] + cache_specs + [
            pl.BlockSpec((bb, sk, hw), lambda g, i: (i, 0, kc + g)),
            pl.BlockSpec((bb, sk, hw), lambda g, i: (i, 0, vc + g)),
            pl.BlockSpec((1, n_case * heads, width), lambda g, i: (g, 0, 0)),
        ],
        out_specs=pl.BlockSpec((bb, sq, hw), lambda g, i: (i, 0, g)),
        out_shape=jax.ShapeDtypeStruct((b, sq, D_ATTN), BF16),
        scratch_shapes=[
            pltpu.VMEM((n_case, heads * CHUNK, kw), F32),
            pltpu.VMEM((2, rows, kw), F32),
            pltpu.VMEM((2, rows, kw), BF16),
            pltpu.VMEM((2, rows, 1), F32),
        ],
        compiler_params=_params(2, 40),
        name=name,
    )(q_arr, *cache_args, k_arr, v_arr, diag)


def _log_sigmoid(x):
    return jnp.minimum(x, 0.0) - jnp.log1p(jnp.exp(-jnp.abs(x)))


def _shift_matrix(tt):
    t = jnp.arange(tt)
    blocks = [(t[:, None] - d) == t[None, :] for d in range(CONV_WIDTH - 1, 0, -1)]
    return jnp.concatenate(blocks, axis=0).astype(BF16)


def _rglru_head(x_bf, prev8, shift, conv_w, conv_b, wg_ref, b_rg, b_ig):
    tt, cw = x_bf.shape
    taps = CONV_WIDTH - 1
    x = x_bf.astype(F32)
    row = lax.broadcasted_iota(jnp.int32, (SUBLANES, cw), 0)
    if shift is not None:
        sh = jnp.dot(shift, x_bf, preferred_element_type=F32)
        xc = conv_b + conv_w[0:1] * sh[0:tt]
        for k in range(1, taps):
            xc = xc + conv_w[k:k + 1] * sh[k * tt:(k + 1) * tt]
        xc = xc + conv_w[taps:taps + 1] * x
        head = xc[0:SUBLANES]
        for k in range(taps):
            d = taps - k
            head = head + conv_w[k:k + 1] * jnp.where(row < d, pltpu.roll(prev8, d, 0), 0.0)
        xc = jnp.concatenate([head, xc[SUBLANES:]], axis=0)
    else:
        xc = conv_b
        for k in range(taps):
            d = taps - k
            sh = pltpu.roll(x, d, 0)
            head = jnp.where(row < d, pltpu.roll(prev8, d, 0), sh[0:SUBLANES])
            xc = xc + conv_w[k:k + 1] * jnp.concatenate([head, sh[SUBLANES:]], axis=0)
        xc = xc + conv_w[taps:taps + 1] * x

    xcb = xc.astype(BF16)
    r_parts, i_parts = [], []
    for n in range(cw // RNN_BLOCK):
        gts = jnp.dot(xcb[:, n * RNN_BLOCK:(n + 1) * RNN_BLOCK], wg_ref[n], preferred_element_type=F32)
        r_parts.append(gts[:, :RNN_BLOCK])
        i_parts.append(gts[:, RNN_BLOCK:])
    r_pre = jnp.concatenate(r_parts, axis=1) + b_rg
    i_pre = jnp.concatenate(i_parts, axis=1) + b_ig
    return xc, r_pre, i_pre, x[tt - SUBLANES:tt]


def _rglru_tail(xc, r_pre, i_pre, gr_bf, h_in, lam):
    rows, cw = xc.shape
    r = jax.nn.sigmoid(r_pre)
    i = jax.nn.sigmoid(i_pre)
    log_a = r * (LRU_C * _log_sigmoid(lam))
    a = jnp.exp(log_a)
    w = -jnp.tanh(log_a)
    one_minus_a2 = 2.0 * w / (1.0 + w)
    root = jnp.where(one_minus_a2 > 0.0, one_minus_a2 * lax.rsqrt(one_minus_a2), 0.0)
    u = root * (i * xc)

    row = lax.broadcasted_iota(jnp.int32, (SUBLANES, cw), 0)
    h = h_in
    hs = []
    for g in range(rows // SUBLANES):
        ag = a[g * SUBLANES:(g + 1) * SUBLANES]
        ug = u[g * SUBLANES:(g + 1) * SUBLANES]
        for d in (1, 2, 4):
            keep = row >= d
            a_prev = jnp.where(keep, pltpu.roll(ag, d, 0), 1.0)
            u_prev = jnp.where(keep, pltpu.roll(ug, d, 0), 0.0)
            ug = ag * u_prev + ug
            ag = ag * a_prev
        hg = ug + ag * h
        hs.append(hg)
        h = jnp.broadcast_to(hg[SUBLANES - 1:SUBLANES, :], (SUBLANES, cw))
    gate = jax.nn.gelu(gr_bf.astype(F32), approximate=True)
    return (jnp.concatenate(hs, axis=0) * gate).astype(BF16), h


def _rglru_body(xr_ref, gr_ref, c0_ref, h0_ref, sh_ref, cw_ref, cb_ref, wg_ref, brg_ref, big_ref, lam_ref,
                ob_ref, hl_ref, prev_ref, hcar_ref):
    @pl.when(pl.program_id(2) == 0)
    def _():
        hcar_ref[...] = jnp.broadcast_to(h0_ref[0], hcar_ref.shape)
        prev_ref[...] = c0_ref[0]

    xc, r_pre, i_pre, prev8 = _rglru_head(xr_ref[0], prev_ref[...], sh_ref[...], cw_ref[...], cb_ref[...],
                                          wg_ref, brg_ref[...], big_ref[...])
    out, h = _rglru_tail(xc, r_pre, i_pre, gr_ref[0], hcar_ref[...], lam_ref[...])
    ob_ref[0] = out
    prev_ref[...] = prev8
    hcar_ref[...] = h
    hl_ref[0] = h[0:1, :]


def _rglru(z3, conv0, h0, conv_w, conv_b, w_gates, b_rg, b_ig, lam, *, xr_col0, gr_col0, tt, cw, name):
    b, t, _ = z3.shape
    d_rnn = conv_w.shape[1]
    n_cg = d_rnn // cw
    blocks_per_cg = cw // RNN_BLOCK
    xc0, gc0 = xr_col0 // cw, gr_col0 // cw
    vec = pl.BlockSpec((1, cw), lambda c, i, s: (0, c))
    return pl.pallas_call(
        _rglru_body,
        grid=(n_cg, b, t // tt),
        in_specs=[
            pl.BlockSpec((1, tt, cw), lambda c, i, s: (i, s, xc0 + c)),
            pl.BlockSpec((1, tt, cw), lambda c, i, s: (i, s, gc0 + c)),
            pl.BlockSpec((1, SUBLANES, cw), lambda c, i, s: (i, 0, c)),
            pl.BlockSpec((1, 1, cw), lambda c, i, s: (i, 0, c)),
            pl.BlockSpec(((CONV_WIDTH - 1) * tt, tt), lambda c, i, s: (0, 0)),
            pl.BlockSpec((CONV_WIDTH, cw), lambda c, i, s: (0, c)),
            vec,
            pl.BlockSpec((blocks_per_cg, RNN_BLOCK, 2 * RNN_BLOCK), lambda c, i, s: (c, 0, 0)),
            vec, vec, vec,
        ],
        out_specs=[
            pl.BlockSpec((1, tt, cw), lambda c, i, s: (i, s, c)),
            pl.BlockSpec((1, 1, cw), lambda c, i, s: (i, 0, c)),
        ],
        out_shape=[
            jax.ShapeDtypeStruct((b, t, d_rnn), BF16),
            jax.ShapeDtypeStruct((b, 1, d_rnn), F32),
        ],
        scratch_shapes=[
            pltpu.VMEM((SUBLANES, cw), F32),
            pltpu.VMEM((SUBLANES, cw), F32),
        ],
        compiler_params=_params(3, 40),
        name=name,
    )(z3, z3, conv0, h0, _shift_matrix(tt), conv_w, conv_b, w_gates, b_rg, b_ig, lam)


def _lookup(table, j):
    out = jnp.int32(table[-1])
    for k in range(len(table) - 2, -1, -1):
        out = jnp.where(j == k, jnp.int32(table[k]), out)
    return out


def _fused_body(x_ref, g_ref, w_ref, c0_ref, h0_ref, cw_ref, cb_ref, wg_ref, brg_ref, big_ref,
                lam_ref, z_ref, s_ref, ob_ref, hl_ref, xn_ref, stash_ref, prev_ref, hcar_ref, hd_ref, gr_ref, *,
                n_tiles, period, n_cg, pieces_per_cg, tt, n_split, stash_slots):
    i, j = pl.program_id(0), pl.program_id(1)
    cw = ob_ref.shape[1]

    @pl.when((j == 0) & (i < n_tiles))
    def _():
        xn_ref[...] = _rmsnorm_rows(x_ref[...], g_ref[...]).astype(BF16)

    @pl.when((j == 0) & (i == 0))
    def _():
        prev_ref[...] = jnp.zeros_like(prev_ref)
        hcar_ref[...] = jnp.zeros_like(hcar_ref)

    tm = xn_ref.shape[0]
    rows_c, rows_s = tm // n_split, tt // n_split
    state_start = tm - s_ref.shape[0]

    def project(c):
        lo = c * rows_c
        acc = jnp.dot(xn_ref[lo:lo + rows_c, :], w_ref[...], preferred_element_type=F32)
        zb = acc.astype(BF16)
        z_ref[lo:lo + rows_c, :] = zb
        if lo >= state_start:
            s_ref[lo - state_start:lo - state_start + rows_c, :] = acc
        stash_ref[_lookup(stash_slots, j), lo:lo + rows_c, :] = zb
        return acc[rows_c - SUBLANES:, :]

    def after(state, acc_rows):
        cg, lam, h = state
        zero = (pltpu.bitcast(acc_rows, jnp.uint32) >> 16) >> 16
        return cg, lam, pltpu.bitcast(pltpu.bitcast(h, jnp.uint32) + zero, F32)

    def recur_head():
        cg, tp = j // pieces_per_cg, j % pieces_per_cg
        r0 = pl.multiple_of(tp * tt, tt)
        first = ((i - 1) % period == 0) & (tp == 0)
        prev8 = jnp.where(first, c0_ref[0, cg], prev_ref[cg])
        h = jnp.where(first, jnp.broadcast_to(h0_ref[0, cg], (SUBLANES, cw)), hcar_ref[cg])
        xc, r_pre, i_pre, prev8 = _rglru_head(stash_ref[cg, pl.ds(r0, tt), :], prev8, None,
                                              cw_ref[cg], cb_ref[cg], wg_ref.at[cg], brg_ref[cg], big_ref[cg])
        hd_ref[0] = xc
        hd_ref[1] = r_pre
        hd_ref[2] = i_pre
        gr_ref[...] = stash_ref[n_cg + cg, pl.ds(r0, tt), :]
        prev_ref[cg] = prev8
        return cg, lam_ref[cg], h

    def recur_tail(k, state):
        cg, lam, h = state
        lo = k * rows_s
        out, h = _rglru_tail(hd_ref[0, lo:lo + rows_s, :], hd_ref[1, lo:lo + rows_s, :],
                             hd_ref[2, lo:lo + rows_s, :], gr_ref[lo:lo + rows_s, :], h, lam)
        ob_ref[lo:lo + rows_s, :] = out
        return cg, lam, h

    def recur_end(state):
        cg, _, h = state
        hcar_ref[cg] = h
        hl_ref[0, 0] = h[0:1, :]

    has_proj = i < n_tiles
    has_rec = (i >= 1) & (j < n_cg * pieces_per_cg)

    @pl.when(has_proj & has_rec)
    def _():
        state = recur_head()
        for k in range(n_split):
            state = recur_tail(k, state)
            state = after(state, project(k))
        recur_end(state)

    @pl.when(has_proj & jnp.logical_not(has_rec))
    def _():
        for k in range(n_split):
            project(k)

    @pl.when(jnp.logical_not(has_proj) & has_rec)
    def _():
        state = recur_head()
        for k in range(n_split):
            state = recur_tail(k, state)
        recur_end(state)


def _in_proj_rglru(x2d, g, w, conv0, h0, conv_w, conv_b, w_gates, b_rg, b_ig, lam, *, tm, tn, tt, period,
                   state_rows, state_col0, state_cols, xr_col0, gr_col0, name):
    m, d = x2d.shape
    n = w.shape[1]
    d_rnn = conv_w.shape[1]
    n_tiles, n_col, n_cg = m // tm, n // tn, d_rnn // tn
    n_seq = n_tiles // period
    pieces_per_cg = tm // tt
    n_pieces = n_cg * pieces_per_cg
    xr0, gr0 = xr_col0 // tn, gr_col0 // tn
    assert n_pieces + n_cg <= n_col and tm % tt == 0 and n_tiles % period == 0

    order, stash_slots = [None] * n_col, [2 * n_cg] * n_col
    for c in range(n_cg):
        jx = (c + 1) * pieces_per_cg
        order[jx], stash_slots[jx] = xr0 + c, c
        order[jx + 1], stash_slots[jx + 1] = gr0 + c, n_cg + c
    rest = [c for c in range(n_col) if c not in order]
    for jj in range(n_col):
        if order[jj] is None:
            order[jj] = rest.pop(0)
    s0, n_state = state_col0 // tn, state_cols // tn
    state_pos = [order.index(s0 + k) for k in range(n_state)]
    assert state_pos == sorted(state_pos)
    state_blk = [sum(p < jj for p in state_pos) for jj in range(n_col)]

    last = lambda i: i == n_tiles
    row = lambda i: jnp.minimum(i, n_tiles - 1)
    col = lambda i, j: jnp.where(last(i), order[-1], _lookup(order, j))
    seq_prev = lambda i: jnp.maximum(i - 1, 0) // period
    piece = lambda i, j: jnp.where(i == 0, 0, jnp.minimum(j, n_pieces - 1))

    def state_block(i, j):
        live = i % period == period - 1
        blk = jnp.where(live, _lookup(state_blk, j), n_state + 1)
        return row(i) // period, jnp.where(last(i), n_state, blk)

    per_cg = lambda a: a.reshape(a.shape[0], n_cg, tn).swapaxes(0, 1)
    whole = lambda a: pl.BlockSpec(a.shape, lambda i, j: (0,) * a.ndim)
    n_split = 4
    assert tt % (n_split * SUBLANES) == 0 and state_rows % (tm // n_split) == 0
    params = [per_cg(conv_w), per_cg(conv_b), w_gates.reshape(n_cg, -1, RNN_BLOCK, 2 * RNN_BLOCK),
              per_cg(b_rg), per_cg(b_ig), per_cg(lam)]
    conv0 = conv0.reshape(n_seq, SUBLANES, n_cg, tn).swapaxes(1, 2)
    h0 = h0.reshape(n_seq, 1, n_cg, tn).swapaxes(1, 2)

    z, state, o_b, h_last = pl.pallas_call(
        functools.partial(_fused_body, n_tiles=n_tiles, period=period, n_cg=n_cg,
                          pieces_per_cg=pieces_per_cg, tt=tt, n_split=n_split,
                          stash_slots=tuple(stash_slots)),
        grid=(n_tiles + 1, n_col),
        in_specs=[
            pl.BlockSpec((tm, d), lambda i, j: (row(i), 0)),
            pl.BlockSpec((1, d), lambda i, j: (0, 0)),
            pl.BlockSpec((d, tn), lambda i, j: (0, col(i, j))),
            pl.BlockSpec((1, n_cg, SUBLANES, tn), lambda i, j: (seq_prev(i), 0, 0, 0)),
            pl.BlockSpec((1, n_cg, 1, tn), lambda i, j: (seq_prev(i), 0, 0, 0)),
        ] + [whole(p) for p in params],
        out_specs=[
            pl.BlockSpec((tm, tn), lambda i, j: (row(i), col(i, j))),
            pl.BlockSpec((state_rows, tn), state_block),
            pl.BlockSpec((tt, tn), lambda i, j: (jnp.maximum(i - 1, 0) * pieces_per_cg
                                                 + piece(i, j) % pieces_per_cg, piece(i, j) // pieces_per_cg)),
            pl.BlockSpec((1, 1, 1, tn), lambda i, j: (jnp.maximum(i - 1, 0), piece(i, j) // pieces_per_cg, 0, 0)),
        ],
        out_shape=[
            jax.ShapeDtypeStruct((m, n), BF16),
            jax.ShapeDtypeStruct((n_seq * state_rows, (n_state + 2) * tn), F32),
            jax.ShapeDtypeStruct((m, d_rnn), BF16),
            jax.ShapeDtypeStruct((n_tiles, n_cg, 1, tn), F32),
        ],
        scratch_shapes=[
            pltpu.VMEM((tm, d), BF16),
            pltpu.VMEM((2 * n_cg + 1, tm, tn), BF16),
            pltpu.VMEM((n_cg, SUBLANES, tn), F32),
            pltpu.VMEM((n_cg, SUBLANES, tn), F32),
            pltpu.VMEM((3, tt, tn), F32),
            pltpu.VMEM((tt, tn), BF16),
        ],
        compiler_params=_params(2, 58),
        name=name,
    )(x2d, g, w, conv0, h0, *params)
    return z, state[:, :state_cols], o_b, h_last.reshape(n_tiles, d_rnn)[period - 1::period]


def _merge_body(oa_ref, ob_ref, ga_ref, gb_ref, wa_ref, wb_ref, o_ref):
    ya = jnp.dot(oa_ref[...], wa_ref[...], preferred_element_type=F32)
    yb = jnp.dot(ob_ref[...], wb_ref[...], preferred_element_type=F32)
    ga = jax.nn.sigmoid(ga_ref[...].astype(F32))
    gb = jax.nn.sigmoid(gb_ref[...].astype(F32))
    o_ref[...] = (ga * ya + gb * yb).astype(o_ref.dtype)


def _gate_merge(o_a, o_b, z2, w_a, w_b, *, ga_col0, gb_col0, tm, tn, name):
    m = o_a.shape[0]
    d = w_a.shape[1]
    ga0, gb0 = ga_col0 // tn, gb_col0 // tn
    return pl.pallas_call(
        _merge_body,
        grid=(m // tm, d // tn),
        in_specs=[
            pl.BlockSpec((tm, o_a.shape[1]), lambda i, j: (i, 0)),
            pl.BlockSpec((tm, o_b.shape[1]), lambda i, j: (i, 0)),
            pl.BlockSpec((tm, tn), lambda i, j: (i, ga0 + j)),
            pl.BlockSpec((tm, tn), lambda i, j: (i, gb0 + j)),
            pl.BlockSpec((w_a.shape[0], tn), lambda i, j: (0, j)),
            pl.BlockSpec((w_b.shape[0], tn), lambda i, j: (0, j)),
        ],
        out_specs=pl.BlockSpec((tm, tn), lambda i, j: (i, j)),
        out_shape=jax.ShapeDtypeStruct((m, d), BF16),
        compiler_params=_params(2, 48),
        name=name,
    )(o_a, o_b, z2, z2, w_a, w_b)


def _out_norm_body(m_ref, w_ref, x_ref, g_ref, o_ref):
    y = jnp.dot(m_ref[...], w_ref[...], preferred_element_type=F32)
    o_ref[...] = x_ref[...] + _rmsnorm_rows(y, g_ref[...])


def _out_norm(merged, w_out, x2d, g, *, tm, name):
    m, d = x2d.shape
    return pl.pallas_call(
        _out_norm_body,
        grid=(m // tm,),
        in_specs=[
            pl.BlockSpec((tm, d), lambda i: (i, 0)),
            pl.BlockSpec((d, d), lambda i: (0, 0)),
            pl.BlockSpec((tm, d), lambda i: (i, 0)),
            pl.BlockSpec((1, d), lambda i: (0, 0)),
        ],
        out_specs=pl.BlockSpec((tm, d), lambda i: (i, 0)),
        out_shape=jax.ShapeDtypeStruct((m, d), F32),
        compiler_params=_params(1, 48),
        name=name,
    )(merged, w_out, x2d, g)


def _ffn_body(x_ref, g1_ref, w1_ref, w2_ref, g2_ref, o_ref, xn_ref):
    f = pl.program_id(1)

    @pl.when(f == 0)
    def _():
        xn_ref[...] = _rmsnorm_rows(x_ref[...], g1_ref[...]).astype(BF16)
        o_ref[...] = jnp.zeros_like(o_ref)

    hid = jnp.dot(xn_ref[...], w1_ref[...], preferred_element_type=F32)
    hid = jnp.square(jnp.maximum(hid, 0.0)).astype(BF16)
    o_ref[...] += jnp.dot(hid, w2_ref[...], preferred_element_type=F32)

    @pl.when(f == pl.num_programs(1) - 1)
    def _():
        o_ref[...] = x_ref[...] + _rmsnorm_rows(o_ref[...], g2_ref[...])


def _ffn(x2d, g1, w1, w2, g2, *, tm, tf, name):
    m, d = x2d.shape
    d_ff = w1.shape[1]
    return pl.pallas_call(
        _ffn_body,
        grid=(m // tm, d_ff // tf),
        in_specs=[
            pl.BlockSpec((tm, d), lambda i, f: (i, 0)),
            pl.BlockSpec((1, d), lambda i, f: (0, 0)),
            pl.BlockSpec((d, tf), lambda i, f: (0, f)),
            pl.BlockSpec((tf, d), lambda i, f: (f, 0)),
            pl.BlockSpec((1, d), lambda i, f: (0, 0)),
        ],
        out_specs=pl.BlockSpec((tm, d), lambda i, f: (i, 0)),
        out_shape=jax.ShapeDtypeStruct((m, d), F32),
        scratch_shapes=[pltpu.VMEM((tm, d), BF16)],
        compiler_params=_params(2, 48),
        name=name,
    )(x2d, g1, w1, w2, g2)


def _pick_tile(n, target):
    t = min(n, target)
    while n % t:
        t //= 2
    return t


def _layer(x, cache_k, cache_v, conv_state, h_state, wts, tag):
    b, t, d = x.shape
    m = b * t
    x2d = x.reshape(m, d)
    d_rnn = wts["conv_w"].shape[1]
    n_in = wts["w_in"].shape[1]
    col_k, col_v, col_xr = D_ATTN, 2 * D_ATTN, 3 * D_ATTN
    col_gr = col_xr + d_rnn
    col_ga = col_gr + d_rnn
    col_gb = col_ga + d

    tm = _pick_tile(m, 1024)
    tn = 1024
    keep = min(KV_REACH, t)
    if keep == t:
        period, state_rows = 1, tm
    else:
        assert t % tm == 0 and keep <= tm
        period, state_rows = t // tm, keep
    rnn_w = (wts["conv_w"], wts["conv_b"], wts["w_gates"], wts["b_rg"], wts["b_ig"], wts["lru_lambda"])
    heads = 2
    if cache_k is None:
        conv0 = jnp.zeros((b, SUBLANES, d_rnn), F32)
        h0 = jnp.zeros((b, 1, d_rnn), F32)
        z2, state, o_b, h_last = _in_proj_rglru(
            x2d, wts["pre_mix_g"], wts["w_in"], conv0, h0, *rnn_w, tm=tm, tn=tn, tt=_pick_tile(tm, 256),
            period=period, state_rows=state_rows, state_col0=col_k, state_cols=col_gr - col_k,
            xr_col0=col_xr, gr_col0=col_gr, name=f"in_proj_rglru_{tag}")
        z3 = z2.reshape(b, t, n_in)
        offsets = [i * CHUNK for i in range(PAST_CHUNKS + 1)]
        o_a = _attention(z3, z3, z3, wts["rel_bias"], q_col0=0, k_col0=col_k, v_col0=col_v, bb=1,
                         heads=heads, offsets=offsets, name=f"attn_{tag}")
    else:
        z2, state = _in_proj(x2d, wts["pre_mix_g"], wts["w_in"], tm=tm, tn=tn, period=period,
                             state_rows=state_rows, state_col0=col_k, state_cols=col_gr - col_k,
                             name=f"in_proj_{tag}")
        z3 = z2.reshape(b, t, n_in)
        n_cached = cache_k.shape[1]
        assert t == CHUNK and n_cached == KV_REACH
        o_a = _attention(z3, z3, z3, wts["rel_bias"], q_col0=0, k_col0=col_k, v_col0=col_v,
                         bb=_pick_tile(b, 8), heads=heads, offsets=[n_cached], name=f"attn_{tag}",
                         k_cache=cache_k.astype(BF16).reshape(b, n_cached, D_ATTN),
                         v_cache=cache_v.astype(BF16).reshape(b, n_cached, D_ATTN))
        conv0 = jnp.pad(conv_state.astype(F32), ((0, 0), (SUBLANES - (CONV_WIDTH - 1), 0), (0, 0)))
        h0 = h_state.astype(F32).reshape(b, 1, d_rnn)
        o_b, h_last = _rglru(z3, conv0, h0, *rnn_w, xr_col0=col_xr, gr_col0=col_gr,
                             tt=_pick_tile(t, 256), cw=1024, name=f"rglru_{tag}")
    state = state.reshape(b, keep, col_gr - col_k)
    new_k = state[:, :, :D_ATTN].reshape(b, keep, N_HEADS, HEAD_DIM)
    new_v = state[:, :, D_ATTN:2 * D_ATTN].reshape(b, keep, N_HEADS, HEAD_DIM)
    conv_tail = state[:, keep - (CONV_WIDTH - 1):, 2 * D_ATTN:]

    merged = _gate_merge(o_a.reshape(m, D_ATTN), o_b.reshape(m, d_rnn), z2, wts["w_attn_up"],
                         wts["w_rnn_up"], ga_col0=col_ga, gb_col0=col_gb, tm=tm, tn=tn,
                         name=f"gate_merge_{tag}")
    tm2 = _pick_tile(m, 512)
    x1 = _out_norm(merged, wts["w_out"], x2d, wts["post_mix_g"], tm=tm2, name=f"out_norm_{tag}")
    y = _ffn(x1, wts["pre_ffn_g"], wts["w_ff1"], wts["w_ff2"], wts["post_ffn_g"], tm=tm2, tf=1024,
             name=f"ffn_{tag}")
    return y.reshape(b, t, d), new_k, new_v, conv_tail, h_last.reshape(b, d_rnn)


def kernel(x_prompt, x_sample, cache_k, cache_v, state_conv, state_h, pre_mix_g, w_in, rel_bias, conv_w, conv_b, w_rg, b_rg, w_ig, b_ig, lru_lambda, w_attn_up, w_rnn_up, w_out, post_mix_g, pre_ffn_g, w_ff1, w_ff2, post_ffn_g):
    depth = w_in.shape[0]
    y_p, y_s = x_prompt, x_sample
    outs_p, outs_s = [], []
    for l in range(depth):
        wts = {
            "pre_mix_g": pre_mix_g[l][None], "post_mix_g": post_mix_g[l][None],
            "pre_ffn_g": pre_ffn_g[l][None], "post_ffn_g": post_ffn_g[l][None],
            "w_in": w_in[l].astype(BF16), "rel_bias": rel_bias[l],
            "conv_w": conv_w[l], "conv_b": conv_b[l][None],
            "w_gates": jnp.concatenate([w_rg[l], w_ig[l]], axis=-1).astype(BF16),
            "b_rg": b_rg[l][None], "b_ig": b_ig[l][None], "lru_lambda": lru_lambda[l][None],
            "w_attn_up": w_attn_up[l].astype(BF16), "w_rnn_up": w_rnn_up[l].astype(BF16),
            "w_out": w_out[l].astype(BF16), "w_ff1": w_ff1[l].astype(BF16), "w_ff2": w_ff2[l].astype(BF16),
        }
        y_p, *st_p = _layer(y_p, None, None, None, None, wts, f"p{l}")
        y_s, *st_s = _layer(y_s, cache_k[l], cache_v[l], state_conv[l], state_h[l], wts, f"s{l}")
        outs_p.append(st_p)
        outs_s.append(st_s)
    stack = lambda outs, i: jnp.stack([o[i] for o in outs])
    return (y_p, y_s,
            stack(outs_p, 0), stack(outs_p, 1), stack(outs_p, 2), stack(outs_p, 3),
            stack(outs_s, 0), stack(outs_s, 1), stack(outs_s, 2), stack(outs_s, 3))
```

```python
import functools

import jax
import jax.numpy as jnp
from jax import lax
from jax.experimental import pallas as pl
from jax.experimental.pallas import tpu as pltpu

F32 = jnp.float32
BF16 = jnp.bfloat16

CHUNK = 64
PAST_CHUNKS = 8
KV_REACH = PAST_CHUNKS * CHUNK
N_HEADS = 16
HEAD_DIM = 64
D_ATTN = N_HEADS * HEAD_DIM
MAX_REL = 256
RNN_BLOCK = 128
CONV_WIDTH = 4
LRU_C = 8.0
EPS = 1e-6
NEG_INF = -1e30

LANES = 128
SUBLANES = 8
MIB = 1024 * 1024


def _params(n_axes, vmem_mib):
    return pltpu.CompilerParams(
        dimension_semantics=("arbitrary",) * n_axes,
        vmem_limit_bytes=vmem_mib * MIB,
    )


def _rmsnorm_rows(x, g):
    ms = jnp.mean(x * x, axis=-1, keepdims=True)
    return x * lax.rsqrt(ms + EPS) * g


def _in_proj_body(x_ref, g_ref, w_ref, o_ref, s_ref, xn_ref, *, period, j0, j1):
    i, j = pl.program_id(0), pl.program_id(1)

    @pl.when(j == 0)
    def _():
        xn_ref[...] = _rmsnorm_rows(x_ref[...], g_ref[...]).astype(BF16)

    keep_f32 = (i % period == period - 1) & (j >= j0) & (j < j1)

    @pl.when(keep_f32)
    def _():
        acc = jnp.dot(xn_ref[...], w_ref[...], preferred_element_type=F32)
        o_ref[...] = acc.astype(o_ref.dtype)
        s_ref[...] = acc[acc.shape[0] - s_ref.shape[0]:, :]

    @pl.when(jnp.logical_not(keep_f32))
    def _():
        o_ref[...] = jnp.dot(xn_ref[...], w_ref[...], preferred_element_type=F32).astype(o_ref.dtype)


def _in_proj(x2d, g, w, *, tm, tn, period, state_rows, state_col0, state_cols, name):
    m, d = x2d.shape
    n = w.shape[1]
    j0, j1 = state_col0 // tn, (state_col0 + state_cols) // tn
    n_state_tiles = m // tm // period

    def state_block(i, j):
        live = i % period == period - 1
        return i // period, jnp.where(live, jnp.clip(j - j0, 0, j1 - j0 - 1), 0)

    return pl.pallas_call(
        functools.partial(_in_proj_body, period=period, j0=j0, j1=j1),
        grid=(m // tm, n // tn),
        in_specs=[
            pl.BlockSpec((tm, d), lambda i, j: (i, 0)),
            pl.BlockSpec((1, d), lambda i, j: (0, 0)),
            pl.BlockSpec((d, tn), lambda i, j: (0, j)),
        ],
        out_specs=[
            pl.BlockSpec((tm, tn), lambda i, j: (i, j)),
            pl.BlockSpec((state_rows, tn), state_block),
        ],
        out_shape=[
            jax.ShapeDtypeStruct((m, n), BF16),
            jax.ShapeDtypeStruct((n_state_tiles * state_rows, state_cols), F32),
        ],
        scratch_shapes=[pltpu.VMEM((tm, d), BF16)],
        compiler_params=_params(2, 52),
        name=name,
    )(x2d, g, w)


def _bias_diagonals(rel_bias, width, offsets, heads):
    x = jnp.arange(width)
    rows = [rel_bias[:, jnp.clip(off + CHUNK - 1 - x, -MAX_REL, MAX_REL) + MAX_REL] for off in offsets]
    g = jnp.stack(rows, axis=1).astype(F32)
    n_groups = rel_bias.shape[0] // heads
    g = g.reshape(n_groups, heads, len(offsets), width)
    return jnp.swapaxes(g, 1, 2).reshape(n_groups, len(offsets) * heads, width)


def _attn_body(*refs, bb, n_chunks, kw, heads, offsets, unit, cached):
    if cached:
        q_ref, kc_ref, vc_ref, k_ref, v_ref, g_ref, o_ref, tab_ref, s_ref, p_ref, l_ref = refs
        band = lambda c_ref, n_ref, bi, ks: jnp.concatenate([c_ref[bi], n_ref[bi]], axis=0)
    else:
        q_ref, k_ref, v_ref, g_ref, o_ref, tab_ref, s_ref, p_ref, l_ref = refs
        kc_ref = vc_ref = None
        band = lambda c_ref, n_ref, bi, ks: n_ref[bi, pl.ds(ks, kw), :]
    width = g_ref.shape[2]
    hw = heads * HEAD_DIM

    @pl.when(pl.program_id(1) == 0)
    def _():
        r = lax.broadcasted_iota(jnp.int32, (CHUNK, kw), 0)
        j = lax.broadcasted_iota(jnp.int32, (CHUNK, kw), 1)
        for c, off in enumerate(offsets):
            cq = (r + off) // CHUNK
            ck = j // CHUNK
            valid = (ck <= cq) & (ck >= cq - PAST_CHUNKS)
            for h in range(heads):
                g = jnp.broadcast_to(g_ref[0, c * heads + h:c * heads + h + 1, :], (CHUNK, width))
                t = pltpu.roll(g, width - (CHUNK - 1), 1, stride=1, stride_axis=0)
                tab_ref[c, h * CHUNK:(h + 1) * CHUNK, :] = jnp.where(valid, t[:, :kw], NEG_INF)

    lane = lax.broadcasted_iota(jnp.int32, (1, hw), 1)
    head_lanes = [(lane >= h * HEAD_DIM) & (lane < (h + 1) * HEAD_DIM) for h in range(heads)]
    rows = heads * CHUNK
    n_units = bb * n_chunks // unit

    def locate(u, j):
        idx = u * unit + j
        bi, c = (0, idx) if bb == 1 else (idx // n_chunks, idx % n_chunks)
        q0 = pl.multiple_of(c * CHUNK, CHUNK)
        ks = pl.multiple_of(jnp.maximum(c * CHUNK - KV_REACH, 0), CHUNK)
        return bi, c, q0, ks

    def scores(u, slot):
        for j in range(unit):
            bi, c, q0, ks = locate(u, j)
            q = q_ref[bi, pl.ds(q0, CHUNK), :]
            qs = jnp.concatenate([jnp.where(m, q, jnp.zeros_like(q)) for m in head_lanes], axis=0)
            qs = qs * jnp.asarray(HEAD_DIM ** -0.5, q.dtype)
            kb = band(kc_ref, k_ref, bi, ks)
            s = lax.dot_general(qs, kb, (((1,), (1,)), ((), ())), preferred_element_type=F32)
            s_ref[slot, j * rows:(j + 1) * rows, :] = s + tab_ref[jnp.minimum(c, len(offsets) - 1)]

    def softmax(slot):
        s = s_ref[slot]
        p = jnp.exp(s - jnp.max(s, axis=-1, keepdims=True))
        p_ref[slot] = p.astype(p_ref.dtype)
        l_ref[slot] = 1.0 / jnp.sum(p, axis=-1, keepdims=True)

    def output(u, slot):
        for j in range(unit):
            bi, c, q0, ks = locate(u, j)
            vb = band(vc_ref, v_ref, bi, ks)
            o_all = jnp.dot(p_ref[slot, j * rows:(j + 1) * rows, :], vb, preferred_element_type=F32)
            o_all = o_all * l_ref[slot, j * rows:(j + 1) * rows, :]
            o = o_all[0:CHUNK]
            for h in range(1, heads):
                o = jnp.where(head_lanes[h], o_all[h * CHUNK:(h + 1) * CHUNK], o)
            o_ref[bi, pl.ds(q0, CHUNK), :] = o.astype(o_ref.dtype)

    def step(i, slot, do_scores, do_softmax, do_output):
        if do_scores:
            scores(i, slot)
        if do_output:
            output(i - 2, slot)
        if do_softmax:
            softmax(1 - slot)

    for i in range(2):
        step(i, i % 2, i < n_units, 1 <= i <= n_units, False)
    n_pairs = max(n_units - 2, 0) // 2
    if n_pairs:
        def body(t, carry):
            i = 2 + 2 * t
            step(i, 0, True, True, True)
            step(i + 1, 1, True, True, True)
            return carry
        lax.fori_loop(0, n_pairs, body, 0)
    for i in range(2 + 2 * n_pairs, n_units + 2):
        step(i, i % 2, i < n_units, 1 <= i <= n_units, i >= 2)


def _attention(q_arr, k_arr, v_arr, rel_bias, *, q_col0, k_col0, v_col0, bb, heads, offsets, name,
               k_cache=None, v_cache=None):
    b, sq, _ = q_arr.shape
    sk = k_arr.shape[1]
    kw = KV_REACH + CHUNK
    cached = k_cache is not None
    if cached:
        assert sq == CHUNK and sk == CHUNK and k_cache.shape[1] == KV_REACH
    else:
        assert sk >= kw and sq % CHUNK == 0
    hw = heads * HEAD_DIM
    n_groups = N_HEADS // heads
    n_case = len(offsets)
    width = -(-(kw + CHUNK - 1) // LANES) * LANES
    diag = _bias_diagonals(rel_bias, width, offsets, heads)
    qc, kc, vc = q_col0 // hw, k_col0 // hw, v_col0 // hw
    unit = 2
    assert (bb * sq // CHUNK) % unit == 0
    rows = unit * heads * CHUNK
    cache_specs = [pl.BlockSpec((bb, KV_REACH, hw), lambda g, i: (i, 0, g))] * 2 if cached else []
    cache_args = [k_cache, v_cache] if cached else []
    return pl.pallas_call(
        functools.partial(_attn_body, bb=bb, n_chunks=sq // CHUNK, kw=kw, heads=heads,
                          offsets=tuple(offsets), unit=unit, cached=cached),
        grid=(n_groups, b // bb),
        in_specs=[pl.BlockSpec((bb, sq, hw), lambda g, i: (i, 0, qc + g))] + cache_specs + [
            pl.BlockSpec((bb, sk, hw), lambda g, i: (i, 0, kc + g)),
            pl.BlockSpec((bb, sk, hw), lambda g, i: (i, 0, vc + g)),
            pl.BlockSpec((1, n_case * heads, width), lambda g, i: (g, 0, 0)),
        ],
        out_specs=pl.BlockSpec((bb, sq, hw), lambda g, i: (i, 0, g)),
        out_shape=jax.ShapeDtypeStruct((b, sq, D_ATTN), BF16),
        scratch_shapes=[
            pltpu.VMEM((n_case, heads * CHUNK, kw), F32),
            pltpu.VMEM((2, rows, kw), F32),
            pltpu.VMEM((2, rows, kw), BF16),
            pltpu.VMEM((2, rows, 1), F32),
        ],
        compiler_params=_params(2, 40),
        name=name,
    )(q_arr, *cache_args, k_arr, v_arr, diag)


def _log_sigmoid(x):
    return jnp.minimum(x, 0.0) - jnp.log1p(jnp.exp(-jnp.abs(x)))


def _shift_matrix(tt):
    t = jnp.arange(tt)
    blocks = [(t[:, None] - d) == t[None, :] for d in range(CONV_WIDTH - 1, 0, -1)]
    return jnp.concatenate(blocks, axis=0).astype(BF16)


def _rglru_head(x_bf, prev8, shift, conv_w, conv_b, wg_ref, b_rg, b_ig):
    tt, cw = x_bf.shape
    taps = CONV_WIDTH - 1
    x = x_bf.astype(F32)
    row = lax.broadcasted_iota(jnp.int32, (SUBLANES, cw), 0)
    if shift is not None:
        sh = jnp.dot(shift, x_bf, preferred_element_type=F32)
        xc = conv_b + conv_w[0:1] * sh[0:tt]
        for k in range(1, taps):
            xc = xc + conv_w[k:k + 1] * sh[k * tt:(k + 1) * tt]
        xc = xc + conv_w[taps:taps + 1] * x
        head = xc[0:SUBLANES]
        for k in range(taps):
            d = taps - k
            head = head + conv_w[k:k + 1] * jnp.where(row < d, pltpu.roll(prev8, d, 0), 0.0)
        xc = jnp.concatenate([head, xc[SUBLANES:]], axis=0)
    else:
        xc = conv_b
        for k in range(taps):
            d = taps - k
            sh = pltpu.roll(x, d, 0)
            head = jnp.where(row < d, pltpu.roll(prev8, d, 0), sh[0:SUBLANES])
            xc = xc + conv_w[k:k + 1] * jnp.concatenate([head, sh[SUBLANES:]], axis=0)
        xc = xc + conv_w[taps:taps + 1] * x

    xcb = xc.astype(BF16)
    r_parts, i_parts = [], []
    for n in range(cw // RNN_BLOCK):
        gts = jnp.dot(xcb[:, n * RNN_BLOCK:(n + 1) * RNN_BLOCK], wg_ref[n], preferred_element_type=F32)
        r_parts.append(gts[:, :RNN_BLOCK])
        i_parts.append(gts[:, RNN_BLOCK:])
    r_pre = jnp.concatenate(r_parts, axis=1) + b_rg
    i_pre = jnp.concatenate(i_parts, axis=1) + b_ig
    return xc, r_pre, i_pre, x[tt - SUBLANES:tt]


def _rglru_tail(xc, r_pre, i_pre, gr_bf, h_in, lam):
    rows, cw = xc.shape
    r = jax.nn.sigmoid(r_pre)
    i = jax.nn.sigmoid(i_pre)
    log_a = r * (LRU_C * _log_sigmoid(lam))
    a = jnp.exp(log_a)
    w = -jnp.tanh(log_a)
    one_minus_a2 = 2.0 * w / (1.0 + w)
    root = jnp.where(one_minus_a2 > 0.0, one_minus_a2 * lax.rsqrt(one_minus_a2), 0.0)
    u = root * (i * xc)

    row = lax.broadcasted_iota(jnp.int32, (SUBLANES, cw), 0)
    h = h_in
    hs = []
    for g in range(rows // SUBLANES):
        ag = a[g * SUBLANES:(g + 1) * SUBLANES]
        ug = u[g * SUBLANES:(g + 1) * SUBLANES]
        for d in (1, 2, 4):
            keep = row >= d
            a_prev = jnp.where(keep, pltpu.roll(ag, d, 0), 1.0)
            u_prev = jnp.where(keep, pltpu.roll(ug, d, 0), 0.0)
            ug = ag * u_prev + ug
            ag = ag * a_prev
        hg = ug + ag * h
        hs.append(hg)
        h = jnp.broadcast_to(hg[SUBLANES - 1:SUBLANES, :], (SUBLANES, cw))
    gate = jax.nn.gelu(gr_bf.astype(F32), approximate=True)
    return (jnp.concatenate(hs, axis=0) * gate).astype(BF16), h


def _rglru_body(xr_ref, gr_ref, c0_ref, h0_ref, sh_ref, cw_ref, cb_ref, wg_ref, brg_ref, big_ref, lam_ref,
                ob_ref, hl_ref, prev_ref, hcar_ref):
    @pl.when(pl.program_id(2) == 0)
    def _():
        hcar_ref[...] = jnp.broadcast_to(h0_ref[0], hcar_ref.shape)
        prev_ref[...] = c0_ref[0]

    xc, r_pre, i_pre, prev8 = _rglru_head(xr_ref[0], prev_ref[...], sh_ref[...], cw_ref[...], cb_ref[...],
                                          wg_ref, brg_ref[...], big_ref[...])
    out, h = _rglru_tail(xc, r_pre, i_pre, gr_ref[0], hcar_ref[...], lam_ref[...])
    ob_ref[0] = out
    prev_ref[...] = prev8
    hcar_ref[...] = h
    hl_ref[0] = h[0:1, :]


def _rglru(z3, conv0, h0, conv_w, conv_b, w_gates, b_rg, b_ig, lam, *, xr_col0, gr_col0, tt, cw, name):
    b, t, _ = z3.shape
    d_rnn = conv_w.shape[1]
    n_cg = d_rnn // cw
    blocks_per_cg = cw // RNN_BLOCK
    xc0, gc0 = xr_col0 // cw, gr_col0 // cw
    vec = pl.BlockSpec((1, cw), lambda c, i, s: (0, c))
    return pl.pallas_call(
        _rglru_body,
        grid=(n_cg, b, t // tt),
        in_specs=[
            pl.BlockSpec((1, tt, cw), lambda c, i, s: (i, s, xc0 + c)),
            pl.BlockSpec((1, tt, cw), lambda c, i, s: (i, s, gc0 + c)),
            pl.BlockSpec((1, SUBLANES, cw), lambda c, i, s: (i, 0, c)),
            pl.BlockSpec((1, 1, cw), lambda c, i, s: (i, 0, c)),
            pl.BlockSpec(((CONV_WIDTH - 1) * tt, tt), lambda c, i, s: (0, 0)),
            pl.BlockSpec((CONV_WIDTH, cw), lambda c, i, s: (0, c)),
            vec,
            pl.BlockSpec((blocks_per_cg, RNN_BLOCK, 2 * RNN_BLOCK), lambda c, i, s: (c, 0, 0)),
            vec, vec, vec,
        ],
        out_specs=[
            pl.BlockSpec((1, tt, cw), lambda c, i, s: (i, s, c)),
            pl.BlockSpec((1, 1, cw), lambda c, i, s: (i, 0, c)),
        ],
        out_shape=[
            jax.ShapeDtypeStruct((b, t, d_rnn), BF16),
            jax.ShapeDtypeStruct((b, 1, d_rnn), F32),
        ],
        scratch_shapes=[
            pltpu.VMEM((SUBLANES, cw), F32),
            pltpu.VMEM((SUBLANES, cw), F32),
        ],
        compiler_params=_params(3, 40),
        name=name,
    )(z3, z3, conv0, h0, _shift_matrix(tt), conv_w, conv_b, w_gates, b_rg, b_ig, lam)


def _lookup(table, j):
    out = jnp.int32(table[-1])
    for k in range(len(table) - 2, -1, -1):
        out = jnp.where(j == k, jnp.int32(table[k]), out)
    return out


def _fused_body(x_ref, g_ref, w_ref, c0_ref, h0_ref, cw_ref, cb_ref, wg_ref, brg_ref, big_ref,
                lam_ref, z_ref, s_ref, ob_ref, hl_ref, xn_ref, stash_ref, prev_ref, hcar_ref, hd_ref, gr_ref, *,
                n_tiles, period, n_cg, pieces_per_cg, tt, n_split, stash_slots):
    i, j = pl.program_id(0), pl.program_id(1)
    cw = ob_ref.shape[1]

    @pl.when((j == 0) & (i < n_tiles))
    def _():
        xn_ref[...] = _rmsnorm_rows(x_ref[...], g_ref[...]).astype(BF16)

    @pl.when((j == 0) & (i == 0))
    def _():
        prev_ref[...] = jnp.zeros_like(prev_ref)
        hcar_ref[...] = jnp.zeros_like(hcar_ref)

    tm = xn_ref.shape[0]
    rows_s = tt // n_split
    state_start = tm - s_ref.shape[0]
    first_rows = tm // 4
    rest_rows = (tm - first_rows) // n_split
    bounds = [0] + [first_rows + c * rest_rows for c in range(n_split + 1)]

    def project(c):
        lo, hi = bounds[c], bounds[c + 1]
        acc = jnp.dot(xn_ref[lo:hi, :], w_ref[...], preferred_element_type=F32)
        zb = acc.astype(BF16)
        z_ref[lo:hi, :] = zb
        if hi > state_start:
            keep = max(lo, state_start)
            s_ref[keep - state_start:hi - state_start, :] = acc[keep - lo:, :]
        stash_ref[_lookup(stash_slots, j), lo:hi, :] = zb
        return acc[hi - lo - SUBLANES:, :]

    def after(state, acc_rows):
        cg, lam, h = state
        zero = (pltpu.bitcast(acc_rows, jnp.uint32) >> 16) >> 16
        return cg, lam, pltpu.bitcast(pltpu.bitcast(h, jnp.uint32) + zero, F32)

    def recur_load():
        cg, tp = j // pieces_per_cg, j % pieces_per_cg
        r0 = pl.multiple_of(tp * tt, tt)
        first = ((i - 1) % period == 0) & (tp == 0)
        prev8 = jnp.where(first, c0_ref[0, cg], prev_ref[cg])
        h = jnp.where(first, jnp.broadcast_to(h0_ref[0, cg], (SUBLANES, cw)), hcar_ref[cg])
        gr_ref[...] = stash_ref[n_cg + cg, pl.ds(r0, tt), :]
        return cg, stash_ref[cg, pl.ds(r0, tt), :], prev8, h

    def recur_head(loaded):
        cg, x_bf, prev8, h = loaded
        xc, r_pre, i_pre, prev8 = _rglru_head(x_bf, prev8, None, cw_ref[cg], cb_ref[cg], wg_ref.at[cg],
                                              brg_ref[cg], big_ref[cg])
        hd_ref[0] = xc
        hd_ref[1] = r_pre
        hd_ref[2] = i_pre
        prev_ref[cg] = prev8
        return cg, lam_ref[cg], h

    def recur_tail(k, state):
        cg, lam, h = state
        lo = k * rows_s
        out, h = _rglru_tail(hd_ref[0, lo:lo + rows_s, :], hd_ref[1, lo:lo + rows_s, :],
                             hd_ref[2, lo:lo + rows_s, :], gr_ref[lo:lo + rows_s, :], h, lam)
        ob_ref[lo:lo + rows_s, :] = out
        return cg, lam, h

    def recur_end(state):
        cg, _, h = state
        hcar_ref[cg] = h
        hl_ref[0, 0] = h[0:1, :]

    has_proj = i < n_tiles
    has_rec = (i >= 1) & (j < n_cg * pieces_per_cg)

    @pl.when(has_proj & has_rec)
    def _():
        state = after(recur_head(recur_load()), project(0))
        for k in range(n_split):
            state = recur_tail(k, state)
            acc_rows = project(k + 1)
            if k + 1 < n_split:
                state = after(state, acc_rows)
        recur_end(state)

    @pl.when(has_proj & jnp.logical_not(has_rec))
    def _():
        for c in range(n_split + 1):
            project(c)

    @pl.when(jnp.logical_not(has_proj) & has_rec)
    def _():
        state = recur_head(recur_load())
        for k in range(n_split):
            state = recur_tail(k, state)
        recur_end(state)


def _in_proj_rglru(x2d, g, w, conv0, h0, conv_w, conv_b, w_gates, b_rg, b_ig, lam, *, tm, tn, tt, period,
                   state_rows, state_col0, state_cols, xr_col0, gr_col0, name):
    m, d = x2d.shape
    n = w.shape[1]
    d_rnn = conv_w.shape[1]
    n_tiles, n_col, n_cg = m // tm, n // tn, d_rnn // tn
    n_seq = n_tiles // period
    pieces_per_cg = tm // tt
    n_pieces = n_cg * pieces_per_cg
    xr0, gr0 = xr_col0 // tn, gr_col0 // tn
    assert n_pieces + n_cg <= n_col and tm % tt == 0 and n_tiles % period == 0

    order, stash_slots = [None] * n_col, [2 * n_cg] * n_col
    for c in range(n_cg):
        jx = (c + 1) * pieces_per_cg
        order[jx], stash_slots[jx] = xr0 + c, c
        order[jx + 1], stash_slots[jx + 1] = gr0 + c, n_cg + c
    rest = [c for c in range(n_col) if c not in order]
    for jj in range(n_col):
        if order[jj] is None:
            order[jj] = rest.pop(0)
    s0, n_state = state_col0 // tn, state_cols // tn
    state_pos = [order.index(s0 + k) for k in range(n_state)]
    assert state_pos == sorted(state_pos)
    state_blk = [sum(p < jj for p in state_pos) for jj in range(n_col)]

    last = lambda i: i == n_tiles
    row = lambda i: jnp.minimum(i, n_tiles - 1)
    col = lambda i, j: jnp.where(last(i), order[-1], _lookup(order, j))
    seq_prev = lambda i: jnp.maximum(i - 1, 0) // period
    piece = lambda i, j: jnp.where(i == 0, 0, jnp.minimum(j, n_pieces - 1))

    def state_block(i, j):
        live = i % period == period - 1
        blk = jnp.where(live, _lookup(state_blk, j), n_state + 1)
        return row(i) // period, jnp.where(last(i), n_state, blk)

    per_cg = lambda a: a.reshape(a.shape[0], n_cg, tn).swapaxes(0, 1)
    whole = lambda a: pl.BlockSpec(a.shape, lambda i, j: (0,) * a.ndim)
    n_split = 4
    assert tt % (n_split * SUBLANES) == 0 and (tm - tm // 4) % (2 * SUBLANES * n_split) == 0
    params = [per_cg(conv_w), per_cg(conv_b), w_gates.reshape(n_cg, -1, RNN_BLOCK, 2 * RNN_BLOCK),
              per_cg(b_rg), per_cg(b_ig), per_cg(lam)]
    conv0 = conv0.reshape(n_seq, SUBLANES, n_cg, tn).swapaxes(1, 2)
    h0 = h0.reshape(n_seq, 1, n_cg, tn).swapaxes(1, 2)

    z, state, o_b, h_last = pl.pallas_call(
        functools.partial(_fused_body, n_tiles=n_tiles, period=period, n_cg=n_cg,
                          pieces_per_cg=pieces_per_cg, tt=tt, n_split=n_split,
                          stash_slots=tuple(stash_slots)),
        grid=(n_tiles + 1, n_col),
        in_specs=[
            pl.BlockSpec((tm, d), lambda i, j: (row(i), 0)),
            pl.BlockSpec((1, d), lambda i, j: (0, 0)),
            pl.BlockSpec((d, tn), lambda i, j: (0, col(i, j))),
            pl.BlockSpec((1, n_cg, SUBLANES, tn), lambda i, j: (seq_prev(i), 0, 0, 0)),
            pl.BlockSpec((1, n_cg, 1, tn), lambda i, j: (seq_prev(i), 0, 0, 0)),
        ] + [whole(p) for p in params],
        out_specs=[
            pl.BlockSpec((tm, tn), lambda i, j: (row(i), col(i, j))),
            pl.BlockSpec((state_rows, tn), state_block),
            pl.BlockSpec((tt, tn), lambda i, j: (jnp.maximum(i - 1, 0) * pieces_per_cg
                                                 + piece(i, j) % pieces_per_cg, piece(i, j) // pieces_per_cg)),
            pl.BlockSpec((1, 1, 1, tn), lambda i, j: (jnp.maximum(i - 1, 0), piece(i, j) // pieces_per_cg, 0, 0)),
        ],
        out_shape=[
            jax.ShapeDtypeStruct((m, n), BF16),
            jax.ShapeDtypeStruct((n_seq * state_rows, (n_state + 2) * tn), F32),
            jax.ShapeDtypeStruct((m, d_rnn), BF16),
            jax.ShapeDtypeStruct((n_tiles, n_cg, 1, tn), F32),
        ],
        scratch_shapes=[
            pltpu.VMEM((tm, d), BF16),
            pltpu.VMEM((2 * n_cg + 1, tm, tn), BF16),
            pltpu.VMEM((n_cg, SUBLANES, tn), F32),
            pltpu.VMEM((n_cg, SUBLANES, tn), F32),
            pltpu.VMEM((3, tt, tn), F32),
            pltpu.VMEM((tt, tn), BF16),
        ],
        compiler_params=_params(2, 58),
        name=name,
    )(x2d, g, w, conv0, h0, *params)
    return z, state, o_b, h_last.reshape(n_tiles, d_rnn)[period - 1::period]


def _merge_body(oa_ref, ob_ref, ga_ref, gb_ref, wa_ref, wb_ref, o_ref):
    ya = jnp.dot(oa_ref[...], wa_ref[...], preferred_element_type=F32)
    yb = jnp.dot(ob_ref[...], wb_ref[...], preferred_element_type=F32)
    ga = jax.nn.sigmoid(ga_ref[...].astype(F32))
    gb = jax.nn.sigmoid(gb_ref[...].astype(F32))
    o_ref[...] = (ga * ya + gb * yb).astype(o_ref.dtype)


def _gate_merge(o_a, o_b, z2, w_a, w_b, *, ga_col0, gb_col0, tm, tn, name):
    m = o_a.shape[0]
    d = w_a.shape[1]
    ga0, gb0 = ga_col0 // tn, gb_col0 // tn
    return pl.pallas_call(
        _merge_body,
        grid=(m // tm, d // tn),
        in_specs=[
            pl.BlockSpec((tm, o_a.shape[1]), lambda i, j: (i, 0)),
            pl.BlockSpec((tm, o_b.shape[1]), lambda i, j: (i, 0)),
            pl.BlockSpec((tm, tn), lambda i, j: (i, ga0 + j)),
            pl.BlockSpec((tm, tn), lambda i, j: (i, gb0 + j)),
            pl.BlockSpec((w_a.shape[0], tn), lambda i, j: (0, j)),
            pl.BlockSpec((w_b.shape[0], tn), lambda i, j: (0, j)),
        ],
        out_specs=pl.BlockSpec((tm, tn), lambda i, j: (i, j)),
        out_shape=jax.ShapeDtypeStruct((m, d), BF16),
        compiler_params=_params(2, 48),
        name=name,
    )(o_a, o_b, z2, z2, w_a, w_b)


def _out_norm_body(m_ref, w_ref, x_ref, g_ref, o_ref):
    y = jnp.dot(m_ref[...], w_ref[...], preferred_element_type=F32)
    o_ref[...] = x_ref[...] + _rmsnorm_rows(y, g_ref[...])


def _out_norm(merged, w_out, x2d, g, *, tm, name):
    m, d = x2d.shape
    return pl.pallas_call(
        _out_norm_body,
        grid=(m // tm,),
        in_specs=[
            pl.BlockSpec((tm, d), lambda i: (i, 0)),
            pl.BlockSpec((d, d), lambda i: (0, 0)),
            pl.BlockSpec((tm, d), lambda i: (i, 0)),
            pl.BlockSpec((1, d), lambda i: (0, 0)),
        ],
        out_specs=pl.BlockSpec((tm, d), lambda i: (i, 0)),
        out_shape=jax.ShapeDtypeStruct((m, d), F32),
        compiler_params=_params(1, 48),
        name=name,
    )(merged, w_out, x2d, g)


def _ffn_body(x_ref, g1_ref, w1_ref, w2_ref, g2_ref, o_ref, xn_ref):
    f = pl.program_id(1)

    @pl.when(f == 0)
    def _():
        xn_ref[...] = _rmsnorm_rows(x_ref[...], g1_ref[...]).astype(BF16)
        o_ref[...] = jnp.zeros_like(o_ref)

    hid = jnp.dot(xn_ref[...], w1_ref[...], preferred_element_type=F32)
    hid = jnp.square(jnp.maximum(hid, 0.0)).astype(BF16)
    o_ref[...] += jnp.dot(hid, w2_ref[...], preferred_element_type=F32)

    @pl.when(f == pl.num_programs(1) - 1)
    def _():
        o_ref[...] = x_ref[...] + _rmsnorm_rows(o_ref[...], g2_ref[...])


def _ffn(x2d, g1, w1, w2, g2, *, tm, tf, name):
    m, d = x2d.shape
    d_ff = w1.shape[1]
    return pl.pallas_call(
        _ffn_body,
        grid=(m // tm, d_ff // tf),
        in_specs=[
            pl.BlockSpec((tm, d), lambda i, f: (i, 0)),
            pl.BlockSpec((1, d), lambda i, f: (0, 0)),
            pl.BlockSpec((d, tf), lambda i, f: (0, f)),
            pl.BlockSpec((tf, d), lambda i, f: (f, 0)),
            pl.BlockSpec((1, d), lambda i, f: (0, 0)),
        ],
        out_specs=pl.BlockSpec((tm, d), lambda i, f: (i, 0)),
        out_shape=jax.ShapeDtypeStruct((m, d), F32),
        scratch_shapes=[pltpu.VMEM((tm, d), BF16)],
        compiler_params=_params(2, 48),
        name=name,
    )(x2d, g1, w1, w2, g2)


def _pick_tile(n, target):
    t = min(n, target)
    while n % t:
        t //= 2
    return t


def _layer(x, cache_k, cache_v, conv_state, h_state, wts, tag):
    b, t, d = x.shape
    m = b * t
    x2d = x.reshape(m, d)
    d_rnn = wts["conv_w"].shape[1]
    n_in = wts["w_in"].shape[1]
    col_k, col_v, col_xr = D_ATTN, 2 * D_ATTN, 3 * D_ATTN
    col_gr = col_xr + d_rnn
    col_ga = col_gr + d_rnn
    col_gb = col_ga + d

    tm = _pick_tile(m, 1024)
    tn = 1024
    keep = min(KV_REACH, t)
    if keep == t:
        period, state_rows = 1, tm
    else:
        assert t % tm == 0 and keep <= tm
        period, state_rows = t // tm, keep
    rnn_w = (wts["conv_w"], wts["conv_b"], wts["w_gates"], wts["b_rg"], wts["b_ig"], wts["lru_lambda"])
    heads = 2
    if cache_k is None:
        conv0 = jnp.zeros((b, SUBLANES, d_rnn), F32)
        h0 = jnp.zeros((b, 1, d_rnn), F32)
        z2, state, o_b, h_last = _in_proj_rglru(
            x2d, wts["pre_mix_g"], wts["w_in"], conv0, h0, *rnn_w, tm=tm, tn=tn, tt=_pick_tile(tm, 256),
            period=period, state_rows=state_rows, state_col0=col_k, state_cols=col_gr - col_k,
            xr_col0=col_xr, gr_col0=col_gr, name=f"in_proj_rglru_{tag}")
        z3 = z2.reshape(b, t, n_in)
        offsets = [i * CHUNK for i in range(PAST_CHUNKS + 1)]
        o_a = _attention(z3, z3, z3, wts["rel_bias"], q_col0=0, k_col0=col_k, v_col0=col_v, bb=1,
                         heads=heads, offsets=offsets, name=f"attn_{tag}")
    else:
        z2, state = _in_proj(x2d, wts["pre_mix_g"], wts["w_in"], tm=tm, tn=tn, period=period,
                             state_rows=state_rows, state_col0=col_k, state_cols=col_gr - col_k,
                             name=f"in_proj_{tag}")
        z3 = z2.reshape(b, t, n_in)
        n_cached = cache_k.shape[1]
        assert t == CHUNK and n_cached == KV_REACH
        o_a = _attention(z3, z3, z3, wts["rel_bias"], q_col0=0, k_col0=col_k, v_col0=col_v,
                         bb=_pick_tile(b, 8), heads=heads, offsets=[n_cached], name=f"attn_{tag}",
                         k_cache=cache_k.astype(BF16).reshape(b, n_cached, D_ATTN),
                         v_cache=cache_v.astype(BF16).reshape(b, n_cached, D_ATTN))
        conv0 = jnp.pad(conv_state.astype(F32), ((0, 0), (SUBLANES - (CONV_WIDTH - 1), 0), (0, 0)))
        h0 = h_state.astype(F32).reshape(b, 1, d_rnn)
        o_b, h_last = _rglru(z3, conv0, h0, *rnn_w, xr_col0=col_xr, gr_col0=col_gr,
                             tt=_pick_tile(t, 256), cw=1024, name=f"rglru_{tag}")
    state = state.reshape(b, keep, -1)
    new_k = state[:, :, :D_ATTN].reshape(b, keep, N_HEADS, HEAD_DIM)
    new_v = state[:, :, D_ATTN:2 * D_ATTN].reshape(b, keep, N_HEADS, HEAD_DIM)
    conv_tail = state[:, keep - (CONV_WIDTH - 1):, 2 * D_ATTN:col_gr - col_k]

    merged = _gate_merge(o_a.reshape(m, D_ATTN), o_b.reshape(m, d_rnn), z2, wts["w_attn_up"],
                         wts["w_rnn_up"], ga_col0=col_ga, gb_col0=col_gb, tm=tm, tn=tn,
                         name=f"gate_merge_{tag}")
    tm2 = _pick_tile(m, 512)
    x1 = _out_norm(merged, wts["w_out"], x2d, wts["post_mix_g"], tm=tm2, name=f"out_norm_{tag}")
    y = _ffn(x1, wts["pre_ffn_g"], wts["w_ff1"], wts["w_ff2"], wts["post_ffn_g"], tm=tm2, tf=1024,
             name=f"ffn_{tag}")
    return y.reshape(b, t, d), new_k, new_v, conv_tail, h_last.reshape(b, d_rnn)


def kernel(x_prompt, x_sample, cache_k, cache_v, state_conv, state_h, pre_mix_g, w_in, rel_bias, conv_w, conv_b, w_rg, b_rg, w_ig, b_ig, lru_lambda, w_attn_up, w_rnn_up, w_out, post_mix_g, pre_ffn_g, w_ff1, w_ff2, post_ffn_g):
    depth = w_in.shape[0]
    y_p, y_s = x_prompt, x_sample
    outs_p, outs_s = [], []
    for l in range(depth):
        wts = {
            "pre_mix_g": pre_mix_g[l][None], "post_mix_g": post_mix_g[l][None],
            "pre_ffn_g": pre_ffn_g[l][None], "post_ffn_g": post_ffn_g[l][None],
            "w_in": w_in[l].astype(BF16), "rel_bias": rel_bias[l],
            "conv_w": conv_w[l], "conv_b": conv_b[l][None],
            "w_gates": jnp.concatenate([w_rg[l], w_ig[l]], axis=-1).astype(BF16),
            "b_rg": b_rg[l][None], "b_ig": b_ig[l][None], "lru_lambda": lru_lambda[l][None],
            "w_attn_up": w_attn_up[l].astype(BF16), "w_rnn_up": w_rnn_up[l].astype(BF16),
            "w_out": w_out[l].astype(BF16), "w_ff1": w_ff1[l].astype(BF16), "w_ff2": w_ff2[l].astype(BF16),
        }
        y_p, *st_p = _layer(y_p, None, None, None, None, wts, f"p{l}")
        y_s, *st_s = _layer(y_s, cache_k[l], cache_v[l], state_conv[l], state_h[l], wts, f"s{l}")
        outs_p.append(st_p)
        outs_s.append(st_s)
    stack = lambda outs, i: jnp.stack([o[i] for o in outs])
    return (y_p, y_s,
            stack(outs_p, 0), stack(outs_p, 1), stack(outs_p, 2), stack(outs_p, 3),
            stack(outs_s, 0), stack(outs_s, 1), stack(outs_s, 2), stack(outs_s, 3))
```

```python
import functools

import jax
import jax.numpy as jnp
from jax import lax
from jax.experimental import pallas as pl
from jax.experimental.pallas import tpu as pltpu

F32 = jnp.float32
BF16 = jnp.bfloat16

CHUNK = 64
PAST_CHUNKS = 8
KV_REACH = PAST_CHUNKS * CHUNK
N_HEADS = 16
HEAD_DIM = 64
D_ATTN = N_HEADS * HEAD_DIM
MAX_REL = 256
RNN_BLOCK = 128
CONV_WIDTH = 4
LRU_C = 8.0
EPS = 1e-6
NEG_INF = -1e30

LANES = 128
KT_BLOCK = 512
SUBLANES = 8
MIB = 1024 * 1024


def _params(n_axes, vmem_mib):
    return pltpu.CompilerParams(
        dimension_semantics=("arbitrary",) * n_axes,
        vmem_limit_bytes=vmem_mib * MIB,
    )


def _rmsnorm_rows(x, g):
    ms = jnp.mean(x * x, axis=-1, keepdims=True)
    return x * lax.rsqrt(ms + EPS) * g


def _in_proj_body(x_ref, g_ref, w_ref, o_ref, s_ref, xn_ref, *, period, j0, j1):
    i, j = pl.program_id(0), pl.program_id(1)

    @pl.when(j == 0)
    def _():
        xn_ref[...] = _rmsnorm_rows(x_ref[...], g_ref[...]).astype(BF16)

    keep_f32 = (i % period == period - 1) & (j >= j0) & (j < j1)

    @pl.when(keep_f32)
    def _():
        acc = jnp.dot(xn_ref[...], w_ref[...], preferred_element_type=F32)
        o_ref[...] = acc.astype(o_ref.dtype)
        s_ref[...] = acc[acc.shape[0] - s_ref.shape[0]:, :]

    @pl.when(jnp.logical_not(keep_f32))
    def _():
        o_ref[...] = jnp.dot(xn_ref[...], w_ref[...], preferred_element_type=F32).astype(o_ref.dtype)


def _in_proj(x2d, g, w, *, tm, tn, period, state_rows, state_col0, state_cols, name):
    m, d = x2d.shape
    n = w.shape[1]
    j0, j1 = state_col0 // tn, (state_col0 + state_cols) // tn
    n_state_tiles = m // tm // period

    def state_block(i, j):
        live = i % period == period - 1
        return i // period, jnp.where(live, jnp.clip(j - j0, 0, j1 - j0 - 1), 0)

    return pl.pallas_call(
        functools.partial(_in_proj_body, period=period, j0=j0, j1=j1),
        grid=(m // tm, n // tn),
        in_specs=[
            pl.BlockSpec((tm, d), lambda i, j: (i, 0)),
            pl.BlockSpec((1, d), lambda i, j: (0, 0)),
            pl.BlockSpec((d, tn), lambda i, j: (0, j)),
        ],
        out_specs=[
            pl.BlockSpec((tm, tn), lambda i, j: (i, j)),
            pl.BlockSpec((state_rows, tn), state_block),
        ],
        out_shape=[
            jax.ShapeDtypeStruct((m, n), BF16),
            jax.ShapeDtypeStruct((n_state_tiles * state_rows, state_cols), F32),
        ],
        scratch_shapes=[pltpu.VMEM((tm, d), BF16)],
        compiler_params=_params(2, 52),
        name=name,
    )(x2d, g, w)


def _bias_diagonals(rel_bias, width, offsets, heads):
    x = jnp.arange(width)
    rows = [rel_bias[:, jnp.clip(off + CHUNK - 1 - x, -MAX_REL, MAX_REL) + MAX_REL] for off in offsets]
    g = jnp.stack(rows, axis=1).astype(F32)
    n_groups = rel_bias.shape[0] // heads
    g = g.reshape(n_groups, heads, len(offsets), width)
    return jnp.swapaxes(g, 1, 2).reshape(n_groups, len(offsets) * heads, width)


def _attn_body(*refs, bb, n_chunks, kw, heads, offsets, unit, cached):
    if cached:
        q_ref, kc_ref, vc_ref, k_ref, v_ref, g_ref, o_ref, tab_ref, s_ref, p_ref, l_ref = refs
        band = lambda c_ref, n_ref, bi, ks: jnp.concatenate([c_ref[bi], n_ref[bi]], axis=0)
    else:
        q_ref, k_ref, v_ref, g_ref, o_ref, tab_ref, s_ref, p_ref, l_ref, kt_ref = refs
        kc_ref = vc_ref = None
        band = lambda c_ref, n_ref, bi, ks: n_ref[bi, pl.ds(ks, kw), :]
    width = g_ref.shape[2]
    hw = heads * HEAD_DIM

    @pl.when(pl.program_id(1) == 0)
    def _():
        r = lax.broadcasted_iota(jnp.int32, (CHUNK, kw), 0)
        j = lax.broadcasted_iota(jnp.int32, (CHUNK, kw), 1)
        for c, off in enumerate(offsets):
            cq = (r + off) // CHUNK
            ck = j // CHUNK
            valid = (ck <= cq) & (ck >= cq - PAST_CHUNKS)
            for h in range(heads):
                g = jnp.broadcast_to(g_ref[0, c * heads + h:c * heads + h + 1, :], (CHUNK, width))
                t = pltpu.roll(g, width - (CHUNK - 1), 1, stride=1, stride_axis=0)
                tab_ref[c, h * CHUNK:(h + 1) * CHUNK, :] = jnp.where(valid, t[:, :kw], NEG_INF)

    if not cached:
        for r0 in range(0, k_ref.shape[1], KT_BLOCK):
            kt_ref[:, r0:r0 + KT_BLOCK] = k_ref[0, r0:r0 + KT_BLOCK, :].T

    lane = lax.broadcasted_iota(jnp.int32, (1, hw), 1)
    head_lanes = [(lane >= h * HEAD_DIM) & (lane < (h + 1) * HEAD_DIM) for h in range(heads)]
    rows = heads * CHUNK
    n_units = bb * n_chunks // unit

    def locate(u, j):
        idx = u * unit + j
        bi, c = (0, idx) if bb == 1 else (idx // n_chunks, idx % n_chunks)
        q0 = pl.multiple_of(c * CHUNK, CHUNK)
        if cached:
            return bi, 0, q0, 0
        past = jnp.maximum(c - PAST_CHUNKS, 0)
        odd = past % 2
        ks = pl.multiple_of((past - odd) * CHUNK, 2 * CHUNK)
        return bi, jnp.minimum(c, PAST_CHUNKS) + odd, q0, ks

    def scores(u, slot):
        for j in range(unit):
            bi, c, q0, ks = locate(u, j)
            q = q_ref[bi, pl.ds(q0, CHUNK), :]
            qs = jnp.concatenate([jnp.where(m, q, jnp.zeros_like(q)) for m in head_lanes], axis=0)
            qs = qs * jnp.asarray(HEAD_DIM ** -0.5, q.dtype)
            if cached:
                kb = band(kc_ref, k_ref, bi, ks)
                s = lax.dot_general(qs, kb, (((1,), (1,)), ((), ())), preferred_element_type=F32)
            else:
                s = jnp.dot(qs, kt_ref[:, pl.ds(ks, kw)], preferred_element_type=F32)
            s_ref[slot, j * rows:(j + 1) * rows, :] = s + tab_ref[c]

    def softmax(slot):
        s = s_ref[slot]
        p = jnp.exp(s - jnp.max(s, axis=-1, keepdims=True))
        p_ref[slot] = p.astype(p_ref.dtype)
        l_ref[slot] = 1.0 / jnp.sum(p, axis=-1, keepdims=True)

    def output(u, slot):
        for j in range(unit):
            bi, c, q0, ks = locate(u, j)
            vb = band(vc_ref, v_ref, bi, ks)
            o_all = jnp.dot(p_ref[slot, j * rows:(j + 1) * rows, :], vb, preferred_element_type=F32)
            o_all = o_all * l_ref[slot, j * rows:(j + 1) * rows, :]
            o = o_all[0:CHUNK]
            for h in range(1, heads):
                o = jnp.where(head_lanes[h], o_all[h * CHUNK:(h + 1) * CHUNK], o)
            o_ref[bi, pl.ds(q0, CHUNK), :] = o.astype(o_ref.dtype)

    def step(i, slot, do_scores, do_softmax, do_output):
        if do_scores:
            scores(i, slot)
        if do_output:
            output(i - 2, slot)
        if do_softmax:
            softmax(1 - slot)

    for i in range(2):
        step(i, i % 2, i < n_units, 1 <= i <= n_units, False)
    n_pairs = max(n_units - 2, 0) // 2
    if n_pairs:
        def body(t, carry):
            i = 2 + 2 * t
            step(i, 0, True, True, True)
            step(i + 1, 1, True, True, True)
            return carry
        lax.fori_loop(0, n_pairs, body, 0)
    for i in range(2 + 2 * n_pairs, n_units + 2):
        step(i, i % 2, i < n_units, 1 <= i <= n_units, i >= 2)


def _attention(q_arr, k_arr, v_arr, rel_bias, *, q_col0, k_col0, v_col0, bb, heads, offsets, name,
               k_cache=None, v_cache=None):
    b, sq, _ = q_arr.shape
    sk = k_arr.shape[1]
    cached = k_cache is not None
    if cached:
        kw = KV_REACH + CHUNK
        assert sq == CHUNK and sk == CHUNK and k_cache.shape[1] == KV_REACH and len(offsets) == 1
    else:
        kw = KV_REACH + 2 * CHUNK
        assert bb == 1 and sq == sk and sk % KT_BLOCK == 0 and sk >= kw and len(offsets) == PAST_CHUNKS + 2
    hw = heads * HEAD_DIM
    n_groups = N_HEADS // heads
    n_case = len(offsets)
    width = -(-(kw + CHUNK - 1) // LANES) * LANES
    diag = _bias_diagonals(rel_bias, width, offsets, heads)
    qc, kc, vc = q_col0 // hw, k_col0 // hw, v_col0 // hw
    unit = 2
    assert (bb * sq // CHUNK) % unit == 0
    rows = unit * heads * CHUNK
    cache_specs = [pl.BlockSpec((bb, KV_REACH, hw), lambda g, i: (i, 0, g))] * 2 if cached else []
    cache_args = [k_cache, v_cache] if cached else []
    return pl.pallas_call(
        functools.partial(_attn_body, bb=bb, n_chunks=sq // CHUNK, kw=kw, heads=heads,
                          offsets=tuple(offsets), unit=unit, cached=cached),
        grid=(n_groups, b // bb),
        in_specs=[pl.BlockSpec((bb, sq, hw), lambda g, i: (i, 0, qc + g))] + cache_specs + [
            pl.BlockSpec((bb, sk, hw), lambda g, i: (i, 0, kc + g)),
            pl.BlockSpec((bb, sk, hw), lambda g, i: (i, 0, vc + g)),
            pl.BlockSpec((1, n_case * heads, width), lambda g, i: (g, 0, 0)),
        ],
        out_specs=pl.BlockSpec((bb, sq, hw), lambda g, i: (i, 0, g)),
        out_shape=jax.ShapeDtypeStruct((b, sq, D_ATTN), BF16),
        scratch_shapes=[
            pltpu.VMEM((n_case, heads * CHUNK, kw), F32),
            pltpu.VMEM((2, rows, kw), F32),
            pltpu.VMEM((2, rows, kw), BF16),
            pltpu.VMEM((2, rows, 1), F32),
        ] + ([] if cached else [pltpu.VMEM((hw, sk), BF16)]),
        compiler_params=_params(2, 40),
        name=name,
    )(q_arr, *cache_args, k_arr, v_arr, diag)


def _log_sigmoid(x):
    return jnp.minimum(x, 0.0) - jnp.log1p(jnp.exp(-jnp.abs(x)))


def _shift_matrix(tt):
    t = jnp.arange(tt)
    blocks = [(t[:, None] - d) == t[None, :] for d in range(CONV_WIDTH - 1, 0, -1)]
    return jnp.concatenate(blocks, axis=0).astype(BF16)


def _rglru_head(x_bf, prev8, shift, conv_w, conv_b, wg_ref, b_rg, b_ig):
    tt, cw = x_bf.shape
    taps = CONV_WIDTH - 1
    x = x_bf.astype(F32)
    row = lax.broadcasted_iota(jnp.int32, (SUBLANES, cw), 0)
    if shift is not None:
        sh = jnp.dot(shift, x_bf, preferred_element_type=F32)
        xc = conv_b + conv_w[0:1] * sh[0:tt]
        for k in range(1, taps):
            xc = xc + conv_w[k:k + 1] * sh[k * tt:(k + 1) * tt]
        xc = xc + conv_w[taps:taps + 1] * x
        head = xc[0:SUBLANES]
        for k in range(taps):
            d = taps - k
            head = head + conv_w[k:k + 1] * jnp.where(row < d, pltpu.roll(prev8, d, 0), 0.0)
        xc = jnp.concatenate([head, xc[SUBLANES:]], axis=0)
    else:
        xc = conv_b
        for k in range(taps):
            d = taps - k
            sh = pltpu.roll(x, d, 0)
            head = jnp.where(row < d, pltpu.roll(prev8, d, 0), sh[0:SUBLANES])
            xc = xc + conv_w[k:k + 1] * jnp.concatenate([head, sh[SUBLANES:]], axis=0)
        xc = xc + conv_w[taps:taps + 1] * x

    xcb = xc.astype(BF16)
    r_parts, i_parts = [], []
    for n in range(cw // RNN_BLOCK):
        gts = jnp.dot(xcb[:, n * RNN_BLOCK:(n + 1) * RNN_BLOCK], wg_ref[n], preferred_element_type=F32)
        r_parts.append(gts[:, :RNN_BLOCK])
        i_parts.append(gts[:, RNN_BLOCK:])
    r_pre = jnp.concatenate(r_parts, axis=1) + b_rg
    i_pre = jnp.concatenate(i_parts, axis=1) + b_ig
    return xc, r_pre, i_pre, x[tt - SUBLANES:tt]


def _rglru_tail(xc, r_pre, i_pre, gr_bf, h_in, lam):
    rows, cw = xc.shape
    r = jax.nn.sigmoid(r_pre)
    i = jax.nn.sigmoid(i_pre)
    log_a = r * (LRU_C * _log_sigmoid(lam))
    a = jnp.exp(log_a)
    w = -jnp.tanh(log_a)
    one_minus_a2 = 2.0 * w / (1.0 + w)
    root = jnp.where(one_minus_a2 > 0.0, one_minus_a2 * lax.rsqrt(one_minus_a2), 0.0)
    u = root * (i * xc)

    row = lax.broadcasted_iota(jnp.int32, (SUBLANES, cw), 0)
    h = h_in
    hs = []
    for g in range(rows // SUBLANES):
        ag = a[g * SUBLANES:(g + 1) * SUBLANES]
        ug = u[g * SUBLANES:(g + 1) * SUBLANES]
        for d in (1, 2, 4):
            keep = row >= d
            a_prev = jnp.where(keep, pltpu.roll(ag, d, 0), 1.0)
            u_prev = jnp.where(keep, pltpu.roll(ug, d, 0), 0.0)
            ug = ag * u_prev + ug
            ag = ag * a_prev
        hg = ug + ag * h
        hs.append(hg)
        h = jnp.broadcast_to(hg[SUBLANES - 1:SUBLANES, :], (SUBLANES, cw))
    gate = jax.nn.gelu(gr_bf.astype(F32), approximate=True)
    return (jnp.concatenate(hs, axis=0) * gate).astype(BF16), h


def _rglru_body(xr_ref, gr_ref, c0_ref, h0_ref, sh_ref, cw_ref, cb_ref, wg_ref, brg_ref, big_ref, lam_ref,
                ob_ref, hl_ref, prev_ref, hcar_ref):
    @pl.when(pl.program_id(2) == 0)
    def _():
        hcar_ref[...] = jnp.broadcast_to(h0_ref[0], hcar_ref.shape)
        prev_ref[...] = c0_ref[0]

    xc, r_pre, i_pre, prev8 = _rglru_head(xr_ref[0], prev_ref[...], sh_ref[...], cw_ref[...], cb_ref[...],
                                          wg_ref, brg_ref[...], big_ref[...])
    out, h = _rglru_tail(xc, r_pre, i_pre, gr_ref[0], hcar_ref[...], lam_ref[...])
    ob_ref[0] = out
    prev_ref[...] = prev8
    hcar_ref[...] = h
    hl_ref[0] = h[0:1, :]


def _rglru(z3, conv0, h0, conv_w, conv_b, w_gates, b_rg, b_ig, lam, *, xr_col0, gr_col0, tt, cw, name):
    b, t, _ = z3.shape
    d_rnn = conv_w.shape[1]
    n_cg = d_rnn // cw
    blocks_per_cg = cw // RNN_BLOCK
    xc0, gc0 = xr_col0 // cw, gr_col0 // cw
    vec = pl.BlockSpec((1, cw), lambda c, i, s: (0, c))
    return pl.pallas_call(
        _rglru_body,
        grid=(n_cg, b, t // tt),
        in_specs=[
            pl.BlockSpec((1, tt, cw), lambda c, i, s: (i, s, xc0 + c)),
            pl.BlockSpec((1, tt, cw), lambda c, i, s: (i, s, gc0 + c)),
            pl.BlockSpec((1, SUBLANES, cw), lambda c, i, s: (i, 0, c)),
            pl.BlockSpec((1, 1, cw), lambda c, i, s: (i, 0, c)),
            pl.BlockSpec(((CONV_WIDTH - 1) * tt, tt), lambda c, i, s: (0, 0)),
            pl.BlockSpec((CONV_WIDTH, cw), lambda c, i, s: (0, c)),
            vec,
            pl.BlockSpec((blocks_per_cg, RNN_BLOCK, 2 * RNN_BLOCK), lambda c, i, s: (c, 0, 0)),
            vec, vec, vec,
        ],
        out_specs=[
            pl.BlockSpec((1, tt, cw), lambda c, i, s: (i, s, c)),
            pl.BlockSpec((1, 1, cw), lambda c, i, s: (i, 0, c)),
        ],
        out_shape=[
            jax.ShapeDtypeStruct((b, t, d_rnn), BF16),
            jax.ShapeDtypeStruct((b, 1, d_rnn), F32),
        ],
        scratch_shapes=[
            pltpu.VMEM((SUBLANES, cw), F32),
            pltpu.VMEM((SUBLANES, cw), F32),
        ],
        compiler_params=_params(3, 40),
        name=name,
    )(z3, z3, conv0, h0, _shift_matrix(tt), conv_w, conv_b, w_gates, b_rg, b_ig, lam)


def _lookup(table, j):
    out = jnp.int32(table[-1])
    for k in range(len(table) - 2, -1, -1):
        out = jnp.where(j == k, jnp.int32(table[k]), out)
    return out


def _fused_body(x_ref, g_ref, w_ref, c0_ref, h0_ref, cw_ref, cb_ref, wg_ref, brg_ref, big_ref,
                lam_ref, z_ref, s_ref, ob_ref, hl_ref, xn_ref, stash_ref, prev_ref, hcar_ref, hd_ref, gr_ref, *,
                n_tiles, period, n_cg, pieces_per_cg, tt, n_split, stash_slots):
    i, j = pl.program_id(0), pl.program_id(1)
    cw = ob_ref.shape[1]

    @pl.when((j == 0) & (i < n_tiles))
    def _():
        xn_ref[...] = _rmsnorm_rows(x_ref[...], g_ref[...]).astype(BF16)

    @pl.when((j == 0) & (i == 0))
    def _():
        prev_ref[...] = jnp.zeros_like(prev_ref)
        hcar_ref[...] = jnp.zeros_like(hcar_ref)

    tm = xn_ref.shape[0]
    rows_s = tt // n_split
    state_start = tm - s_ref.shape[0]
    first_rows = tm // 4
    rest_rows = (tm - first_rows) // n_split
    bounds = [0] + [first_rows + c * rest_rows for c in range(n_split + 1)]

    def project(c):
        lo, hi = bounds[c], bounds[c + 1]
        acc = jnp.dot(xn_ref[lo:hi, :], w_ref[...], preferred_element_type=F32)
        zb = acc.astype(BF16)
        z_ref[lo:hi, :] = zb
        if hi > state_start:
            keep = max(lo, state_start)
            s_ref[keep - state_start:hi - state_start, :] = acc[keep - lo:, :]
        stash_ref[_lookup(stash_slots, j), lo:hi, :] = zb
        return acc[hi - lo - SUBLANES:, :]

    def after(state, acc_rows):
        cg, lam, h = state
        zero = (pltpu.bitcast(acc_rows, jnp.uint32) >> 16) >> 16
        return cg, lam, pltpu.bitcast(pltpu.bitcast(h, jnp.uint32) + zero, F32)

    def recur_load():
        cg, tp = j // pieces_per_cg, j % pieces_per_cg
        r0 = pl.multiple_of(tp * tt, tt)
        first = ((i - 1) % period == 0) & (tp == 0)
        prev8 = jnp.where(first, c0_ref[0, cg], prev_ref[cg])
        h = jnp.where(first, jnp.broadcast_to(h0_ref[0, cg], (SUBLANES, cw)), hcar_ref[cg])
        gr_ref[...] = stash_ref[n_cg + cg, pl.ds(r0, tt), :]
        return cg, stash_ref[cg, pl.ds(r0, tt), :], prev8, h

    def recur_head(loaded):
        cg, x_bf, prev8, h = loaded
        xc, r_pre, i_pre, prev8 = _rglru_head(x_bf, prev8, None, cw_ref[cg], cb_ref[cg], wg_ref.at[cg],
                                              brg_ref[cg], big_ref[cg])
        hd_ref[0] = xc
        hd_ref[1] = r_pre
        hd_ref[2] = i_pre
        prev_ref[cg] = prev8
        return cg, lam_ref[cg], h

    def recur_tail(k, state):
        cg, lam, h = state
        lo = k * rows_s
        out, h = _rglru_tail(hd_ref[0, lo:lo + rows_s, :], hd_ref[1, lo:lo + rows_s, :],
                             hd_ref[2, lo:lo + rows_s, :], gr_ref[lo:lo + rows_s, :], h, lam)
        ob_ref[lo:lo + rows_s, :] = out
        return cg, lam, h

    def recur_end(state):
        cg, _, h = state
        hcar_ref[cg] = h
        hl_ref[0, 0] = h[0:1, :]

    has_proj = i < n_tiles
    has_rec = (i >= 1) & (j < n_cg * pieces_per_cg)

    @pl.when(has_proj & has_rec)
    def _():
        state = after(recur_head(recur_load()), project(0))
        for k in range(n_split):
            state = recur_tail(k, state)
            acc_rows = project(k + 1)
            if k + 1 < n_split:
                state = after(state, acc_rows)
        recur_end(state)

    @pl.when(has_proj & jnp.logical_not(has_rec))
    def _():
        for c in range(n_split + 1):
            project(c)

    @pl.when(jnp.logical_not(has_proj) & has_rec)
    def _():
        state = recur_head(recur_load())
        for k in range(n_split):
            state = recur_tail(k, state)
        recur_end(state)


def _in_proj_rglru(x2d, g, w, conv0, h0, conv_w, conv_b, w_gates, b_rg, b_ig, lam, *, tm, tn, tt, period,
                   state_rows, state_col0, state_cols, xr_col0, gr_col0, name):
    m, d = x2d.shape
    n = w.shape[1]
    d_rnn = conv_w.shape[1]
    n_tiles, n_col, n_cg = m // tm, n // tn, d_rnn // tn
    n_seq = n_tiles // period
    pieces_per_cg = tm // tt
    n_pieces = n_cg * pieces_per_cg
    xr0, gr0 = xr_col0 // tn, gr_col0 // tn
    assert n_pieces + n_cg <= n_col and tm % tt == 0 and n_tiles % period == 0

    order, stash_slots = [None] * n_col, [2 * n_cg] * n_col
    for c in range(n_cg):
        jx = (c + 1) * pieces_per_cg
        order[jx], stash_slots[jx] = xr0 + c, c
        order[jx + 1], stash_slots[jx + 1] = gr0 + c, n_cg + c
    rest = [c for c in range(n_col) if c not in order]
    for jj in range(n_col):
        if order[jj] is None:
            order[jj] = rest.pop(0)
    s0, n_state = state_col0 // tn, state_cols // tn
    state_pos = [order.index(s0 + k) for k in range(n_state)]
    assert state_pos == sorted(state_pos)
    state_blk = [sum(p < jj for p in state_pos) for jj in range(n_col)]

    last = lambda i: i == n_tiles
    row = lambda i: jnp.minimum(i, n_tiles - 1)
    col = lambda i, j: jnp.where(last(i), order[-1], _lookup(order, j))
    seq_prev = lambda i: jnp.maximum(i - 1, 0) // period
    piece = lambda i, j: jnp.where(i == 0, 0, jnp.minimum(j, n_pieces - 1))

    def state_block(i, j):
        live = i % period == period - 1
        blk = jnp.where(live, _lookup(state_blk, j), n_state + 1)
        return row(i) // period, jnp.where(last(i), n_state, blk)

    per_cg = lambda a: a.reshape(a.shape[0], n_cg, tn).swapaxes(0, 1)
    whole = lambda a: pl.BlockSpec(a.shape, lambda i, j: (0,) * a.ndim)
    n_split = 4
    assert tt % (n_split * SUBLANES) == 0 and (tm - tm // 4) % (2 * SUBLANES * n_split) == 0
    params = [per_cg(conv_w), per_cg(conv_b), w_gates.reshape(n_cg, -1, RNN_BLOCK, 2 * RNN_BLOCK),
              per_cg(b_rg), per_cg(b_ig), per_cg(lam)]
    conv0 = conv0.reshape(n_seq, SUBLANES, n_cg, tn).swapaxes(1, 2)
    h0 = h0.reshape(n_seq, 1, n_cg, tn).swapaxes(1, 2)

    z, state, o_b, h_last = pl.pallas_call(
        functools.partial(_fused_body, n_tiles=n_tiles, period=period, n_cg=n_cg,
                          pieces_per_cg=pieces_per_cg, tt=tt, n_split=n_split,
                          stash_slots=tuple(stash_slots)),
        grid=(n_tiles + 1, n_col),
        in_specs=[
            pl.BlockSpec((tm, d), lambda i, j: (row(i), 0)),
            pl.BlockSpec((1, d), lambda i, j: (0, 0)),
            pl.BlockSpec((d, tn), lambda i, j: (0, col(i, j))),
            pl.BlockSpec((1, n_cg, SUBLANES, tn), lambda i, j: (seq_prev(i), 0, 0, 0)),
            pl.BlockSpec((1, n_cg, 1, tn), lambda i, j: (seq_prev(i), 0, 0, 0)),
        ] + [whole(p) for p in params],
        out_specs=[
            pl.BlockSpec((tm, tn), lambda i, j: (row(i), col(i, j))),
            pl.BlockSpec((state_rows, tn), state_block),
            pl.BlockSpec((tt, tn), lambda i, j: (jnp.maximum(i - 1, 0) * pieces_per_cg
                                                 + piece(i, j) % pieces_per_cg, piece(i, j) // pieces_per_cg)),
            pl.BlockSpec((1, 1, 1, tn), lambda i, j: (jnp.maximum(i - 1, 0), piece(i, j) // pieces_per_cg, 0, 0)),
        ],
        out_shape=[
            jax.ShapeDtypeStruct((m, n), BF16),
            jax.ShapeDtypeStruct((n_seq * state_rows, (n_state + 2) * tn), F32),
            jax.ShapeDtypeStruct((m, d_rnn), BF16),
            jax.ShapeDtypeStruct((n_tiles, n_cg, 1, tn), F32),
        ],
        scratch_shapes=[
            pltpu.VMEM((tm, d), BF16),
            pltpu.VMEM((2 * n_cg + 1, tm, tn), BF16),
            pltpu.VMEM((n_cg, SUBLANES, tn), F32),
            pltpu.VMEM((n_cg, SUBLANES, tn), F32),
            pltpu.VMEM((3, tt, tn), F32),
            pltpu.VMEM((tt, tn), BF16),
        ],
        compiler_params=_params(2, 58),
        name=name,
    )(x2d, g, w, conv0, h0, *params)
    return z, state, o_b, h_last.reshape(n_tiles, d_rnn)[period - 1::period]


def _merge_body(oa_ref, ob_ref, ga_ref, gb_ref, wa_ref, wb_ref, o_ref):
    ya = jnp.dot(oa_ref[...], wa_ref[...], preferred_element_type=F32)
    yb = jnp.dot(ob_ref[...], wb_ref[...], preferred_element_type=F32)
    ga = jax.nn.sigmoid(ga_ref[...].astype(F32))
    gb = jax.nn.sigmoid(gb_ref[...].astype(F32))
    o_ref[...] = (ga * ya + gb * yb).astype(o_ref.dtype)


def _gate_merge(o_a, o_b, z2, w_a, w_b, *, ga_col0, gb_col0, tm, tn, name):
    m = o_a.shape[0]
    d = w_a.shape[1]
    ga0, gb0 = ga_col0 // tn, gb_col0 // tn
    return pl.pallas_call(
        _merge_body,
        grid=(m // tm, d // tn),
        in_specs=[
            pl.BlockSpec((tm, o_a.shape[1]), lambda i, j: (i, 0)),
            pl.BlockSpec((tm, o_b.shape[1]), lambda i, j: (i, 0)),
            pl.BlockSpec((tm, tn), lambda i, j: (i, ga0 + j)),
            pl.BlockSpec((tm, tn), lambda i, j: (i, gb0 + j)),
            pl.BlockSpec((w_a.shape[0], tn), lambda i, j: (0, j)),
            pl.BlockSpec((w_b.shape[0], tn), lambda i, j: (0, j)),
        ],
        out_specs=pl.BlockSpec((tm, tn), lambda i, j: (i, j)),
        out_shape=jax.ShapeDtypeStruct((m, d), BF16),
        compiler_params=_params(2, 48),
        name=name,
    )(o_a, o_b, z2, z2, w_a, w_b)


def _out_norm_body(m_ref, w_ref, x_ref, g_ref, o_ref):
    y = jnp.dot(m_ref[...], w_ref[...], preferred_element_type=F32)
    o_ref[...] = x_ref[...] + _rmsnorm_rows(y, g_ref[...])


def _out_norm(merged, w_out, x2d, g, *, tm, name):
    m, d = x2d.shape
    return pl.pallas_call(
        _out_norm_body,
        grid=(m // tm,),
        in_specs=[
            pl.BlockSpec((tm, d), lambda i: (i, 0)),
            pl.BlockSpec((d, d), lambda i: (0, 0)),
            pl.BlockSpec((tm, d), lambda i: (i, 0)),
            pl.BlockSpec((1, d), lambda i: (0, 0)),
        ],
        out_specs=pl.BlockSpec((tm, d), lambda i: (i, 0)),
        out_shape=jax.ShapeDtypeStruct((m, d), F32),
        compiler_params=_params(1, 48),
        name=name,
    )(merged, w_out, x2d, g)


def _ffn_body(x_ref, g1_ref, w1_ref, w2_ref, g2_ref, o_ref, xn_ref):
    f = pl.program_id(1)

    @pl.when(f == 0)
    def _():
        xn_ref[...] = _rmsnorm_rows(x_ref[...], g1_ref[...]).astype(BF16)
        o_ref[...] = jnp.zeros_like(o_ref)

    hid = jnp.dot(xn_ref[...], w1_ref[...], preferred_element_type=F32)
    hid = jnp.square(jnp.maximum(hid, 0.0)).astype(BF16)
    o_ref[...] += jnp.dot(hid, w2_ref[...], preferred_element_type=F32)

    @pl.when(f == pl.num_programs(1) - 1)
    def _():
        o_ref[...] = x_ref[...] + _rmsnorm_rows(o_ref[...], g2_ref[...])


def _ffn(x2d, g1, w1, w2, g2, *, tm, tf, name):
    m, d = x2d.shape
    d_ff = w1.shape[1]
    return pl.pallas_call(
        _ffn_body,
        grid=(m // tm, d_ff // tf),
        in_specs=[
            pl.BlockSpec((tm, d), lambda i, f: (i, 0)),
            pl.BlockSpec((1, d), lambda i, f: (0, 0)),
            pl.BlockSpec((d, tf), lambda i, f: (0, f)),
            pl.BlockSpec((tf, d), lambda i, f: (f, 0)),
            pl.BlockSpec((1, d), lambda i, f: (0, 0)),
        ],
        out_specs=pl.BlockSpec((tm, d), lambda i, f: (i, 0)),
        out_shape=jax.ShapeDtypeStruct((m, d), F32),
        scratch_shapes=[pltpu.VMEM((tm, d), BF16)],
        compiler_params=_params(2, 48),
        name=name,
    )(x2d, g1, w1, w2, g2)


def _pick_tile(n, target):
    t = min(n, target)
    while n % t:
        t //= 2
    return t


def _layer(x, cache_k, cache_v, conv_state, h_state, wts, tag):
    b, t, d = x.shape
    m = b * t
    x2d = x.reshape(m, d)
    d_rnn = wts["conv_w"].shape[1]
    n_in = wts["w_in"].shape[1]
    col_k, col_v, col_xr = D_ATTN, 2 * D_ATTN, 3 * D_ATTN
    col_gr = col_xr + d_rnn
    col_ga = col_gr + d_rnn
    col_gb = col_ga + d

    tm = _pick_tile(m, 1024)
    tn = 1024
    keep = min(KV_REACH, t)
    if keep == t:
        period, state_rows = 1, tm
    else:
        assert t % tm == 0 and keep <= tm
        period, state_rows = t // tm, keep
    rnn_w = (wts["conv_w"], wts["conv_b"], wts["w_gates"], wts["b_rg"], wts["b_ig"], wts["lru_lambda"])
    heads = 2
    if cache_k is None:
        conv0 = jnp.zeros((b, SUBLANES, d_rnn), F32)
        h0 = jnp.zeros((b, 1, d_rnn), F32)
        z2, state, o_b, h_last = _in_proj_rglru(
            x2d, wts["pre_mix_g"], wts["w_in"], conv0, h0, *rnn_w, tm=tm, tn=tn, tt=_pick_tile(tm, 256),
            period=period, state_rows=state_rows, state_col0=col_k, state_cols=col_gr - col_k,
            xr_col0=col_xr, gr_col0=col_gr, name=f"in_proj_rglru_{tag}")
        z3 = z2.reshape(b, t, n_in)
        offsets = [i * CHUNK for i in range(PAST_CHUNKS + 2)]
        o_a = _attention(z3, z3, z3, wts["rel_bias"], q_col0=0, k_col0=col_k, v_col0=col_v, bb=1,
                         heads=heads, offsets=offsets, name=f"attn_{tag}")
    else:
        z2, state = _in_proj(x2d, wts["pre_mix_g"], wts["w_in"], tm=tm, tn=tn, period=period,
                             state_rows=state_rows, state_col0=col_k, state_cols=col_gr - col_k,
                             name=f"in_proj_{tag}")
        z3 = z2.reshape(b, t, n_in)
        n_cached = cache_k.shape[1]
        assert t == CHUNK and n_cached == KV_REACH
        o_a = _attention(z3, z3, z3, wts["rel_bias"], q_col0=0, k_col0=col_k, v_col0=col_v,
                         bb=_pick_tile(b, 8), heads=heads, offsets=[n_cached], name=f"attn_{tag}",
                         k_cache=cache_k.astype(BF16).reshape(b, n_cached, D_ATTN),
                         v_cache=cache_v.astype(BF16).reshape(b, n_cached, D_ATTN))
        conv0 = jnp.pad(conv_state.astype(F32), ((0, 0), (SUBLANES - (CONV_WIDTH - 1), 0), (0, 0)))
        h0 = h_state.astype(F32).reshape(b, 1, d_rnn)
        o_b, h_last = _rglru(z3, conv0, h0, *rnn_w, xr_col0=col_xr, gr_col0=col_gr,
                             tt=_pick_tile(t, 256), cw=1024, name=f"rglru_{tag}")
    state = state.reshape(b, keep, -1)
    new_k = state[:, :, :D_ATTN].reshape(b, keep, N_HEADS, HEAD_DIM)
    new_v = state[:, :, D_ATTN:2 * D_ATTN].reshape(b, keep, N_HEADS, HEAD_DIM)
    conv_tail = state[:, keep - (CONV_WIDTH - 1):, 2 * D_ATTN:col_gr - col_k]

    merged = _gate_merge(o_a.reshape(m, D_ATTN), o_b.reshape(m, d_rnn), z2, wts["w_attn_up"],
                         wts["w_rnn_up"], ga_col0=col_ga, gb_col0=col_gb, tm=tm, tn=tn,
                         name=f"gate_merge_{tag}")
    tm2 = _pick_tile(m, 512)
    x1 = _out_norm(merged, wts["w_out"], x2d, wts["post_mix_g"], tm=tm2, name=f"out_norm_{tag}")
    y = _ffn(x1, wts["pre_ffn_g"], wts["w_ff1"], wts["w_ff2"], wts["post_ffn_g"], tm=tm2, tf=1024,
             name=f"ffn_{tag}")
    return y.reshape(b, t, d), new_k, new_v, conv_tail, h_last.reshape(b, d_rnn)


def kernel(x_prompt, x_sample, cache_k, cache_v, state_conv, state_h, pre_mix_g, w_in, rel_bias, conv_w, conv_b, w_rg, b_rg, w_ig, b_ig, lru_lambda, w_attn_up, w_rnn_up, w_out, post_mix_g, pre_ffn_g, w_ff1, w_ff2, post_ffn_g):
    depth = w_in.shape[0]
    y_p, y_s = x_prompt, x_sample
    outs_p, outs_s = [], []
    for l in range(depth):
        wts = {
            "pre_mix_g": pre_mix_g[l][None], "post_mix_g": post_mix_g[l][None],
            "pre_ffn_g": pre_ffn_g[l][None], "post_ffn_g": post_ffn_g[l][None],
            "w_in": w_in[l].astype(BF16), "rel_bias": rel_bias[l],
            "conv_w": conv_w[l], "conv_b": conv_b[l][None],
            "w_gates": jnp.concatenate([w_rg[l], w_ig[l]], axis=-1).astype(BF16),
            "b_rg": b_rg[l][None], "b_ig": b_ig[l][None], "lru_lambda": lru_lambda[l][None],
            "w_attn_up": w_attn_up[l].astype(BF16), "w_rnn_up": w_rnn_up[l].astype(BF16),
            "w_out": w_out[l].astype(BF16), "w_ff1": w_ff1[l].astype(BF16), "w_ff2": w_ff2[l].astype(BF16),
        }
        y_p, *st_p = _layer(y_p, None, None, None, None, wts, f"p{l}")
        y_s, *st_s = _layer(y_s, cache_k[l], cache_v[l], state_conv[l], state_h[l], wts, f"s{l}")
        outs_p.append(st_p)
        outs_s.append(st_s)
    stack = lambda outs, i: jnp.stack([o[i] for o in outs])
    return (y_p, y_s,
            stack(outs_p, 0), stack(outs_p, 1), stack(outs_p, 2), stack(outs_p, 3),
            stack(outs_s, 0), stack(outs_s, 1), stack(outs_s, 2), stack(outs_s, 3))
```

```python
import functools

import jax
import jax.numpy as jnp
from jax import lax
from jax.experimental import pallas as pl
from jax.experimental.pallas import tpu as pltpu

F32 = jnp.float32
BF16 = jnp.bfloat16

CHUNK = 64
PAST_CHUNKS = 8
KV_REACH = PAST_CHUNKS * CHUNK
N_HEADS = 16
HEAD_DIM = 64
D_ATTN = N_HEADS * HEAD_DIM
MAX_REL = 256
RNN_BLOCK = 128
CONV_WIDTH = 4
LRU_C = 8.0
EPS = 1e-6
NEG_INF = -1e30

LANES = 128
SUBLANES = 8
MIB = 1024 * 1024

ROW_TILE = 1024
COL_TILE = 1024
WIDE_ROW_TILE = 512
FFN_TILE = 1024
RNN_FRAMES = 256
HEADS_PER_STEP = 2
SAMPLE_SEQS = 8
KT_BLOCK = 512
VMEM_PROJ, VMEM_FUSED, VMEM_ATTN, VMEM_RNN, VMEM_MLP = 52, 58, 40, 40, 48


def _params(n_axes, vmem_mib):
    return pltpu.CompilerParams(
        dimension_semantics=("arbitrary",) * n_axes,
        vmem_limit_bytes=vmem_mib * MIB,
    )


def _rmsnorm_rows(x, g):
    ms = jnp.mean(x * x, axis=-1, keepdims=True)
    return x * lax.rsqrt(ms + EPS) * g


def _in_proj_body(x_ref, g_ref, w_ref, o_ref, s_ref, xn_ref, *, period, j0, j1):
    i, j = pl.program_id(0), pl.program_id(1)

    @pl.when(j == 0)
    def _():
        xn_ref[...] = _rmsnorm_rows(x_ref[...], g_ref[...]).astype(BF16)

    keep_f32 = (i % period == period - 1) & (j >= j0) & (j < j1)

    @pl.when(keep_f32)
    def _():
        acc = jnp.dot(xn_ref[...], w_ref[...], preferred_element_type=F32)
        o_ref[...] = acc.astype(o_ref.dtype)
        s_ref[...] = acc[acc.shape[0] - s_ref.shape[0]:, :]

    @pl.when(jnp.logical_not(keep_f32))
    def _():
        o_ref[...] = jnp.dot(xn_ref[...], w_ref[...], preferred_element_type=F32).astype(o_ref.dtype)


def _in_proj(x2d, g, w, *, tm, tn, period, state_rows, state_col0, state_cols, name):
    m, d = x2d.shape
    n = w.shape[1]
    j0, j1 = state_col0 // tn, (state_col0 + state_cols) // tn
    n_state_tiles = m // tm // period

    def state_block(i, j):
        live = i % period == period - 1
        return i // period, jnp.where(live, jnp.clip(j - j0, 0, j1 - j0 - 1), 0)

    return pl.pallas_call(
        functools.partial(_in_proj_body, period=period, j0=j0, j1=j1),
        grid=(m // tm, n // tn),
        in_specs=[
            pl.BlockSpec((tm, d), lambda i, j: (i, 0)),
            pl.BlockSpec((1, d), lambda i, j: (0, 0)),
            pl.BlockSpec((d, tn), lambda i, j: (0, j)),
        ],
        out_specs=[
            pl.BlockSpec((tm, tn), lambda i, j: (i, j)),
            pl.BlockSpec((state_rows, tn), state_block),
        ],
        out_shape=[
            jax.ShapeDtypeStruct((m, n), BF16),
            jax.ShapeDtypeStruct((n_state_tiles * state_rows, state_cols), F32),
        ],
        scratch_shapes=[pltpu.VMEM((tm, d), BF16)],
        compiler_params=_params(2, VMEM_PROJ),
        name=name,
    )(x2d, g, w)


def _bias_diagonals(rel_bias, width, offsets, heads):
    x = jnp.arange(width)
    rows = [rel_bias[:, jnp.clip(off + CHUNK - 1 - x, -MAX_REL, MAX_REL) + MAX_REL] for off in offsets]
    g = jnp.stack(rows, axis=1).astype(F32)
    n_groups = rel_bias.shape[0] // heads
    g = g.reshape(n_groups, heads, len(offsets), width)
    return jnp.swapaxes(g, 1, 2).reshape(n_groups, len(offsets) * heads, width)


def _attn_body(*refs, bb, n_chunks, kw, heads, offsets, unit, cached):
    if cached:
        q_ref, kc_ref, vc_ref, k_ref, v_ref, g_ref, o_ref, tab_ref, s_ref, p_ref, l_ref = refs
        band = lambda c_ref, n_ref, bi, ks: jnp.concatenate([c_ref[bi], n_ref[bi]], axis=0)
    else:
        q_ref, k_ref, v_ref, g_ref, o_ref, tab_ref, s_ref, p_ref, l_ref, kt_ref = refs
        kc_ref = vc_ref = None
        band = lambda c_ref, n_ref, bi, ks: n_ref[bi, pl.ds(ks, kw), :]
    width = g_ref.shape[2]
    hw = heads * HEAD_DIM

    @pl.when(pl.program_id(1) == 0)
    def _():
        r = lax.broadcasted_iota(jnp.int32, (CHUNK, kw), 0)
        j = lax.broadcasted_iota(jnp.int32, (CHUNK, kw), 1)
        for c, off in enumerate(offsets):
            cq = (r + off) // CHUNK
            ck = j // CHUNK
            valid = (ck <= cq) & (ck >= cq - PAST_CHUNKS)
            for h in range(heads):
                g = jnp.broadcast_to(g_ref[0, c * heads + h:c * heads + h + 1, :], (CHUNK, width))
                t = pltpu.roll(g, width - (CHUNK - 1), 1, stride=1, stride_axis=0)
                tab_ref[c, h * CHUNK:(h + 1) * CHUNK, :] = jnp.where(valid, t[:, :kw], NEG_INF)

    if not cached:
        for r0 in range(0, k_ref.shape[1], KT_BLOCK):
            kt_ref[:, r0:r0 + KT_BLOCK] = k_ref[0, r0:r0 + KT_BLOCK, :].T

    lane = lax.broadcasted_iota(jnp.int32, (1, hw), 1)
    head_lanes = [(lane >= h * HEAD_DIM) & (lane < (h + 1) * HEAD_DIM) for h in range(heads)]
    rows = heads * CHUNK
    n_units = bb * n_chunks // unit

    def locate(u, j):
        idx = u * unit + j
        bi, c = (0, idx) if bb == 1 else (idx // n_chunks, idx % n_chunks)
        q0 = pl.multiple_of(c * CHUNK, CHUNK)
        if cached:
            return bi, 0, q0, 0
        past = jnp.maximum(c - PAST_CHUNKS, 0)
        odd = past % 2
        ks = pl.multiple_of((past - odd) * CHUNK, 2 * CHUNK)
        return bi, jnp.minimum(c, PAST_CHUNKS) + odd, q0, ks

    def scores(u, slot):
        for j in range(unit):
            bi, c, q0, ks = locate(u, j)
            q = q_ref[bi, pl.ds(q0, CHUNK), :]
            qs = jnp.concatenate([jnp.where(m, q, jnp.zeros_like(q)) for m in head_lanes], axis=0)
            qs = qs * jnp.asarray(HEAD_DIM ** -0.5, q.dtype)
            if cached:
                kb = band(kc_ref, k_ref, bi, ks)
                s = lax.dot_general(qs, kb, (((1,), (1,)), ((), ())), preferred_element_type=F32)
            else:
                s = jnp.dot(qs, kt_ref[:, pl.ds(ks, kw)], preferred_element_type=F32)
            s_ref[slot, j * rows:(j + 1) * rows, :] = s + tab_ref[c]

    def softmax(slot):
        s = s_ref[slot]
        p = jnp.exp(s - jnp.max(s, axis=-1, keepdims=True))
        p_ref[slot] = p.astype(p_ref.dtype)
        l_ref[slot] = 1.0 / jnp.sum(p, axis=-1, keepdims=True)

    def output(u, slot):
        for j in range(unit):
            bi, c, q0, ks = locate(u, j)
            vb = band(vc_ref, v_ref, bi, ks)
            o_all = jnp.dot(p_ref[slot, j * rows:(j + 1) * rows, :], vb, preferred_element_type=F32)
            o_all = o_all * l_ref[slot, j * rows:(j + 1) * rows, :]
            o = o_all[0:CHUNK]
            for h in range(1, heads):
                o = jnp.where(head_lanes[h], o_all[h * CHUNK:(h + 1) * CHUNK], o)
            o_ref[bi, pl.ds(q0, CHUNK), :] = o.astype(o_ref.dtype)

    def step(i, slot, do_scores, do_softmax, do_output):
        if do_scores:
            scores(i, slot)
        if do_output:
            output(i - 2, slot)
        if do_softmax:
            softmax(1 - slot)

    for i in range(2):
        step(i, i % 2, i < n_units, 1 <= i <= n_units, False)
    n_pairs = max(n_units - 2, 0) // 2
    if n_pairs:
        def body(t, carry):
            i = 2 + 2 * t
            step(i, 0, True, True, True)
            step(i + 1, 1, True, True, True)
            return carry
        lax.fori_loop(0, n_pairs, body, 0)
    for i in range(2 + 2 * n_pairs, n_units + 2):
        step(i, i % 2, i < n_units, 1 <= i <= n_units, i >= 2)


def _attention(q_arr, k_arr, v_arr, rel_bias, *, q_col0, k_col0, v_col0, bb, heads, offsets, name,
               k_cache=None, v_cache=None):
    b, sq, _ = q_arr.shape
    sk = k_arr.shape[1]
    cached = k_cache is not None
    if cached:
        kw = KV_REACH + CHUNK
        assert sq == CHUNK and sk == CHUNK and k_cache.shape[1] == KV_REACH and len(offsets) == 1
    else:
        kw = KV_REACH + 2 * CHUNK
        assert bb == 1 and sq == sk and sk % KT_BLOCK == 0 and sk >= kw and len(offsets) == PAST_CHUNKS + 2
    hw = heads * HEAD_DIM
    n_groups = N_HEADS // heads
    n_case = len(offsets)
    width = -(-(kw + CHUNK - 1) // LANES) * LANES
    diag = _bias_diagonals(rel_bias, width, offsets, heads)
    qc, kc, vc = q_col0 // hw, k_col0 // hw, v_col0 // hw
    unit = 2
    assert (bb * sq // CHUNK) % unit == 0
    rows = unit * heads * CHUNK
    cache_specs = [pl.BlockSpec((bb, KV_REACH, hw), lambda g, i: (i, 0, g))] * 2 if cached else []
    cache_args = [k_cache, v_cache] if cached else []
    return pl.pallas_call(
        functools.partial(_attn_body, bb=bb, n_chunks=sq // CHUNK, kw=kw, heads=heads,
                          offsets=tuple(offsets), unit=unit, cached=cached),
        grid=(n_groups, b // bb),
        in_specs=[pl.BlockSpec((bb, sq, hw), lambda g, i: (i, 0, qc + g))] + cache_specs + [
            pl.BlockSpec((bb, sk, hw), lambda g, i: (i, 0, kc + g)),
            pl.BlockSpec((bb, sk, hw), lambda g, i: (i, 0, vc + g)),
            pl.BlockSpec((1, n_case * heads, width), lambda g, i: (g, 0, 0)),
        ],
        out_specs=pl.BlockSpec((bb, sq, hw), lambda g, i: (i, 0, g)),
        out_shape=jax.ShapeDtypeStruct((b, sq, D_ATTN), BF16),
        scratch_shapes=[
            pltpu.VMEM((n_case, heads * CHUNK, kw), F32),
            pltpu.VMEM((2, rows, kw), F32),
            pltpu.VMEM((2, rows, kw), BF16),
            pltpu.VMEM((2, rows, 1), F32),
        ] + ([] if cached else [pltpu.VMEM((hw, sk), BF16)]),
        compiler_params=_params(2, VMEM_ATTN),
        name=name,
    )(q_arr, *cache_args, k_arr, v_arr, diag)


def _log_sigmoid(x):
    return jnp.minimum(x, 0.0) - jnp.log1p(jnp.exp(-jnp.abs(x)))


def _shift_matrix(tt):
    t = jnp.arange(tt)
    blocks = [(t[:, None] - d) == t[None, :] for d in range(CONV_WIDTH - 1, 0, -1)]
    return jnp.concatenate(blocks, axis=0).astype(BF16)


def _rglru_head(x_bf, prev8, shift, conv_w, conv_b, wg_ref, b_rg, b_ig):
    tt, cw = x_bf.shape
    taps = CONV_WIDTH - 1
    x = x_bf.astype(F32)
    row = lax.broadcasted_iota(jnp.int32, (SUBLANES, cw), 0)
    if shift is not None:
        sh = jnp.dot(shift, x_bf, preferred_element_type=F32)
        xc = conv_b + conv_w[0:1] * sh[0:tt]
        for k in range(1, taps):
            xc = xc + conv_w[k:k + 1] * sh[k * tt:(k + 1) * tt]
        xc = xc + conv_w[taps:taps + 1] * x
        head = xc[0:SUBLANES]
        for k in range(taps):
            d = taps - k
            head = head + conv_w[k:k + 1] * jnp.where(row < d, pltpu.roll(prev8, d, 0), 0.0)
        xc = jnp.concatenate([head, xc[SUBLANES:]], axis=0)
    else:
        xc = conv_b
        for k in range(taps):
            d = taps - k
            sh = pltpu.roll(x, d, 0)
            head = jnp.where(row < d, pltpu.roll(prev8, d, 0), sh[0:SUBLANES])
            xc = xc + conv_w[k:k + 1] * jnp.concatenate([head, sh[SUBLANES:]], axis=0)
        xc = xc + conv_w[taps:taps + 1] * x

    xcb = xc.astype(BF16)
    r_parts, i_parts = [], []
    for n in range(cw // RNN_BLOCK):
        gts = jnp.dot(xcb[:, n * RNN_BLOCK:(n + 1) * RNN_BLOCK], wg_ref[n], preferred_element_type=F32)
        r_parts.append(gts[:, :RNN_BLOCK])
        i_parts.append(gts[:, RNN_BLOCK:])
    r_pre = jnp.concatenate(r_parts, axis=1) + b_rg
    i_pre = jnp.concatenate(i_parts, axis=1) + b_ig
    return xc, r_pre, i_pre, x[tt - SUBLANES:tt]


def _rglru_tail(xc, r_pre, i_pre, gr_bf, h_in, lam):
    rows, cw = xc.shape
    r = jax.nn.sigmoid(r_pre)
    i = jax.nn.sigmoid(i_pre)
    log_a = r * (LRU_C * _log_sigmoid(lam))
    a = jnp.exp(log_a)
    w = -jnp.tanh(log_a)
    one_minus_a2 = 2.0 * w / (1.0 + w)
    root = jnp.where(one_minus_a2 > 0.0, one_minus_a2 * lax.rsqrt(one_minus_a2), 0.0)
    u = root * (i * xc)

    row = lax.broadcasted_iota(jnp.int32, (SUBLANES, cw), 0)
    h = h_in
    hs = []
    for g in range(rows // SUBLANES):
        ag = a[g * SUBLANES:(g + 1) * SUBLANES]
        ug = u[g * SUBLANES:(g + 1) * SUBLANES]
        for d in (1, 2, 4):
            keep = row >= d
            a_prev = jnp.where(keep, pltpu.roll(ag, d, 0), 1.0)
            u_prev = jnp.where(keep, pltpu.roll(ug, d, 0), 0.0)
            ug = ag * u_prev + ug
            ag = ag * a_prev
        hg = ug + ag * h
        hs.append(hg)
        h = jnp.broadcast_to(hg[SUBLANES - 1:SUBLANES, :], (SUBLANES, cw))
    gate = jax.nn.gelu(gr_bf.astype(F32), approximate=True)
    return (jnp.concatenate(hs, axis=0) * gate).astype(BF16), h


def _rglru_body(xr_ref, gr_ref, c0_ref, h0_ref, sh_ref, cw_ref, cb_ref, wg_ref, brg_ref, big_ref, lam_ref,
                ob_ref, hl_ref, prev_ref, hcar_ref):
    @pl.when(pl.program_id(2) == 0)
    def _():
        hcar_ref[...] = jnp.broadcast_to(h0_ref[0], hcar_ref.shape)
        prev_ref[...] = c0_ref[0]

    xc, r_pre, i_pre, prev8 = _rglru_head(xr_ref[0], prev_ref[...], sh_ref[...], cw_ref[...], cb_ref[...],
                                          wg_ref, brg_ref[...], big_ref[...])
    out, h = _rglru_tail(xc, r_pre, i_pre, gr_ref[0], hcar_ref[...], lam_ref[...])
    ob_ref[0] = out
    prev_ref[...] = prev8
    hcar_ref[...] = h
    hl_ref[0] = h[0:1, :]


def _rglru(z3, conv0, h0, conv_w, conv_b, w_gates, b_rg, b_ig, lam, *, xr_col0, gr_col0, tt, cw, name):
    b, t, _ = z3.shape
    d_rnn = conv_w.shape[1]
    n_cg = d_rnn // cw
    blocks_per_cg = cw // RNN_BLOCK
    xc0, gc0 = xr_col0 // cw, gr_col0 // cw
    vec = pl.BlockSpec((1, cw), lambda c, i, s: (0, c))
    return pl.pallas_call(
        _rglru_body,
        grid=(n_cg, b, t // tt),
        in_specs=[
            pl.BlockSpec((1, tt, cw), lambda c, i, s: (i, s, xc0 + c)),
            pl.BlockSpec((1, tt, cw), lambda c, i, s: (i, s, gc0 + c)),
            pl.BlockSpec((1, SUBLANES, cw), lambda c, i, s: (i, 0, c)),
            pl.BlockSpec((1, 1, cw), lambda c, i, s: (i, 0, c)),
            pl.BlockSpec(((CONV_WIDTH - 1) * tt, tt), lambda c, i, s: (0, 0)),
            pl.BlockSpec((CONV_WIDTH, cw), lambda c, i, s: (0, c)),
            vec,
            pl.BlockSpec((blocks_per_cg, RNN_BLOCK, 2 * RNN_BLOCK), lambda c, i, s: (c, 0, 0)),
            vec, vec, vec,
        ],
        out_specs=[
            pl.BlockSpec((1, tt, cw), lambda c, i, s: (i, s, c)),
            pl.BlockSpec((1, 1, cw), lambda c, i, s: (i, 0, c)),
        ],
        out_shape=[
            jax.ShapeDtypeStruct((b, t, d_rnn), BF16),
            jax.ShapeDtypeStruct((b, 1, d_rnn), F32),
        ],
        scratch_shapes=[
            pltpu.VMEM((SUBLANES, cw), F32),
            pltpu.VMEM((SUBLANES, cw), F32),
        ],
        compiler_params=_params(3, VMEM_RNN),
        name=name,
    )(z3, z3, conv0, h0, _shift_matrix(tt), conv_w, conv_b, w_gates, b_rg, b_ig, lam)


def _lookup(table, j):
    out = jnp.int32(table[-1])
    for k in range(len(table) - 2, -1, -1):
        out = jnp.where(j == k, jnp.int32(table[k]), out)
    return out


def _fused_body(x_ref, g_ref, w_ref, c0_ref, h0_ref, cw_ref, cb_ref, wg_ref, brg_ref, big_ref,
                lam_ref, z_ref, s_ref, ob_ref, hl_ref, xn_ref, stash_ref, prev_ref, hcar_ref, hd_ref, gr_ref, *,
                n_tiles, period, n_cg, pieces_per_cg, tt, n_split, stash_slots):
    i, j = pl.program_id(0), pl.program_id(1)
    cw = ob_ref.shape[1]

    @pl.when((j == 0) & (i < n_tiles))
    def _():
        xn_ref[...] = _rmsnorm_rows(x_ref[...], g_ref[...]).astype(BF16)

    @pl.when((j == 0) & (i == 0))
    def _():
        prev_ref[...] = jnp.zeros_like(prev_ref)
        hcar_ref[...] = jnp.zeros_like(hcar_ref)

    tm = xn_ref.shape[0]
    rows_s = tt // n_split
    state_start = tm - s_ref.shape[0]
    first_rows = tm // 4
    rest_rows = (tm - first_rows) // n_split
    bounds = [0] + [first_rows + c * rest_rows for c in range(n_split + 1)]

    def project(c):
        lo, hi = bounds[c], bounds[c + 1]
        acc = jnp.dot(xn_ref[lo:hi, :], w_ref[...], preferred_element_type=F32)
        zb = acc.astype(BF16)
        z_ref[lo:hi, :] = zb
        if hi > state_start:
            keep = max(lo, state_start)
            s_ref[keep - state_start:hi - state_start, :] = acc[keep - lo:, :]
        stash_ref[_lookup(stash_slots, j), lo:hi, :] = zb
        return acc[hi - lo - SUBLANES:, :]

    def after(state, acc_rows):
        cg, lam, h = state
        zero = (pltpu.bitcast(acc_rows, jnp.uint32) >> 16) >> 16
        return cg, lam, pltpu.bitcast(pltpu.bitcast(h, jnp.uint32) + zero, F32)

    def recur_load():
        cg, tp = j // pieces_per_cg, j % pieces_per_cg
        r0 = pl.multiple_of(tp * tt, tt)
        first = ((i - 1) % period == 0) & (tp == 0)
        prev8 = jnp.where(first, c0_ref[0, cg], prev_ref[cg])
        h = jnp.where(first, jnp.broadcast_to(h0_ref[0, cg], (SUBLANES, cw)), hcar_ref[cg])
        gr_ref[...] = stash_ref[n_cg + cg, pl.ds(r0, tt), :]
        return cg, stash_ref[cg, pl.ds(r0, tt), :], prev8, h

    def recur_head(loaded):
        cg, x_bf, prev8, h = loaded
        xc, r_pre, i_pre, prev8 = _rglru_head(x_bf, prev8, None, cw_ref[cg], cb_ref[cg], wg_ref.at[cg],
                                              brg_ref[cg], big_ref[cg])
        hd_ref[0] = xc
        hd_ref[1] = r_pre
        hd_ref[2] = i_pre
        prev_ref[cg] = prev8
        return cg, lam_ref[cg], h

    def recur_tail(k, state):
        cg, lam, h = state
        lo = k * rows_s
        out, h = _rglru_tail(hd_ref[0, lo:lo + rows_s, :], hd_ref[1, lo:lo + rows_s, :],
                             hd_ref[2, lo:lo + rows_s, :], gr_ref[lo:lo + rows_s, :], h, lam)
        ob_ref[lo:lo + rows_s, :] = out
        return cg, lam, h

    def recur_end(state):
        cg, _, h = state
        hcar_ref[cg] = h
        hl_ref[0, 0] = h[0:1, :]

    has_proj = i < n_tiles
    has_rec = (i >= 1) & (j < n_cg * pieces_per_cg)

    @pl.when(has_proj & has_rec)
    def _():
        state = after(recur_head(recur_load()), project(0))
        for k in range(n_split):
            state = recur_tail(k, state)
            acc_rows = project(k + 1)
            if k + 1 < n_split:
                state = after(state, acc_rows)
        recur_end(state)

    @pl.when(has_proj & jnp.logical_not(has_rec))
    def _():
        for c in range(n_split + 1):
            project(c)

    @pl.when(jnp.logical_not(has_proj) & has_rec)
    def _():
        state = recur_head(recur_load())
        for k in range(n_split):
            state = recur_tail(k, state)
        recur_end(state)


def _in_proj_rglru(x2d, g, w, conv0, h0, conv_w, conv_b, w_gates, b_rg, b_ig, lam, *, tm, tn, tt, period,
                   state_rows, state_col0, state_cols, xr_col0, gr_col0, name):
    m, d = x2d.shape
    n = w.shape[1]
    d_rnn = conv_w.shape[1]
    n_tiles, n_col, n_cg = m // tm, n // tn, d_rnn // tn
    n_seq = n_tiles // period
    pieces_per_cg = tm // tt
    n_pieces = n_cg * pieces_per_cg
    xr0, gr0 = xr_col0 // tn, gr_col0 // tn
    assert n_pieces + n_cg <= n_col and tm % tt == 0 and n_tiles % period == 0

    order, stash_slots = [None] * n_col, [2 * n_cg] * n_col
    for c in range(n_cg):
        jx = (c + 1) * pieces_per_cg
        order[jx], stash_slots[jx] = xr0 + c, c
        order[jx + 1], stash_slots[jx + 1] = gr0 + c, n_cg + c
    rest = [c for c in range(n_col) if c not in order]
    for jj in range(n_col):
        if order[jj] is None:
            order[jj] = rest.pop(0)
    s0, n_state = state_col0 // tn, state_cols // tn
    state_pos = [order.index(s0 + k) for k in range(n_state)]
    assert state_pos == sorted(state_pos)
    state_blk = [sum(p < jj for p in state_pos) for jj in range(n_col)]

    last = lambda i: i == n_tiles
    row = lambda i: jnp.minimum(i, n_tiles - 1)
    col = lambda i, j: jnp.where(last(i), order[-1], _lookup(order, j))
    seq_prev = lambda i: jnp.maximum(i - 1, 0) // period
    piece = lambda i, j: jnp.where(i == 0, 0, jnp.minimum(j, n_pieces - 1))

    def state_block(i, j):
        live = i % period == period - 1
        blk = jnp.where(live, _lookup(state_blk, j), n_state + 1)
        return row(i) // period, jnp.where(last(i), n_state, blk)

    per_cg = lambda a: a.reshape(a.shape[0], n_cg, tn).swapaxes(0, 1)
    whole = lambda a: pl.BlockSpec(a.shape, lambda i, j: (0,) * a.ndim)
    n_split = 4
    assert tt % (n_split * SUBLANES) == 0 and (tm - tm // 4) % (2 * SUBLANES * n_split) == 0
    params = [per_cg(conv_w), per_cg(conv_b), w_gates.reshape(n_cg, -1, RNN_BLOCK, 2 * RNN_BLOCK),
              per_cg(b_rg), per_cg(b_ig), per_cg(lam)]
    conv0 = conv0.reshape(n_seq, SUBLANES, n_cg, tn).swapaxes(1, 2)
    h0 = h0.reshape(n_seq, 1, n_cg, tn).swapaxes(1, 2)

    z, state, o_b, h_last = pl.pallas_call(
        functools.partial(_fused_body, n_tiles=n_tiles, period=period, n_cg=n_cg,
                          pieces_per_cg=pieces_per_cg, tt=tt, n_split=n_split,
                          stash_slots=tuple(stash_slots)),
        grid=(n_tiles + 1, n_col),
        in_specs=[
            pl.BlockSpec((tm, d), lambda i, j: (row(i), 0)),
            pl.BlockSpec((1, d), lambda i, j: (0, 0)),
            pl.BlockSpec((d, tn), lambda i, j: (0, col(i, j))),
            pl.BlockSpec((1, n_cg, SUBLANES, tn), lambda i, j: (seq_prev(i), 0, 0, 0)),
            pl.BlockSpec((1, n_cg, 1, tn), lambda i, j: (seq_prev(i), 0, 0, 0)),
        ] + [whole(p) for p in params],
        out_specs=[
            pl.BlockSpec((tm, tn), lambda i, j: (row(i), col(i, j))),
            pl.BlockSpec((state_rows, tn), state_block),
            pl.BlockSpec((tt, tn), lambda i, j: (jnp.maximum(i - 1, 0) * pieces_per_cg
                                                 + piece(i, j) % pieces_per_cg, piece(i, j) // pieces_per_cg)),
            pl.BlockSpec((1, 1, 1, tn), lambda i, j: (jnp.maximum(i - 1, 0), piece(i, j) // pieces_per_cg, 0, 0)),
        ],
        out_shape=[
            jax.ShapeDtypeStruct((m, n), BF16),
            jax.ShapeDtypeStruct((n_seq * state_rows, (n_state + 2) * tn), F32),
            jax.ShapeDtypeStruct((m, d_rnn), BF16),
            jax.ShapeDtypeStruct((n_tiles, n_cg, 1, tn), F32),
        ],
        scratch_shapes=[
            pltpu.VMEM((tm, d), BF16),
            pltpu.VMEM((2 * n_cg + 1, tm, tn), BF16),
            pltpu.VMEM((n_cg, SUBLANES, tn), F32),
            pltpu.VMEM((n_cg, SUBLANES, tn), F32),
            pltpu.VMEM((3, tt, tn), F32),
            pltpu.VMEM((tt, tn), BF16),
        ],
        compiler_params=_params(2, VMEM_FUSED),
        name=name,
    )(x2d, g, w, conv0, h0, *params)
    return z, state, o_b, h_last.reshape(n_tiles, d_rnn)[period - 1::period]


def _merge_body(oa_ref, ob_ref, ga_ref, gb_ref, wa_ref, wb_ref, o_ref):
    ya = jnp.dot(oa_ref[...], wa_ref[...], preferred_element_type=F32)
    yb = jnp.dot(ob_ref[...], wb_ref[...], preferred_element_type=F32)
    ga = jax.nn.sigmoid(ga_ref[...].astype(F32))
    gb = jax.nn.sigmoid(gb_ref[...].astype(F32))
    o_ref[...] = (ga * ya + gb * yb).astype(o_ref.dtype)


def _gate_merge(o_a, o_b, z2, w_a, w_b, *, ga_col0, gb_col0, tm, tn, name):
    m = o_a.shape[0]
    d = w_a.shape[1]
    ga0, gb0 = ga_col0 // tn, gb_col0 // tn
    return pl.pallas_call(
        _merge_body,
        grid=(m // tm, d // tn),
        in_specs=[
            pl.BlockSpec((tm, o_a.shape[1]), lambda i, j: (i, 0)),
            pl.BlockSpec((tm, o_b.shape[1]), lambda i, j: (i, 0)),
            pl.BlockSpec((tm, tn), lambda i, j: (i, ga0 + j)),
            pl.BlockSpec((tm, tn), lambda i, j: (i, gb0 + j)),
            pl.BlockSpec((w_a.shape[0], tn), lambda i, j: (0, j)),
            pl.BlockSpec((w_b.shape[0], tn), lambda i, j: (0, j)),
        ],
        out_specs=pl.BlockSpec((tm, tn), lambda i, j: (i, j)),
        out_shape=jax.ShapeDtypeStruct((m, d), BF16),
        compiler_params=_params(2, VMEM_MLP),
        name=name,
    )(o_a, o_b, z2, z2, w_a, w_b)


def _out_norm_body(m_ref, w_ref, x_ref, g_ref, o_ref):
    y = jnp.dot(m_ref[...], w_ref[...], preferred_element_type=F32)
    o_ref[...] = x_ref[...] + _rmsnorm_rows(y, g_ref[...])


def _out_norm(merged, w_out, x2d, g, *, tm, name):
    m, d = x2d.shape
    return pl.pallas_call(
        _out_norm_body,
        grid=(m // tm,),
        in_specs=[
            pl.BlockSpec((tm, d), lambda i: (i, 0)),
            pl.BlockSpec((d, d), lambda i: (0, 0)),
            pl.BlockSpec((tm, d), lambda i: (i, 0)),
            pl.BlockSpec((1, d), lambda i: (0, 0)),
        ],
        out_specs=pl.BlockSpec((tm, d), lambda i: (i, 0)),
        out_shape=jax.ShapeDtypeStruct((m, d), F32),
        compiler_params=_params(1, VMEM_MLP),
        name=name,
    )(merged, w_out, x2d, g)


def _ffn_body(x_ref, g1_ref, w1_ref, w2_ref, g2_ref, o_ref, xn_ref):
    f = pl.program_id(1)

    @pl.when(f == 0)
    def _():
        xn_ref[...] = _rmsnorm_rows(x_ref[...], g1_ref[...]).astype(BF16)
        o_ref[...] = jnp.zeros_like(o_ref)

    hid = jnp.dot(xn_ref[...], w1_ref[...], preferred_element_type=F32)
    hid = jnp.square(jnp.maximum(hid, 0.0)).astype(BF16)
    o_ref[...] += jnp.dot(hid, w2_ref[...], preferred_element_type=F32)

    @pl.when(f == pl.num_programs(1) - 1)
    def _():
        o_ref[...] = x_ref[...] + _rmsnorm_rows(o_ref[...], g2_ref[...])


def _ffn(x2d, g1, w1, w2, g2, *, tm, tf, name):
    m, d = x2d.shape
    d_ff = w1.shape[1]
    return pl.pallas_call(
        _ffn_body,
        grid=(m // tm, d_ff // tf),
        in_specs=[
            pl.BlockSpec((tm, d), lambda i, f: (i, 0)),
            pl.BlockSpec((1, d), lambda i, f: (0, 0)),
            pl.BlockSpec((d, tf), lambda i, f: (0, f)),
            pl.BlockSpec((tf, d), lambda i, f: (f, 0)),
            pl.BlockSpec((1, d), lambda i, f: (0, 0)),
        ],
        out_specs=pl.BlockSpec((tm, d), lambda i, f: (i, 0)),
        out_shape=jax.ShapeDtypeStruct((m, d), F32),
        scratch_shapes=[pltpu.VMEM((tm, d), BF16)],
        compiler_params=_params(2, VMEM_MLP),
        name=name,
    )(x2d, g1, w1, w2, g2)


def _pick_tile(n, target):
    t = min(n, target)
    while n % t:
        t //= 2
    return t


def _layer(x, cache_k, cache_v, conv_state, h_state, wts, tag):
    b, t, d = x.shape
    m = b * t
    x2d = x.reshape(m, d)
    d_rnn = wts["conv_w"].shape[1]
    n_in = wts["w_in"].shape[1]
    col_k, col_v, col_xr = D_ATTN, 2 * D_ATTN, 3 * D_ATTN
    col_gr = col_xr + d_rnn
    col_ga = col_gr + d_rnn
    col_gb = col_ga + d

    tm = _pick_tile(m, ROW_TILE)
    tn = COL_TILE
    keep = min(KV_REACH, t)
    if keep == t:
        period, state_rows = 1, tm
    else:
        assert t % tm == 0 and keep <= tm
        period, state_rows = t // tm, keep
    rnn_w = (wts["conv_w"], wts["conv_b"], wts["w_gates"], wts["b_rg"], wts["b_ig"], wts["lru_lambda"])
    heads = HEADS_PER_STEP
    if cache_k is None:
        conv0 = jnp.zeros((b, SUBLANES, d_rnn), F32)
        h0 = jnp.zeros((b, 1, d_rnn), F32)
        z2, state, o_b, h_last = _in_proj_rglru(
            x2d, wts["pre_mix_g"], wts["w_in"], conv0, h0, *rnn_w, tm=tm, tn=tn, tt=_pick_tile(tm, RNN_FRAMES),
            period=period, state_rows=state_rows, state_col0=col_k, state_cols=col_gr - col_k,
            xr_col0=col_xr, gr_col0=col_gr, name=f"in_proj_rglru_{tag}")
        z3 = z2.reshape(b, t, n_in)
        offsets = [i * CHUNK for i in range(PAST_CHUNKS + 2)]
        o_a = _attention(z3, z3, z3, wts["rel_bias"], q_col0=0, k_col0=col_k, v_col0=col_v, bb=1,
                         heads=heads, offsets=offsets, name=f"attn_{tag}")
    else:
        z2, state = _in_proj(x2d, wts["pre_mix_g"], wts["w_in"], tm=tm, tn=tn, period=period,
                             state_rows=state_rows, state_col0=col_k, state_cols=col_gr - col_k,
                             name=f"in_proj_{tag}")
        z3 = z2.reshape(b, t, n_in)
        n_cached = cache_k.shape[1]
        assert t == CHUNK and n_cached == KV_REACH
        o_a = _attention(z3, z3, z3, wts["rel_bias"], q_col0=0, k_col0=col_k, v_col0=col_v,
                         bb=_pick_tile(b, SAMPLE_SEQS), heads=heads, offsets=[n_cached], name=f"attn_{tag}",
                         k_cache=cache_k.astype(BF16).reshape(b, n_cached, D_ATTN),
                         v_cache=cache_v.astype(BF16).reshape(b, n_cached, D_ATTN))
        conv0 = jnp.pad(conv_state.astype(F32), ((0, 0), (SUBLANES - (CONV_WIDTH - 1), 0), (0, 0)))
        h0 = h_state.astype(F32).reshape(b, 1, d_rnn)
        o_b, h_last = _rglru(z3, conv0, h0, *rnn_w, xr_col0=col_xr, gr_col0=col_gr,
                             tt=_pick_tile(t, RNN_FRAMES), cw=COL_TILE, name=f"rglru_{tag}")
    state = state.reshape(b, keep, -1)
    new_k = state[:, :, :D_ATTN].reshape(b, keep, N_HEADS, HEAD_DIM)
    new_v = state[:, :, D_ATTN:2 * D_ATTN].reshape(b, keep, N_HEADS, HEAD_DIM)
    conv_tail = state[:, keep - (CONV_WIDTH - 1):, 2 * D_ATTN:col_gr - col_k]

    merged = _gate_merge(o_a.reshape(m, D_ATTN), o_b.reshape(m, d_rnn), z2, wts["w_attn_up"],
                         wts["w_rnn_up"], ga_col0=col_ga, gb_col0=col_gb, tm=tm, tn=tn,
                         name=f"gate_merge_{tag}")
    tm2 = _pick_tile(m, WIDE_ROW_TILE)
    x1 = _out_norm(merged, wts["w_out"], x2d, wts["post_mix_g"], tm=tm2, name=f"out_norm_{tag}")
    y = _ffn(x1, wts["pre_ffn_g"], wts["w_ff1"], wts["w_ff2"], wts["post_ffn_g"], tm=tm2, tf=FFN_TILE,
             name=f"ffn_{tag}")
    return y.reshape(b, t, d), new_k, new_v, conv_tail, h_last.reshape(b, d_rnn)


def kernel(x_prompt, x_sample, cache_k, cache_v, state_conv, state_h, pre_mix_g, w_in, rel_bias, conv_w, conv_b, w_rg, b_rg, w_ig, b_ig, lru_lambda, w_attn_up, w_rnn_up, w_out, post_mix_g, pre_ffn_g, w_ff1, w_ff2, post_ffn_g):
    depth = w_in.shape[0]
    y_p, y_s = x_prompt, x_sample
    outs_p, outs_s = [], []
    for l in range(depth):
        wts = {
            "pre_mix_g": pre_mix_g[l][None], "post_mix_g": post_mix_g[l][None],
            "pre_ffn_g": pre_ffn_g[l][None], "post_ffn_g": post_ffn_g[l][None],
            "w_in": w_in[l].astype(BF16), "rel_bias": rel_bias[l],
            "conv_w": conv_w[l], "conv_b": conv_b[l][None],
            "w_gates": jnp.concatenate([w_rg[l], w_ig[l]], axis=-1).astype(BF16),
            "b_rg": b_rg[l][None], "b_ig": b_ig[l][None], "lru_lambda": lru_lambda[l][None],
            "w_attn_up": w_attn_up[l].astype(BF16), "w_rnn_up": w_rnn_up[l].astype(BF16),
            "w_out": w_out[l].astype(BF16), "w_ff1": w_ff1[l].astype(BF16), "w_ff2": w_ff2[l].astype(BF16),
        }
        y_p, *st_p = _layer(y_p, None, None, None, None, wts, f"p{l}")
        y_s, *st_s = _layer(y_s, cache_k[l], cache_v[l], state_conv[l], state_h[l], wts, f"s{l}")
        outs_p.append(st_p)
        outs_s.append(st_s)
    stack = lambda outs, i: jnp.stack([o[i] for o in outs])
    return (y_p, y_s,
            stack(outs_p, 0), stack(outs_p, 1), stack(outs_p, 2), stack(outs_p, 3),
            stack(outs_s, 0), stack(outs_s, 1), stack(outs_s, 2), stack(outs_s, 3))
```

```python
import functools

import jax
import jax.numpy as jnp
from jax import lax
from jax.experimental import pallas as pl
from jax.experimental.pallas import tpu as pltpu

F32 = jnp.float32
BF16 = jnp.bfloat16

CHUNK = 64
PAST_CHUNKS = 8
KV_REACH = PAST_CHUNKS * CHUNK
N_HEADS = 16
HEAD_DIM = 64
D_ATTN = N_HEADS * HEAD_DIM
MAX_REL = 256
RNN_BLOCK = 128
CONV_WIDTH = 4
LRU_C = 8.0
EPS = 1e-6
NEG_INF = -1e30

LANES = 128
SUBLANES = 8
MIB = 1024 * 1024

ROW_TILE = 1024
COL_TILE = 1024
WIDE_ROW_TILE = 512
FFN_TILE = 1024
RNN_FRAMES = 256
HEADS_PER_STEP = 2
SAMPLE_SEQS = 8
KT_BLOCK = 512
ROW_CHUNKS = 4
VMEM_PROJ, VMEM_FUSED, VMEM_ATTN, VMEM_RNN, VMEM_MLP = 52, 58, 40, 40, 48


def _params(n_axes, vmem_mib):
    return pltpu.CompilerParams(
        dimension_semantics=("arbitrary",) * n_axes,
        vmem_limit_bytes=vmem_mib * MIB,
    )


def _rmsnorm_rows(x, g):
    ms = jnp.mean(x * x, axis=-1, keepdims=True)
    return x * lax.rsqrt(ms + EPS) * g


def _in_proj_body(x_ref, g_ref, w_ref, o_ref, s_ref, xn_ref, *, period, j0, j1):
    i, j = pl.program_id(0), pl.program_id(1)

    @pl.when(j == 0)
    def _():
        xn_ref[...] = _rmsnorm_rows(x_ref[...], g_ref[...]).astype(BF16)

    keep_f32 = (i % period == period - 1) & (j >= j0) & (j < j1)

    @pl.when(keep_f32)
    def _():
        acc = jnp.dot(xn_ref[...], w_ref[...], preferred_element_type=F32)
        o_ref[...] = acc.astype(o_ref.dtype)
        s_ref[...] = acc[acc.shape[0] - s_ref.shape[0]:, :]

    @pl.when(jnp.logical_not(keep_f32))
    def _():
        o_ref[...] = jnp.dot(xn_ref[...], w_ref[...], preferred_element_type=F32).astype(o_ref.dtype)


def _in_proj(x2d, g, w, *, tm, tn, period, state_rows, state_col0, state_cols, name):
    m, d = x2d.shape
    n = w.shape[1]
    j0, j1 = state_col0 // tn, (state_col0 + state_cols) // tn
    n_state_tiles = m // tm // period

    def state_block(i, j):
        live = i % period == period - 1
        return i // period, jnp.where(live, jnp.clip(j - j0, 0, j1 - j0 - 1), 0)

    return pl.pallas_call(
        functools.partial(_in_proj_body, period=period, j0=j0, j1=j1),
        grid=(m // tm, n // tn),
        in_specs=[
            pl.BlockSpec((tm, d), lambda i, j: (i, 0)),
            pl.BlockSpec((1, d), lambda i, j: (0, 0)),
            pl.BlockSpec((d, tn), lambda i, j: (0, j)),
        ],
        out_specs=[
            pl.BlockSpec((tm, tn), lambda i, j: (i, j)),
            pl.BlockSpec((state_rows, tn), state_block),
        ],
        out_shape=[
            jax.ShapeDtypeStruct((m, n), BF16),
            jax.ShapeDtypeStruct((n_state_tiles * state_rows, state_cols), F32),
        ],
        scratch_shapes=[pltpu.VMEM((tm, d), BF16)],
        compiler_params=_params(2, VMEM_PROJ),
        name=name,
    )(x2d, g, w)


def _bias_diagonals(rel_bias, width, offsets, heads):
    x = jnp.arange(width)
    rows = [rel_bias[:, jnp.clip(off + CHUNK - 1 - x, -MAX_REL, MAX_REL) + MAX_REL] for off in offsets]
    g = jnp.stack(rows, axis=1).astype(F32)
    n_groups = rel_bias.shape[0] // heads
    g = g.reshape(n_groups, heads, len(offsets), width)
    return jnp.swapaxes(g, 1, 2).reshape(n_groups, len(offsets) * heads, width)


def _attn_body(*refs, bb, n_chunks, kw, heads, offsets, unit, cached):
    if cached:
        q_ref, kc_ref, vc_ref, k_ref, v_ref, g_ref, o_ref, tab_ref, s_ref, p_ref, l_ref = refs
        band = lambda c_ref, n_ref, bi, ks: jnp.concatenate([c_ref[bi], n_ref[bi]], axis=0)
    else:
        q_ref, k_ref, v_ref, g_ref, o_ref, tab_ref, s_ref, p_ref, l_ref, kt_ref = refs
        kc_ref = vc_ref = None
        band = lambda c_ref, n_ref, bi, ks: n_ref[bi, pl.ds(ks, kw), :]
    width = g_ref.shape[2]
    hw = heads * HEAD_DIM

    @pl.when(pl.program_id(1) == 0)
    def _():
        r = lax.broadcasted_iota(jnp.int32, (CHUNK, kw), 0)
        j = lax.broadcasted_iota(jnp.int32, (CHUNK, kw), 1)
        for c, off in enumerate(offsets):
            cq = (r + off) // CHUNK
            ck = j // CHUNK
            valid = (ck <= cq) & (ck >= cq - PAST_CHUNKS)
            for h in range(heads):
                g = jnp.broadcast_to(g_ref[0, c * heads + h:c * heads + h + 1, :], (CHUNK, width))
                t = pltpu.roll(g, width - (CHUNK - 1), 1, stride=1, stride_axis=0)
                tab_ref[c, h * CHUNK:(h + 1) * CHUNK, :] = jnp.where(valid, t[:, :kw], NEG_INF)

    if not cached:
        for r0 in range(0, k_ref.shape[1], KT_BLOCK):
            kt_ref[:, r0:r0 + KT_BLOCK] = k_ref[0, r0:r0 + KT_BLOCK, :].T

    lane = lax.broadcasted_iota(jnp.int32, (1, hw), 1)
    head_lanes = [(lane >= h * HEAD_DIM) & (lane < (h + 1) * HEAD_DIM) for h in range(heads)]
    rows = heads * CHUNK
    n_units = bb * n_chunks // unit

    def locate(u, j):
        idx = u * unit + j
        bi, c = (0, idx) if bb == 1 else (idx // n_chunks, idx % n_chunks)
        q0 = pl.multiple_of(c * CHUNK, CHUNK)
        if cached:
            return bi, 0, q0, 0
        past = jnp.maximum(c - PAST_CHUNKS, 0)
        odd = past % 2
        ks = pl.multiple_of((past - odd) * CHUNK, 2 * CHUNK)
        return bi, jnp.minimum(c, PAST_CHUNKS) + odd, q0, ks

    def scores(u, slot):
        for j in range(unit):
            bi, c, q0, ks = locate(u, j)
            q = q_ref[bi, pl.ds(q0, CHUNK), :]
            qs = jnp.concatenate([jnp.where(m, q, jnp.zeros_like(q)) for m in head_lanes], axis=0)
            qs = qs * jnp.asarray(HEAD_DIM ** -0.5, q.dtype)
            if cached:
                kb = band(kc_ref, k_ref, bi, ks)
                s = lax.dot_general(qs, kb, (((1,), (1,)), ((), ())), preferred_element_type=F32)
            else:
                s = jnp.dot(qs, kt_ref[:, pl.ds(ks, kw)], preferred_element_type=F32)
            s_ref[slot, j * rows:(j + 1) * rows, :] = s + tab_ref[c]

    def softmax(slot):
        s = s_ref[slot]
        p = jnp.exp(s - jnp.max(s, axis=-1, keepdims=True))
        p_ref[slot] = p.astype(p_ref.dtype)
        l_ref[slot] = 1.0 / jnp.sum(p, axis=-1, keepdims=True)

    def output(u, slot):
        for j in range(unit):
            bi, c, q0, ks = locate(u, j)
            vb = band(vc_ref, v_ref, bi, ks)
            o_all = jnp.dot(p_ref[slot, j * rows:(j + 1) * rows, :], vb, preferred_element_type=F32)
            o_all = o_all * l_ref[slot, j * rows:(j + 1) * rows, :]
            o = o_all[0:CHUNK]
            for h in range(1, heads):
                o = jnp.where(head_lanes[h], o_all[h * CHUNK:(h + 1) * CHUNK], o)
            o_ref[bi, pl.ds(q0, CHUNK), :] = o.astype(o_ref.dtype)

    def step(i, slot, do_scores, do_softmax, do_output):
        if do_scores:
            scores(i, slot)
        if do_output:
            output(i - 2, slot)
        if do_softmax:
            softmax(1 - slot)

    for i in range(2):
        step(i, i % 2, i < n_units, 1 <= i <= n_units, False)
    n_pairs = max(n_units - 2, 0) // 2
    if n_pairs:
        def body(t, carry):
            i = 2 + 2 * t
            step(i, 0, True, True, True)
            step(i + 1, 1, True, True, True)
            return carry
        lax.fori_loop(0, n_pairs, body, 0)
    for i in range(2 + 2 * n_pairs, n_units + 2):
        step(i, i % 2, i < n_units, 1 <= i <= n_units, i >= 2)


def _attention(q_arr, k_arr, v_arr, rel_bias, *, q_col0, k_col0, v_col0, bb, heads, offsets, name,
               k_cache=None, v_cache=None):
    b, sq, _ = q_arr.shape
    sk = k_arr.shape[1]
    cached = k_cache is not None
    if cached:
        kw = KV_REACH + CHUNK
        assert sq == CHUNK and sk == CHUNK and k_cache.shape[1] == KV_REACH and len(offsets) == 1
    else:
        kw = KV_REACH + 2 * CHUNK
        assert bb == 1 and sq == sk and sk % KT_BLOCK == 0 and sk >= kw and len(offsets) == PAST_CHUNKS + 2
    hw = heads * HEAD_DIM
    n_groups = N_HEADS // heads
    n_case = len(offsets)
    width = -(-(kw + CHUNK - 1) // LANES) * LANES
    diag = _bias_diagonals(rel_bias, width, offsets, heads)
    qc, kc, vc = q_col0 // hw, k_col0 // hw, v_col0 // hw
    unit = 2
    assert (bb * sq // CHUNK) % unit == 0
    rows = unit * heads * CHUNK
    cache_specs = [pl.BlockSpec((bb, KV_REACH, hw), lambda g, i: (i, 0, g))] * 2 if cached else []
    cache_args = [k_cache, v_cache] if cached else []
    return pl.pallas_call(
        functools.partial(_attn_body, bb=bb, n_chunks=sq // CHUNK, kw=kw, heads=heads,
                          offsets=tuple(offsets), unit=unit, cached=cached),
        grid=(n_groups, b // bb),
        in_specs=[pl.BlockSpec((bb, sq, hw), lambda g, i: (i, 0, qc + g))] + cache_specs + [
            pl.BlockSpec((bb, sk, hw), lambda g, i: (i, 0, kc + g)),
            pl.BlockSpec((bb, sk, hw), lambda g, i: (i, 0, vc + g)),
            pl.BlockSpec((1, n_case * heads, width), lambda g, i: (g, 0, 0)),
        ],
        out_specs=pl.BlockSpec((bb, sq, hw), lambda g, i: (i, 0, g)),
        out_shape=jax.ShapeDtypeStruct((b, sq, D_ATTN), BF16),
        scratch_shapes=[
            pltpu.VMEM((n_case, heads * CHUNK, kw), F32),
            pltpu.VMEM((2, rows, kw), F32),
            pltpu.VMEM((2, rows, kw), BF16),
            pltpu.VMEM((2, rows, 1), F32),
        ] + ([] if cached else [pltpu.VMEM((hw, sk), BF16)]),
        compiler_params=_params(2, VMEM_ATTN),
        name=name,
    )(q_arr, *cache_args, k_arr, v_arr, diag)


def _log_sigmoid(x):
    return jnp.minimum(x, 0.0) - jnp.log1p(jnp.exp(-jnp.abs(x)))


def _shift_matrix(tt):
    t = jnp.arange(tt)
    blocks = [(t[:, None] - d) == t[None, :] for d in range(CONV_WIDTH - 1, 0, -1)]
    return jnp.concatenate(blocks, axis=0).astype(BF16)


def _rglru_head(x_bf, prev8, shift, conv_w, conv_b, wg_ref, b_rg, b_ig):
    tt, cw = x_bf.shape
    taps = CONV_WIDTH - 1
    x = x_bf.astype(F32)
    row = lax.broadcasted_iota(jnp.int32, (SUBLANES, cw), 0)
    if shift is not None:
        sh = jnp.dot(shift, x_bf, preferred_element_type=F32)
        xc = conv_b + conv_w[0:1] * sh[0:tt]
        for k in range(1, taps):
            xc = xc + conv_w[k:k + 1] * sh[k * tt:(k + 1) * tt]
        xc = xc + conv_w[taps:taps + 1] * x
        head = xc[0:SUBLANES]
        for k in range(taps):
            d = taps - k
            head = head + conv_w[k:k + 1] * jnp.where(row < d, pltpu.roll(prev8, d, 0), 0.0)
        xc = jnp.concatenate([head, xc[SUBLANES:]], axis=0)
    else:
        xc = conv_b
        for k in range(taps):
            d = taps - k
            sh = pltpu.roll(x, d, 0)
            head = jnp.where(row < d, pltpu.roll(prev8, d, 0), sh[0:SUBLANES])
            xc = xc + conv_w[k:k + 1] * jnp.concatenate([head, sh[SUBLANES:]], axis=0)
        xc = xc + conv_w[taps:taps + 1] * x

    xcb = xc.astype(BF16)
    r_parts, i_parts = [], []
    for n in range(cw // RNN_BLOCK):
        gts = jnp.dot(xcb[:, n * RNN_BLOCK:(n + 1) * RNN_BLOCK], wg_ref[n], preferred_element_type=F32)
        r_parts.append(gts[:, :RNN_BLOCK])
        i_parts.append(gts[:, RNN_BLOCK:])
    r_pre = jnp.concatenate(r_parts, axis=1) + b_rg
    i_pre = jnp.concatenate(i_parts, axis=1) + b_ig
    return xc, r_pre, i_pre, x[tt - SUBLANES:tt]


def _rglru_tail(xc, r_pre, i_pre, gr_bf, h_in, lam):
    rows, cw = xc.shape
    r = jax.nn.sigmoid(r_pre)
    i = jax.nn.sigmoid(i_pre)
    log_a = r * (LRU_C * _log_sigmoid(lam))
    a = jnp.exp(log_a)
    w = -jnp.tanh(log_a)
    one_minus_a2 = 2.0 * w / (1.0 + w)
    root = jnp.where(one_minus_a2 > 0.0, one_minus_a2 * lax.rsqrt(one_minus_a2), 0.0)
    u = root * (i * xc)

    row = lax.broadcasted_iota(jnp.int32, (SUBLANES, cw), 0)
    h = h_in
    hs = []
    for g in range(rows // SUBLANES):
        ag = a[g * SUBLANES:(g + 1) * SUBLANES]
        ug = u[g * SUBLANES:(g + 1) * SUBLANES]
        for d in (1, 2, 4):
            keep = row >= d
            a_prev = jnp.where(keep, pltpu.roll(ag, d, 0), 1.0)
            u_prev = jnp.where(keep, pltpu.roll(ug, d, 0), 0.0)
            ug = ag * u_prev + ug
            ag = ag * a_prev
        hg = ug + ag * h
        hs.append(hg)
        h = jnp.broadcast_to(hg[SUBLANES - 1:SUBLANES, :], (SUBLANES, cw))
    gate = jax.nn.gelu(gr_bf.astype(F32), approximate=True)
    return (jnp.concatenate(hs, axis=0) * gate).astype(BF16), h


def _rglru_body(xr_ref, gr_ref, c0_ref, h0_ref, sh_ref, cw_ref, cb_ref, wg_ref, brg_ref, big_ref, lam_ref,
                ob_ref, hl_ref, prev_ref, hcar_ref):
    @pl.when(pl.program_id(2) == 0)
    def _():
        hcar_ref[...] = jnp.broadcast_to(h0_ref[0], hcar_ref.shape)
        prev_ref[...] = c0_ref[0]

    xc, r_pre, i_pre, prev8 = _rglru_head(xr_ref[0], prev_ref[...], sh_ref[...], cw_ref[...], cb_ref[...],
                                          wg_ref, brg_ref[...], big_ref[...])
    out, h = _rglru_tail(xc, r_pre, i_pre, gr_ref[0], hcar_ref[...], lam_ref[...])
    ob_ref[0] = out
    prev_ref[...] = prev8
    hcar_ref[...] = h
    hl_ref[0] = h[0:1, :]


def _rglru(z3, conv0, h0, conv_w, conv_b, w_gates, b_rg, b_ig, lam, *, xr_col0, gr_col0, tt, cw, name):
    b, t, _ = z3.shape
    d_rnn = conv_w.shape[1]
    n_cg = d_rnn // cw
    blocks_per_cg = cw // RNN_BLOCK
    xc0, gc0 = xr_col0 // cw, gr_col0 // cw
    vec = pl.BlockSpec((1, cw), lambda c, i, s: (0, c))
    return pl.pallas_call(
        _rglru_body,
        grid=(n_cg, b, t // tt),
        in_specs=[
            pl.BlockSpec((1, tt, cw), lambda c, i, s: (i, s, xc0 + c)),
            pl.BlockSpec((1, tt, cw), lambda c, i, s: (i, s, gc0 + c)),
            pl.BlockSpec((1, SUBLANES, cw), lambda c, i, s: (i, 0, c)),
            pl.BlockSpec((1, 1, cw), lambda c, i, s: (i, 0, c)),
            pl.BlockSpec(((CONV_WIDTH - 1) * tt, tt), lambda c, i, s: (0, 0)),
            pl.BlockSpec((CONV_WIDTH, cw), lambda c, i, s: (0, c)),
            vec,
            pl.BlockSpec((blocks_per_cg, RNN_BLOCK, 2 * RNN_BLOCK), lambda c, i, s: (c, 0, 0)),
            vec, vec, vec,
        ],
        out_specs=[
            pl.BlockSpec((1, tt, cw), lambda c, i, s: (i, s, c)),
            pl.BlockSpec((1, 1, cw), lambda c, i, s: (i, 0, c)),
        ],
        out_shape=[
            jax.ShapeDtypeStruct((b, t, d_rnn), BF16),
            jax.ShapeDtypeStruct((b, 1, d_rnn), F32),
        ],
        scratch_shapes=[
            pltpu.VMEM((SUBLANES, cw), F32),
            pltpu.VMEM((SUBLANES, cw), F32),
        ],
        compiler_params=_params(3, VMEM_RNN),
        name=name,
    )(z3, z3, conv0, h0, _shift_matrix(tt), conv_w, conv_b, w_gates, b_rg, b_ig, lam)


def _lookup(table, j):
    out = jnp.int32(table[-1])
    for k in range(len(table) - 2, -1, -1):
        out = jnp.where(j == k, jnp.int32(table[k]), out)
    return out


def _fused_body(x_ref, g_ref, w_ref, c0_ref, h0_ref, cw_ref, cb_ref, wg_ref, brg_ref, big_ref,
                lam_ref, z_ref, s_ref, ob_ref, hl_ref, xn_ref, stash_ref, prev_ref, hcar_ref, hd_ref, gr_ref, *,
                n_tiles, period, n_cg, pieces_per_cg, tt, n_split, stash_slots):
    i, j = pl.program_id(0), pl.program_id(1)
    cw = ob_ref.shape[1]

    @pl.when((j == 0) & (i < n_tiles))
    def _():
        xn_ref[...] = _rmsnorm_rows(x_ref[...], g_ref[...]).astype(BF16)

    @pl.when((j == 0) & (i == 0))
    def _():
        prev_ref[...] = jnp.zeros_like(prev_ref)
        hcar_ref[...] = jnp.zeros_like(hcar_ref)

    tm = xn_ref.shape[0]
    rows_s = tt // n_split
    state_start = tm - s_ref.shape[0]
    first_rows = tm // 4
    rest_rows = (tm - first_rows) // n_split
    bounds = [0] + [first_rows + c * rest_rows for c in range(n_split + 1)]

    def project(c):
        lo, hi = bounds[c], bounds[c + 1]
        acc = jnp.dot(xn_ref[lo:hi, :], w_ref[...], preferred_element_type=F32)
        zb = acc.astype(BF16)
        z_ref[lo:hi, :] = zb
        if hi > state_start:
            keep = max(lo, state_start)
            s_ref[keep - state_start:hi - state_start, :] = acc[keep - lo:, :]
        stash_ref[_lookup(stash_slots, j), lo:hi, :] = zb
        return acc[hi - lo - SUBLANES:, :]

    def after(state, acc_rows):
        cg, lam, h = state
        zero = (pltpu.bitcast(acc_rows, jnp.uint32) >> 16) >> 16
        return cg, lam, pltpu.bitcast(pltpu.bitcast(h, jnp.uint32) + zero, F32)

    def recur_load():
        cg, tp = j // pieces_per_cg, j % pieces_per_cg
        r0 = pl.multiple_of(tp * tt, tt)
        first = ((i - 1) % period == 0) & (tp == 0)
        prev8 = jnp.where(first, c0_ref[0, cg], prev_ref[cg])
        h = jnp.where(first, jnp.broadcast_to(h0_ref[0, cg], (SUBLANES, cw)), hcar_ref[cg])
        gr_ref[...] = stash_ref[n_cg + cg, pl.ds(r0, tt), :]
        return cg, stash_ref[cg, pl.ds(r0, tt), :], prev8, h

    def recur_head(loaded):
        cg, x_bf, prev8, h = loaded
        xc, r_pre, i_pre, prev8 = _rglru_head(x_bf, prev8, None, cw_ref[cg], cb_ref[cg], wg_ref.at[cg],
                                              brg_ref[cg], big_ref[cg])
        hd_ref[0] = xc
        hd_ref[1] = r_pre
        hd_ref[2] = i_pre
        prev_ref[cg] = prev8
        return cg, lam_ref[cg], h

    def recur_tail(k, state):
        cg, lam, h = state
        lo = k * rows_s
        out, h = _rglru_tail(hd_ref[0, lo:lo + rows_s, :], hd_ref[1, lo:lo + rows_s, :],
                             hd_ref[2, lo:lo + rows_s, :], gr_ref[lo:lo + rows_s, :], h, lam)
        ob_ref[lo:lo + rows_s, :] = out
        return cg, lam, h

    def recur_end(state):
        cg, _, h = state
        hcar_ref[cg] = h
        hl_ref[0, 0] = h[0:1, :]

    has_proj = i < n_tiles
    has_rec = (i >= 1) & (j < n_cg * pieces_per_cg)

    @pl.when(has_proj & has_rec)
    def _():
        state = after(recur_head(recur_load()), project(0))
        for k in range(n_split):
            state = recur_tail(k, state)
            acc_rows = project(k + 1)
            if k + 1 < n_split:
                state = after(state, acc_rows)
        recur_end(state)

    @pl.when(has_proj & jnp.logical_not(has_rec))
    def _():
        for c in range(n_split + 1):
            project(c)

    @pl.when(jnp.logical_not(has_proj) & has_rec)
    def _():
        state = recur_head(recur_load())
        for k in range(n_split):
            state = recur_tail(k, state)
        recur_end(state)


def _in_proj_rglru(x2d, g, w, conv0, h0, conv_w, conv_b, w_gates, b_rg, b_ig, lam, *, tm, tn, tt, period,
                   state_rows, state_col0, state_cols, xr_col0, gr_col0, name):
    m, d = x2d.shape
    n = w.shape[1]
    d_rnn = conv_w.shape[1]
    n_tiles, n_col, n_cg = m // tm, n // tn, d_rnn // tn
    n_seq = n_tiles // period
    pieces_per_cg = tm // tt
    n_pieces = n_cg * pieces_per_cg
    xr0, gr0 = xr_col0 // tn, gr_col0 // tn
    assert n_pieces + n_cg <= n_col and tm % tt == 0 and n_tiles % period == 0

    order, stash_slots = [None] * n_col, [2 * n_cg] * n_col
    for c in range(n_cg):
        jx = (c + 1) * pieces_per_cg
        order[jx], stash_slots[jx] = xr0 + c, c
        order[jx + 1], stash_slots[jx + 1] = gr0 + c, n_cg + c
    rest = [c for c in range(n_col) if c not in order]
    for jj in range(n_col):
        if order[jj] is None:
            order[jj] = rest.pop(0)
    s0, n_state = state_col0 // tn, state_cols // tn
    state_pos = [order.index(s0 + k) for k in range(n_state)]
    assert state_pos == sorted(state_pos)
    state_blk = [sum(p < jj for p in state_pos) for jj in range(n_col)]

    last = lambda i: i == n_tiles
    row = lambda i: jnp.minimum(i, n_tiles - 1)
    col = lambda i, j: jnp.where(last(i), order[-1], _lookup(order, j))
    seq_prev = lambda i: jnp.maximum(i - 1, 0) // period
    piece = lambda i, j: jnp.where(i == 0, 0, jnp.minimum(j, n_pieces - 1))

    def state_block(i, j):
        live = i % period == period - 1
        blk = jnp.where(live, _lookup(state_blk, j), n_state + 1)
        return row(i) // period, jnp.where(last(i), n_state, blk)

    per_cg = lambda a: a.reshape(a.shape[0], n_cg, tn).swapaxes(0, 1)
    whole = lambda a: pl.BlockSpec(a.shape, lambda i, j: (0,) * a.ndim)
    n_split = 4
    assert tt % (n_split * SUBLANES) == 0 and (tm - tm // 4) % (2 * SUBLANES * n_split) == 0
    params = [per_cg(conv_w), per_cg(conv_b), w_gates.reshape(n_cg, -1, RNN_BLOCK, 2 * RNN_BLOCK),
              per_cg(b_rg), per_cg(b_ig), per_cg(lam)]
    conv0 = conv0.reshape(n_seq, SUBLANES, n_cg, tn).swapaxes(1, 2)
    h0 = h0.reshape(n_seq, 1, n_cg, tn).swapaxes(1, 2)

    z, state, o_b, h_last = pl.pallas_call(
        functools.partial(_fused_body, n_tiles=n_tiles, period=period, n_cg=n_cg,
                          pieces_per_cg=pieces_per_cg, tt=tt, n_split=n_split,
                          stash_slots=tuple(stash_slots)),
        grid=(n_tiles + 1, n_col),
        in_specs=[
            pl.BlockSpec((tm, d), lambda i, j: (row(i), 0)),
            pl.BlockSpec((1, d), lambda i, j: (0, 0)),
            pl.BlockSpec((d, tn), lambda i, j: (0, col(i, j))),
            pl.BlockSpec((1, n_cg, SUBLANES, tn), lambda i, j: (seq_prev(i), 0, 0, 0)),
            pl.BlockSpec((1, n_cg, 1, tn), lambda i, j: (seq_prev(i), 0, 0, 0)),
        ] + [whole(p) for p in params],
        out_specs=[
            pl.BlockSpec((tm, tn), lambda i, j: (row(i), col(i, j))),
            pl.BlockSpec((state_rows, tn), state_block),
            pl.BlockSpec((tt, tn), lambda i, j: (jnp.maximum(i - 1, 0) * pieces_per_cg
                                                 + piece(i, j) % pieces_per_cg, piece(i, j) // pieces_per_cg)),
            pl.BlockSpec((1, 1, 1, tn), lambda i, j: (jnp.maximum(i - 1, 0), piece(i, j) // pieces_per_cg, 0, 0)),
        ],
        out_shape=[
            jax.ShapeDtypeStruct((m, n), BF16),
            jax.ShapeDtypeStruct((n_seq * state_rows, (n_state + 2) * tn), F32),
            jax.ShapeDtypeStruct((m, d_rnn), BF16),
            jax.ShapeDtypeStruct((n_tiles, n_cg, 1, tn), F32),
        ],
        scratch_shapes=[
            pltpu.VMEM((tm, d), BF16),
            pltpu.VMEM((2 * n_cg + 1, tm, tn), BF16),
            pltpu.VMEM((n_cg, SUBLANES, tn), F32),
            pltpu.VMEM((n_cg, SUBLANES, tn), F32),
            pltpu.VMEM((3, tt, tn), F32),
            pltpu.VMEM((tt, tn), BF16),
        ],
        compiler_params=_params(2, VMEM_FUSED),
        name=name,
    )(x2d, g, w, conv0, h0, *params)
    return z, state, o_b, h_last.reshape(n_tiles, d_rnn)[period - 1::period]


def _merge_body(oa_ref, ob_ref, ga_ref, gb_ref, wa_ref, wb_ref, o_ref):
    tm = o_ref.shape[0]
    for c in range(ROW_CHUNKS):
        lo, hi = c * tm // ROW_CHUNKS, (c + 1) * tm // ROW_CHUNKS
        ya = jnp.dot(oa_ref[lo:hi, :], wa_ref[...], preferred_element_type=F32)
        yb = jnp.dot(ob_ref[lo:hi, :], wb_ref[...], preferred_element_type=F32)
        ga = jax.nn.sigmoid(ga_ref[lo:hi, :].astype(F32))
        gb = jax.nn.sigmoid(gb_ref[lo:hi, :].astype(F32))
        o_ref[lo:hi, :] = (ga * ya + gb * yb).astype(o_ref.dtype)


def _gate_merge(o_a, o_b, z2, w_a, w_b, *, ga_col0, gb_col0, tm, tn, name):
    m = o_a.shape[0]
    d = w_a.shape[1]
    ga0, gb0 = ga_col0 // tn, gb_col0 // tn
    return pl.pallas_call(
        _merge_body,
        grid=(m // tm, d // tn),
        in_specs=[
            pl.BlockSpec((tm, o_a.shape[1]), lambda i, j: (i, 0)),
            pl.BlockSpec((tm, o_b.shape[1]), lambda i, j: (i, 0)),
            pl.BlockSpec((tm, tn), lambda i, j: (i, ga0 + j)),
            pl.BlockSpec((tm, tn), lambda i, j: (i, gb0 + j)),
            pl.BlockSpec((w_a.shape[0], tn), lambda i, j: (0, j)),
            pl.BlockSpec((w_b.shape[0], tn), lambda i, j: (0, j)),
        ],
        out_specs=pl.BlockSpec((tm, tn), lambda i, j: (i, j)),
        out_shape=jax.ShapeDtypeStruct((m, d), BF16),
        compiler_params=_params(2, VMEM_MLP),
        name=name,
    )(o_a, o_b, z2, z2, w_a, w_b)


def _out_norm_body(m_ref, w_ref, x_ref, g_ref, o_ref):
    y = jnp.dot(m_ref[...], w_ref[...], preferred_element_type=F32)
    o_ref[...] = x_ref[...] + _rmsnorm_rows(y, g_ref[...])


def _out_norm(merged, w_out, x2d, g, *, tm, name):
    m, d = x2d.shape
    return pl.pallas_call(
        _out_norm_body,
        grid=(m // tm,),
        in_specs=[
            pl.BlockSpec((tm, d), lambda i: (i, 0)),
            pl.BlockSpec((d, d), lambda i: (0, 0)),
            pl.BlockSpec((tm, d), lambda i: (i, 0)),
            pl.BlockSpec((1, d), lambda i: (0, 0)),
        ],
        out_specs=pl.BlockSpec((tm, d), lambda i: (i, 0)),
        out_shape=jax.ShapeDtypeStruct((m, d), F32),
        compiler_params=_params(1, VMEM_MLP),
        name=name,
    )(merged, w_out, x2d, g)


def _ffn_body(x_ref, g1_ref, w1_ref, w2_ref, g2_ref, o_ref, xn_ref, *, n_steps, edge_chunks):
    f = pl.program_id(1)

    def rows_chunk(lo, hi, first, last):
        if first:
            xn = _rmsnorm_rows(x_ref[lo:hi, :], g1_ref[...]).astype(BF16)
            xn_ref[lo:hi, :] = xn
        else:
            xn = xn_ref[lo:hi, :]
        hid = jnp.dot(xn, w1_ref[...], preferred_element_type=F32)
        hid = jnp.square(jnp.maximum(hid, 0.0)).astype(BF16)
        acc = jnp.dot(hid, w2_ref[...], preferred_element_type=F32)
        if not first:
            acc = o_ref[lo:hi, :] + acc
        if last:
            acc = x_ref[lo:hi, :] + _rmsnorm_rows(acc, g2_ref[...])
        o_ref[lo:hi, :] = acc

    tm = o_ref.shape[0]
    for first, last in sorted({(k == 0, k == n_steps - 1) for k in range(n_steps)}):
        chunks = edge_chunks if (first or last) else 1

        @pl.when(((f == 0) == first) & ((f == n_steps - 1) == last))
        def _(first=first, last=last, chunks=chunks):
            for c in range(chunks):
                rows_chunk(c * tm // chunks, (c + 1) * tm // chunks, first, last)


def _ffn(x2d, g1, w1, w2, g2, *, tm, tf, name):
    m, d = x2d.shape
    d_ff = w1.shape[1]
    return pl.pallas_call(
        functools.partial(_ffn_body, n_steps=d_ff // tf, edge_chunks=ROW_CHUNKS),
        grid=(m // tm, d_ff // tf),
        in_specs=[
            pl.BlockSpec((tm, d), lambda i, f: (i, 0)),
            pl.BlockSpec((1, d), lambda i, f: (0, 0)),
            pl.BlockSpec((d, tf), lambda i, f: (0, f)),
            pl.BlockSpec((tf, d), lambda i, f: (f, 0)),
            pl.BlockSpec((1, d), lambda i, f: (0, 0)),
        ],
        out_specs=pl.BlockSpec((tm, d), lambda i, f: (i, 0)),
        out_shape=jax.ShapeDtypeStruct((m, d), F32),
        scratch_shapes=[pltpu.VMEM((tm, d), BF16)],
        compiler_params=_params(2, VMEM_MLP),
        name=name,
    )(x2d, g1, w1, w2, g2)


def _pick_tile(n, target):
    t = min(n, target)
    while n % t:
        t //= 2
    return t


def _layer(x, cache_k, cache_v, conv_state, h_state, wts, tag):
    b, t, d = x.shape
    m = b * t
    x2d = x.reshape(m, d)
    d_rnn = wts["conv_w"].shape[1]
    n_in = wts["w_in"].shape[1]
    col_k, col_v, col_xr = D_ATTN, 2 * D_ATTN, 3 * D_ATTN
    col_gr = col_xr + d_rnn
    col_ga = col_gr + d_rnn
    col_gb = col_ga + d

    tm = _pick_tile(m, ROW_TILE)
    tn = COL_TILE
    keep = min(KV_REACH, t)
    if keep == t:
        period, state_rows = 1, tm
    else:
        assert t % tm == 0 and keep <= tm
        period, state_rows = t // tm, keep
    rnn_w = (wts["conv_w"], wts["conv_b"], wts["w_gates"], wts["b_rg"], wts["b_ig"], wts["lru_lambda"])
    heads = HEADS_PER_STEP
    if cache_k is None:
        conv0 = jnp.zeros((b, SUBLANES, d_rnn), F32)
        h0 = jnp.zeros((b, 1, d_rnn), F32)
        z2, state, o_b, h_last = _in_proj_rglru(
            x2d, wts["pre_mix_g"], wts["w_in"], conv0, h0, *rnn_w, tm=tm, tn=tn, tt=_pick_tile(tm, RNN_FRAMES),
            period=period, state_rows=state_rows, state_col0=col_k, state_cols=col_gr - col_k,
            xr_col0=col_xr, gr_col0=col_gr, name=f"in_proj_rglru_{tag}")
        z3 = z2.reshape(b, t, n_in)
        offsets = [i * CHUNK for i in range(PAST_CHUNKS + 2)]
        o_a = _attention(z3, z3, z3, wts["rel_bias"], q_col0=0, k_col0=col_k, v_col0=col_v, bb=1,
                         heads=heads, offsets=offsets, name=f"attn_{tag}")
    else:
        z2, state = _in_proj(x2d, wts["pre_mix_g"], wts["w_in"], tm=tm, tn=tn, period=period,
                             state_rows=state_rows, state_col0=col_k, state_cols=col_gr - col_k,
                             name=f"in_proj_{tag}")
        z3 = z2.reshape(b, t, n_in)
        n_cached = cache_k.shape[1]
        assert t == CHUNK and n_cached == KV_REACH
        o_a = _attention(z3, z3, z3, wts["rel_bias"], q_col0=0, k_col0=col_k, v_col0=col_v,
                         bb=_pick_tile(b, SAMPLE_SEQS), heads=heads, offsets=[n_cached], name=f"attn_{tag}",
                         k_cache=cache_k.astype(BF16).reshape(b, n_cached, D_ATTN),
                         v_cache=cache_v.astype(BF16).reshape(b, n_cached, D_ATTN))
        conv0 = jnp.pad(conv_state.astype(F32), ((0, 0), (SUBLANES - (CONV_WIDTH - 1), 0), (0, 0)))
        h0 = h_state.astype(F32).reshape(b, 1, d_rnn)
        o_b, h_last = _rglru(z3, conv0, h0, *rnn_w, xr_col0=col_xr, gr_col0=col_gr,
                             tt=_pick_tile(t, RNN_FRAMES), cw=COL_TILE, name=f"rglru_{tag}")
    state = state.reshape(b, keep, -1)
    new_k = state[:, :, :D_ATTN].reshape(b, keep, N_HEADS, HEAD_DIM)
    new_v = state[:, :, D_ATTN:2 * D_ATTN].reshape(b, keep, N_HEADS, HEAD_DIM)
    conv_tail = state[:, keep - (CONV_WIDTH - 1):, 2 * D_ATTN:col_gr - col_k]

    merged = _gate_merge(o_a.reshape(m, D_ATTN), o_b.reshape(m, d_rnn), z2, wts["w_attn_up"],
                         wts["w_rnn_up"], ga_col0=col_ga, gb_col0=col_gb, tm=tm, tn=tn,
                         name=f"gate_merge_{tag}")
    tm2 = _pick_tile(m, WIDE_ROW_TILE)
    x1 = _out_norm(merged, wts["w_out"], x2d, wts["post_mix_g"], tm=tm2, name=f"out_norm_{tag}")
    y = _ffn(x1, wts["pre_ffn_g"], wts["w_ff1"], wts["w_ff2"], wts["post_ffn_g"], tm=tm2, tf=FFN_TILE,
             name=f"ffn_{tag}")
    return y.reshape(b, t, d), new_k, new_v, conv_tail, h_last.reshape(b, d_rnn)


def kernel(x_prompt, x_sample, cache_k, cache_v, state_conv, state_h, pre_mix_g, w_in, rel_bias, conv_w, conv_b, w_rg, b_rg, w_ig, b_ig, lru_lambda, w_attn_up, w_rnn_up, w_out, post_mix_g, pre_ffn_g, w_ff1, w_ff2, post_ffn_g):
    depth = w_in.shape[0]
    y_p, y_s = x_prompt, x_sample
    outs_p, outs_s = [], []
    for l in range(depth):
        wts = {
            "pre_mix_g": pre_mix_g[l][None], "post_mix_g": post_mix_g[l][None],
            "pre_ffn_g": pre_ffn_g[l][None], "post_ffn_g": post_ffn_g[l][None],
            "w_in": w_in[l].astype(BF16), "rel_bias": rel_bias[l],
            "conv_w": conv_w[l], "conv_b": conv_b[l][None],
            "w_gates": jnp.concatenate([w_rg[l], w_ig[l]], axis=-1).astype(BF16),
            "b_rg": b_rg[l][None], "b_ig": b_ig[l][None], "lru_lambda": lru_lambda[l][None],
            "w_attn_up": w_attn_up[l].astype(BF16), "w_rnn_up": w_rnn_up[l].astype(BF16),
            "w_out": w_out[l].astype(BF16), "w_ff1": w_ff1[l].astype(BF16), "w_ff2": w_ff2[l].astype(BF16),
        }
        y_p, *st_p = _layer(y_p, None, None, None, None, wts, f"p{l}")
        y_s, *st_s = _layer(y_s, cache_k[l], cache_v[l], state_conv[l], state_h[l], wts, f"s{l}")
        outs_p.append(st_p)
        outs_s.append(st_s)
    stack = lambda outs, i: jnp.stack([o[i] for o in outs])
    return (y_p, y_s,
            stack(outs_p, 0), stack(outs_p, 1), stack(outs_p, 2), stack(outs_p, 3),
            stack(outs_s, 0), stack(outs_s, 1), stack(outs_s, 2), stack(outs_s, 3))
```

```python
import functools

import jax
import jax.numpy as jnp
from jax import lax
from jax.experimental import pallas as pl
from jax.experimental.pallas import tpu as pltpu

F32 = jnp.float32
BF16 = jnp.bfloat16

CHUNK = 64
PAST_CHUNKS = 8
KV_REACH = PAST_CHUNKS * CHUNK
N_HEADS = 16
HEAD_DIM = 64
D_ATTN = N_HEADS * HEAD_DIM
MAX_REL = 256
RNN_BLOCK = 128
CONV_WIDTH = 4
LRU_C = 8.0
EPS = 1e-6
NEG_INF = -1e30

LANES = 128
SUBLANES = 8
MIB = 1024 * 1024

ROW_TILE = 1024
COL_TILE = 1024
WIDE_ROW_TILE = 512
FFN_TILE = 1024
RNN_FRAMES = 256
HEADS_PER_STEP = 2
SAMPLE_SEQS = 8
KT_BLOCK = 512
ROW_CHUNKS = 4
VMEM_PROJ, VMEM_FUSED, VMEM_ATTN, VMEM_RNN, VMEM_MLP = 52, 58, 40, 40, 48


def _params(n_axes, vmem_mib):
    return pltpu.CompilerParams(
        dimension_semantics=("arbitrary",) * n_axes,
        vmem_limit_bytes=vmem_mib * MIB,
    )


def _rmsnorm_rows(x, g):
    ms = jnp.mean(x * x, axis=-1, keepdims=True)
    return x * lax.rsqrt(ms + EPS) * g


def _in_proj_body(x_ref, g_ref, w_ref, o_ref, s_ref, xn_ref, *, period, j0, j1):
    i, j = pl.program_id(0), pl.program_id(1)

    @pl.when(j == 0)
    def _():
        xn_ref[...] = _rmsnorm_rows(x_ref[...], g_ref[...]).astype(BF16)

    keep_f32 = (i % period == period - 1) & (j >= j0) & (j < j1)

    @pl.when(keep_f32)
    def _():
        acc = jnp.dot(xn_ref[...], w_ref[...], preferred_element_type=F32)
        o_ref[...] = acc.astype(o_ref.dtype)
        s_ref[...] = acc[acc.shape[0] - s_ref.shape[0]:, :]

    @pl.when(jnp.logical_not(keep_f32))
    def _():
        o_ref[...] = jnp.dot(xn_ref[...], w_ref[...], preferred_element_type=F32).astype(o_ref.dtype)


def _in_proj(x2d, g, w, *, tm, tn, period, state_rows, state_col0, state_cols, name):
    m, d = x2d.shape
    n = w.shape[1]
    j0, j1 = state_col0 // tn, (state_col0 + state_cols) // tn
    n_state_tiles = m // tm // period

    def state_block(i, j):
        live = i % period == period - 1
        return i // period, jnp.where(live, jnp.clip(j - j0, 0, j1 - j0 - 1), 0)

    return pl.pallas_call(
        functools.partial(_in_proj_body, period=period, j0=j0, j1=j1),
        grid=(m // tm, n // tn),
        in_specs=[
            pl.BlockSpec((tm, d), lambda i, j: (i, 0)),
            pl.BlockSpec((1, d), lambda i, j: (0, 0)),
            pl.BlockSpec((d, tn), lambda i, j: (0, j)),
        ],
        out_specs=[
            pl.BlockSpec((tm, tn), lambda i, j: (i, j)),
            pl.BlockSpec((state_rows, tn), state_block),
        ],
        out_shape=[
            jax.ShapeDtypeStruct((m, n), BF16),
            jax.ShapeDtypeStruct((n_state_tiles * state_rows, state_cols), F32),
        ],
        scratch_shapes=[pltpu.VMEM((tm, d), BF16)],
        compiler_params=_params(2, VMEM_PROJ),
        name=name,
    )(x2d, g, w)


def _bias_diagonals(rel_bias, width, offsets, heads):
    x = jnp.arange(width)
    rows = [rel_bias[:, jnp.clip(off + CHUNK - 1 - x, -MAX_REL, MAX_REL) + MAX_REL] for off in offsets]
    g = jnp.stack(rows, axis=1).astype(F32)
    n_groups = rel_bias.shape[0] // heads
    g = g.reshape(n_groups, heads, len(offsets), width)
    return jnp.swapaxes(g, 1, 2).reshape(n_groups, len(offsets) * heads, width)


def _attn_body(*refs, bb, n_chunks, kw, heads, offsets, unit, cached):
    if cached:
        q_ref, kc_ref, vc_ref, k_ref, v_ref, g_ref, o_ref, tab_ref, s_ref, p_ref, l_ref = refs
        band = lambda c_ref, n_ref, bi, ks: jnp.concatenate([c_ref[bi], n_ref[bi]], axis=0)
    else:
        q_ref, k_ref, v_ref, g_ref, o_ref, tab_ref, s_ref, p_ref, l_ref, kt_ref = refs
        kc_ref = vc_ref = None
        band = lambda c_ref, n_ref, bi, ks: n_ref[bi, pl.ds(ks, kw), :]
    width = g_ref.shape[2]
    hw = heads * HEAD_DIM

    @pl.when(pl.program_id(1) == 0)
    def _():
        r = lax.broadcasted_iota(jnp.int32, (CHUNK, kw), 0)
        j = lax.broadcasted_iota(jnp.int32, (CHUNK, kw), 1)
        for c, off in enumerate(offsets):
            cq = (r + off) // CHUNK
            ck = j // CHUNK
            valid = (ck <= cq) & (ck >= cq - PAST_CHUNKS)
            for h in range(heads):
                g = jnp.broadcast_to(g_ref[0, c * heads + h:c * heads + h + 1, :], (CHUNK, width))
                t = pltpu.roll(g, width - (CHUNK - 1), 1, stride=1, stride_axis=0)
                tab_ref[c, h * CHUNK:(h + 1) * CHUNK, :] = jnp.where(valid, t[:, :kw], NEG_INF)

    if not cached:
        for r0 in range(0, k_ref.shape[1], KT_BLOCK):
            kt_ref[:, r0:r0 + KT_BLOCK] = k_ref[0, r0:r0 + KT_BLOCK, :].T

    lane = lax.broadcasted_iota(jnp.int32, (1, hw), 1)
    head_lanes = [(lane >= h * HEAD_DIM) & (lane < (h + 1) * HEAD_DIM) for h in range(heads)]
    rows = heads * CHUNK
    n_units = bb * n_chunks // unit

    def locate(u, j):
        idx = u * unit + j
        bi, c = (0, idx) if bb == 1 else (idx // n_chunks, idx % n_chunks)
        q0 = pl.multiple_of(c * CHUNK, CHUNK)
        if cached:
            return bi, 0, q0, 0
        past = jnp.maximum(c - PAST_CHUNKS, 0)
        odd = past % 2
        ks = pl.multiple_of((past - odd) * CHUNK, 2 * CHUNK)
        return bi, jnp.minimum(c, PAST_CHUNKS) + odd, q0, ks

    def scores(u, slot):
        for j in range(unit):
            bi, c, q0, ks = locate(u, j)
            q = q_ref[bi, pl.ds(q0, CHUNK), :]
            qs = jnp.concatenate([jnp.where(m, q, jnp.zeros_like(q)) for m in head_lanes], axis=0)
            qs = qs * jnp.asarray(HEAD_DIM ** -0.5, q.dtype)
            if cached:
                kb = band(kc_ref, k_ref, bi, ks)
                s = lax.dot_general(qs, kb, (((1,), (1,)), ((), ())), preferred_element_type=F32)
            else:
                s = jnp.dot(qs, kt_ref[:, pl.ds(ks, kw)], preferred_element_type=F32)
            s_ref[slot, j * rows:(j + 1) * rows, :] = s + tab_ref[c]

    def softmax(slot):
        s = s_ref[slot]
        p = jnp.exp(s - jnp.max(s, axis=-1, keepdims=True))
        p_ref[slot] = p.astype(p_ref.dtype)
        l_ref[slot] = 1.0 / jnp.sum(p, axis=-1, keepdims=True)

    def output(u, slot):
        for j in range(unit):
            bi, c, q0, ks = locate(u, j)
            vb = band(vc_ref, v_ref, bi, ks)
            o_all = jnp.dot(p_ref[slot, j * rows:(j + 1) * rows, :], vb, preferred_element_type=F32)
            o_all = o_all * l_ref[slot, j * rows:(j + 1) * rows, :]
            o = o_all[0:CHUNK]
            for h in range(1, heads):
                o = jnp.where(head_lanes[h], o_all[h * CHUNK:(h + 1) * CHUNK], o)
            o_ref[bi, pl.ds(q0, CHUNK), :] = o.astype(o_ref.dtype)

    def step(i, slot, do_scores, do_softmax, do_output):
        if do_scores:
            scores(i, slot)
        if do_output:
            output(i - 2, slot)
        if do_softmax:
            softmax(1 - slot)

    for i in range(2):
        step(i, i % 2, i < n_units, 1 <= i <= n_units, False)
    n_pairs = max(n_units - 2, 0) // 2
    if n_pairs:
        def body(t, carry):
            i = 2 + 2 * t
            step(i, 0, True, True, True)
            step(i + 1, 1, True, True, True)
            return carry
        lax.fori_loop(0, n_pairs, body, 0)
    for i in range(2 + 2 * n_pairs, n_units + 2):
        step(i, i % 2, i < n_units, 1 <= i <= n_units, i >= 2)


def _attention(q_arr, k_arr, v_arr, rel_bias, *, q_col0, k_col0, v_col0, bb, heads, offsets, name,
               k_cache=None, v_cache=None):
    b, sq, _ = q_arr.shape
    sk = k_arr.shape[1]
    cached = k_cache is not None
    if cached:
        kw = KV_REACH + CHUNK
        assert sq == CHUNK and sk == CHUNK and k_cache.shape[1] == KV_REACH and len(offsets) == 1
    else:
        kw = KV_REACH + 2 * CHUNK
        assert bb == 1 and sq == sk and sk % KT_BLOCK == 0 and sk >= kw and len(offsets) == PAST_CHUNKS + 2
    hw = heads * HEAD_DIM
    n_groups = N_HEADS // heads
    n_case = len(offsets)
    width = -(-(kw + CHUNK - 1) // LANES) * LANES
    diag = _bias_diagonals(rel_bias, width, offsets, heads)
    qc, kc, vc = q_col0 // hw, k_col0 // hw, v_col0 // hw
    unit = 2
    assert (bb * sq // CHUNK) % unit == 0
    rows = unit * heads * CHUNK
    cache_specs = [pl.BlockSpec((bb, KV_REACH, hw), lambda g, i: (i, 0, g))] * 2 if cached else []
    cache_args = [k_cache, v_cache] if cached else []
    return pl.pallas_call(
        functools.partial(_attn_body, bb=bb, n_chunks=sq // CHUNK, kw=kw, heads=heads,
                          offsets=tuple(offsets), unit=unit, cached=cached),
        grid=(n_groups, b // bb),
        in_specs=[pl.BlockSpec((bb, sq, hw), lambda g, i: (i, 0, qc + g))] + cache_specs + [
            pl.BlockSpec((bb, sk, hw), lambda g, i: (i, 0, kc + g)),
            pl.BlockSpec((bb, sk, hw), lambda g, i: (i, 0, vc + g)),
            pl.BlockSpec((1, n_case * heads, width), lambda g, i: (g, 0, 0)),
        ],
        out_specs=pl.BlockSpec((bb, sq, hw), lambda g, i: (i, 0, g)),
        out_shape=jax.ShapeDtypeStruct((b, sq, D_ATTN), BF16),
        scratch_shapes=[
            pltpu.VMEM((n_case, heads * CHUNK, kw), F32),
            pltpu.VMEM((2, rows, kw), F32),
            pltpu.VMEM((2, rows, kw), BF16),
            pltpu.VMEM((2, rows, 1), F32),
        ] + ([] if cached else [pltpu.VMEM((hw, sk), BF16)]),
        compiler_params=_params(2, VMEM_ATTN),
        name=name,
    )(q_arr, *cache_args, k_arr, v_arr, diag)


def _log_sigmoid(x):
    return jnp.minimum(x, 0.0) - jnp.log1p(jnp.exp(-jnp.abs(x)))


def _shift_matrix(tt):
    t = jnp.arange(tt)
    blocks = [(t[:, None] - d) == t[None, :] for d in range(CONV_WIDTH - 1, 0, -1)]
    return jnp.concatenate(blocks, axis=0).astype(BF16)


def _rglru_head(x_bf, prev8, shift, conv_w, conv_b, wg_ref, b_rg, b_ig):
    tt, cw = x_bf.shape
    taps = CONV_WIDTH - 1
    x = x_bf.astype(F32)
    row = lax.broadcasted_iota(jnp.int32, (SUBLANES, cw), 0)
    if shift is not None:
        sh = jnp.dot(shift, x_bf, preferred_element_type=F32)
        xc = conv_b + conv_w[0:1] * sh[0:tt]
        for k in range(1, taps):
            xc = xc + conv_w[k:k + 1] * sh[k * tt:(k + 1) * tt]
        xc = xc + conv_w[taps:taps + 1] * x
        head = xc[0:SUBLANES]
        for k in range(taps):
            d = taps - k
            head = head + conv_w[k:k + 1] * jnp.where(row < d, pltpu.roll(prev8, d, 0), 0.0)
        xc = jnp.concatenate([head, xc[SUBLANES:]], axis=0)
    else:
        xc = conv_b
        for k in range(taps):
            d = taps - k
            sh = pltpu.roll(x, d, 0)
            head = jnp.where(row < d, pltpu.roll(prev8, d, 0), sh[0:SUBLANES])
            xc = xc + conv_w[k:k + 1] * jnp.concatenate([head, sh[SUBLANES:]], axis=0)
        xc = xc + conv_w[taps:taps + 1] * x

    xcb = xc.astype(BF16)
    r_parts, i_parts = [], []
    for n in range(cw // RNN_BLOCK):
        gts = jnp.dot(xcb[:, n * RNN_BLOCK:(n + 1) * RNN_BLOCK], wg_ref[n], preferred_element_type=F32)
        r_parts.append(gts[:, :RNN_BLOCK])
        i_parts.append(gts[:, RNN_BLOCK:])
    r_pre = jnp.concatenate(r_parts, axis=1) + b_rg
    i_pre = jnp.concatenate(i_parts, axis=1) + b_ig
    return xc, r_pre, i_pre, x[tt - SUBLANES:tt]


def _rglru_tail(xc, r_pre, i_pre, gr_bf, h_in, lam):
    rows, cw = xc.shape
    r = jax.nn.sigmoid(r_pre)
    i = jax.nn.sigmoid(i_pre)
    log_a = r * (LRU_C * _log_sigmoid(lam))
    a = jnp.exp(log_a)
    w = -jnp.tanh(log_a)
    one_minus_a2 = 2.0 * w / (1.0 + w)
    root = jnp.where(one_minus_a2 > 0.0, one_minus_a2 * lax.rsqrt(one_minus_a2), 0.0)
    u = root * (i * xc)

    row = lax.broadcasted_iota(jnp.int32, (SUBLANES, cw), 0)
    h = h_in
    hs = []
    for g in range(rows // SUBLANES):
        ag = a[g * SUBLANES:(g + 1) * SUBLANES]
        ug = u[g * SUBLANES:(g + 1) * SUBLANES]
        for d in (1, 2, 4):
            keep = row >= d
            a_prev = jnp.where(keep, pltpu.roll(ag, d, 0), 1.0)
            u_prev = jnp.where(keep, pltpu.roll(ug, d, 0), 0.0)
            ug = ag * u_prev + ug
            ag = ag * a_prev
        hg = ug + ag * h
        hs.append(hg)
        h = jnp.broadcast_to(hg[SUBLANES - 1:SUBLANES, :], (SUBLANES, cw))
    gate = jax.nn.gelu(gr_bf.astype(F32), approximate=True)
    return (jnp.concatenate(hs, axis=0) * gate).astype(BF16), h


def _rglru_body(xr_ref, gr_ref, c0_ref, h0_ref, sh_ref, cw_ref, cb_ref, wg_ref, brg_ref, big_ref, lam_ref,
                ob_ref, hl_ref, prev_ref, hcar_ref):
    @pl.when(pl.program_id(2) == 0)
    def _():
        hcar_ref[...] = jnp.broadcast_to(h0_ref[0], hcar_ref.shape)
        prev_ref[...] = c0_ref[0]

    xc, r_pre, i_pre, prev8 = _rglru_head(xr_ref[0], prev_ref[...], sh_ref[...], cw_ref[...], cb_ref[...],
                                          wg_ref, brg_ref[...], big_ref[...])
    out, h = _rglru_tail(xc, r_pre, i_pre, gr_ref[0], hcar_ref[...], lam_ref[...])
    ob_ref[0] = out
    prev_ref[...] = prev8
    hcar_ref[...] = h
    hl_ref[0] = h[0:1, :]


def _rglru(z3, conv0, h0, conv_w, conv_b, w_gates, b_rg, b_ig, lam, *, xr_col0, gr_col0, tt, cw, name):
    b, t, _ = z3.shape
    d_rnn = conv_w.shape[1]
    n_cg = d_rnn // cw
    blocks_per_cg = cw // RNN_BLOCK
    xc0, gc0 = xr_col0 // cw, gr_col0 // cw
    vec = pl.BlockSpec((1, cw), lambda c, i, s: (0, c))
    return pl.pallas_call(
        _rglru_body,
        grid=(n_cg, b, t // tt),
        in_specs=[
            pl.BlockSpec((1, tt, cw), lambda c, i, s: (i, s, xc0 + c)),
            pl.BlockSpec((1, tt, cw), lambda c, i, s: (i, s, gc0 + c)),
            pl.BlockSpec((1, SUBLANES, cw), lambda c, i, s: (i, 0, c)),
            pl.BlockSpec((1, 1, cw), lambda c, i, s: (i, 0, c)),
            pl.BlockSpec(((CONV_WIDTH - 1) * tt, tt), lambda c, i, s: (0, 0)),
            pl.BlockSpec((CONV_WIDTH, cw), lambda c, i, s: (0, c)),
            vec,
            pl.BlockSpec((blocks_per_cg, RNN_BLOCK, 2 * RNN_BLOCK), lambda c, i, s: (c, 0, 0)),
            vec, vec, vec,
        ],
        out_specs=[
            pl.BlockSpec((1, tt, cw), lambda c, i, s: (i, s, c)),
            pl.BlockSpec((1, 1, cw), lambda c, i, s: (i, 0, c)),
        ],
        out_shape=[
            jax.ShapeDtypeStruct((b, t, d_rnn), BF16),
            jax.ShapeDtypeStruct((b, 1, d_rnn), F32),
        ],
        scratch_shapes=[
            pltpu.VMEM((SUBLANES, cw), F32),
            pltpu.VMEM((SUBLANES, cw), F32),
        ],
        compiler_params=_params(3, VMEM_RNN),
        name=name,
    )(z3, z3, conv0, h0, _shift_matrix(tt), conv_w, conv_b, w_gates, b_rg, b_ig, lam)


def _lookup(table, j):
    out = jnp.int32(table[-1])
    for k in range(len(table) - 2, -1, -1):
        out = jnp.where(j == k, jnp.int32(table[k]), out)
    return out


def _fused_body(x_ref, g_ref, w_ref, c0_ref, h0_ref, cw_ref, cb_ref, wg_ref, brg_ref, big_ref,
                lam_ref, z_ref, s_ref, ob_ref, hl_ref, xn_ref, stash_ref, prev_ref, hcar_ref, hd_ref, gr_ref, *,
                n_tiles, period, n_cg, pieces_per_cg, tt, n_split, stash_slots):
    i, j = pl.program_id(0), pl.program_id(1)
    cw = ob_ref.shape[1]

    @pl.when((j == 0) & (i < n_tiles))
    def _():
        xn_ref[...] = _rmsnorm_rows(x_ref[...], g_ref[...]).astype(BF16)

    @pl.when((j == 0) & (i == 0))
    def _():
        prev_ref[...] = jnp.zeros_like(prev_ref)
        hcar_ref[...] = jnp.zeros_like(hcar_ref)

    tm = xn_ref.shape[0]
    rows_s = tt // n_split
    state_start = tm - s_ref.shape[0]
    first_rows = tm // 4
    rest_rows = (tm - first_rows) // n_split
    bounds = [0] + [first_rows + c * rest_rows for c in range(n_split + 1)]

    def project(c):
        lo, hi = bounds[c], bounds[c + 1]
        acc = jnp.dot(xn_ref[lo:hi, :], w_ref[...], preferred_element_type=F32)
        zb = acc.astype(BF16)
        z_ref[lo:hi, :] = zb
        if hi > state_start:
            keep = max(lo, state_start)
            s_ref[keep - state_start:hi - state_start, :] = acc[keep - lo:, :]
        stash_ref[_lookup(stash_slots, j), lo:hi, :] = zb
        return acc[hi - lo - SUBLANES:, :]

    def after(state, acc_rows):
        cg, lam, h = state
        zero = (pltpu.bitcast(acc_rows, jnp.uint32) >> 16) >> 16
        return cg, lam, pltpu.bitcast(pltpu.bitcast(h, jnp.uint32) + zero, F32)

    def recur_load():
        cg, tp = j // pieces_per_cg, j % pieces_per_cg
        r0 = pl.multiple_of(tp * tt, tt)
        first = ((i - 1) % period == 0) & (tp == 0)
        prev8 = jnp.where(first, c0_ref[0, cg], prev_ref[cg])
        h = jnp.where(first, jnp.broadcast_to(h0_ref[0, cg], (SUBLANES, cw)), hcar_ref[cg])
        gr_ref[...] = stash_ref[n_cg + cg, pl.ds(r0, tt), :]
        return cg, stash_ref[cg, pl.ds(r0, tt), :], prev8, h

    def recur_head(loaded):
        cg, x_bf, prev8, h = loaded
        xc, r_pre, i_pre, prev8 = _rglru_head(x_bf, prev8, None, cw_ref[cg], cb_ref[cg], wg_ref.at[cg],
                                              brg_ref[cg], big_ref[cg])
        hd_ref[0] = xc
        hd_ref[1] = r_pre
        hd_ref[2] = i_pre
        prev_ref[cg] = prev8
        return cg, lam_ref[cg], h

    def recur_tail(k, state):
        cg, lam, h = state
        lo = k * rows_s
        out, h = _rglru_tail(hd_ref[0, lo:lo + rows_s, :], hd_ref[1, lo:lo + rows_s, :],
                             hd_ref[2, lo:lo + rows_s, :], gr_ref[lo:lo + rows_s, :], h, lam)
        ob_ref[lo:lo + rows_s, :] = out
        return cg, lam, h

    def recur_end(state):
        cg, _, h = state
        hcar_ref[cg] = h
        hl_ref[0, 0] = h[0:1, :]

    has_proj = i < n_tiles
    has_rec = (i >= 1) & (j < n_cg * pieces_per_cg)

    @pl.when(has_proj & has_rec)
    def _():
        state = after(recur_head(recur_load()), project(0))
        for k in range(n_split):
            state = recur_tail(k, state)
            acc_rows = project(k + 1)
            if k + 1 < n_split:
                state = after(state, acc_rows)
        recur_end(state)

    @pl.when(has_proj & jnp.logical_not(has_rec))
    def _():
        for c in range(n_split + 1):
            project(c)

    @pl.when(jnp.logical_not(has_proj) & has_rec)
    def _():
        state = recur_head(recur_load())
        for k in range(n_split):
            state = recur_tail(k, state)
        recur_end(state)


def _in_proj_rglru(x2d, g, w, conv0, h0, conv_w, conv_b, w_gates, b_rg, b_ig, lam, *, tm, tn, tt, period,
                   state_rows, state_col0, state_cols, xr_col0, gr_col0, name):
    m, d = x2d.shape
    n = w.shape[1]
    d_rnn = conv_w.shape[1]
    n_tiles, n_col, n_cg = m // tm, n // tn, d_rnn // tn
    n_seq = n_tiles // period
    pieces_per_cg = tm // tt
    n_pieces = n_cg * pieces_per_cg
    xr0, gr0 = xr_col0 // tn, gr_col0 // tn
    assert n_pieces + n_cg <= n_col and tm % tt == 0 and n_tiles % period == 0

    order, stash_slots = [None] * n_col, [2 * n_cg] * n_col
    for c in range(n_cg):
        jx = (c + 1) * pieces_per_cg
        order[jx], stash_slots[jx] = xr0 + c, c
        order[jx + 1], stash_slots[jx + 1] = gr0 + c, n_cg + c
    rest = [c for c in range(n_col) if c not in order]
    for jj in range(n_col):
        if order[jj] is None:
            order[jj] = rest.pop(0)
    s0, n_state = state_col0 // tn, state_cols // tn
    state_pos = [order.index(s0 + k) for k in range(n_state)]
    assert state_pos == sorted(state_pos)
    state_blk = [sum(p < jj for p in state_pos) for jj in range(n_col)]

    last = lambda i: i == n_tiles
    row = lambda i: jnp.minimum(i, n_tiles - 1)
    col = lambda i, j: jnp.where(last(i), order[-1], _lookup(order, j))
    seq_prev = lambda i: jnp.maximum(i - 1, 0) // period
    piece = lambda i, j: jnp.where(i == 0, 0, jnp.minimum(j, n_pieces - 1))

    def state_block(i, j):
        live = i % period == period - 1
        blk = jnp.where(live, _lookup(state_blk, j), n_state + 1)
        return row(i) // period, jnp.where(last(i), n_state, blk)

    per_cg = lambda a: a.reshape(a.shape[0], n_cg, tn).swapaxes(0, 1)
    whole = lambda a: pl.BlockSpec(a.shape, lambda i, j: (0,) * a.ndim)
    n_split = 4
    assert tt % (n_split * SUBLANES) == 0 and (tm - tm // 4) % (2 * SUBLANES * n_split) == 0
    params = [per_cg(conv_w), per_cg(conv_b), w_gates.reshape(n_cg, -1, RNN_BLOCK, 2 * RNN_BLOCK),
              per_cg(b_rg), per_cg(b_ig), per_cg(lam)]
    conv0 = conv0.reshape(n_seq, SUBLANES, n_cg, tn).swapaxes(1, 2)
    h0 = h0.reshape(n_seq, 1, n_cg, tn).swapaxes(1, 2)

    z, state, o_b, h_last = pl.pallas_call(
        functools.partial(_fused_body, n_tiles=n_tiles, period=period, n_cg=n_cg,
                          pieces_per_cg=pieces_per_cg, tt=tt, n_split=n_split,
                          stash_slots=tuple(stash_slots)),
        grid=(n_tiles + 1, n_col),
        in_specs=[
            pl.BlockSpec((tm, d), lambda i, j: (row(i), 0)),
            pl.BlockSpec((1, d), lambda i, j: (0, 0)),
            pl.BlockSpec((d, tn), lambda i, j: (0, col(i, j))),
            pl.BlockSpec((1, n_cg, SUBLANES, tn), lambda i, j: (seq_prev(i), 0, 0, 0)),
            pl.BlockSpec((1, n_cg, 1, tn), lambda i, j: (seq_prev(i), 0, 0, 0)),
        ] + [whole(p) for p in params],
        out_specs=[
            pl.BlockSpec((tm, tn), lambda i, j: (row(i), col(i, j))),
            pl.BlockSpec((state_rows, tn), state_block),
            pl.BlockSpec((tt, tn), lambda i, j: (jnp.maximum(i - 1, 0) * pieces_per_cg
                                                 + piece(i, j) % pieces_per_cg, piece(i, j) // pieces_per_cg)),
            pl.BlockSpec((1, 1, 1, tn), lambda i, j: (jnp.maximum(i - 1, 0), piece(i, j) // pieces_per_cg, 0, 0)),
        ],
        out_shape=[
            jax.ShapeDtypeStruct((m, n), BF16),
            jax.ShapeDtypeStruct((n_seq * state_rows, (n_state + 2) * tn), F32),
            jax.ShapeDtypeStruct((m, d_rnn), BF16),
            jax.ShapeDtypeStruct((n_tiles, n_cg, 1, tn), F32),
        ],
        scratch_shapes=[
            pltpu.VMEM((tm, d), BF16),
            pltpu.VMEM((2 * n_cg + 1, tm, tn), BF16),
            pltpu.VMEM((n_cg, SUBLANES, tn), F32),
            pltpu.VMEM((n_cg, SUBLANES, tn), F32),
            pltpu.VMEM((3, tt, tn), F32),
            pltpu.VMEM((tt, tn), BF16),
        ],
        compiler_params=_params(2, VMEM_FUSED),
        name=name,
    )(x2d, g, w, conv0, h0, *params)
    return z, state, o_b, h_last.reshape(n_tiles, d_rnn)[period - 1::period]


def _merge_body(oa_ref, ob_ref, ga_ref, gb_ref, wa_ref, wb_ref, o_ref):
    tm = o_ref.shape[0]
    for c in range(ROW_CHUNKS):
        lo, hi = c * tm // ROW_CHUNKS, (c + 1) * tm // ROW_CHUNKS
        ya = jnp.dot(oa_ref[lo:hi, :], wa_ref[...], preferred_element_type=F32)
        yb = jnp.dot(ob_ref[lo:hi, :], wb_ref[...], preferred_element_type=F32)
        ga = jax.nn.sigmoid(ga_ref[lo:hi, :].astype(F32))
        gb = jax.nn.sigmoid(gb_ref[lo:hi, :].astype(F32))
        o_ref[lo:hi, :] = (ga * ya + gb * yb).astype(o_ref.dtype)


def _gate_merge(o_a, o_b, z2, w_a, w_b, *, ga_col0, gb_col0, tm, tn, name):
    m = o_a.shape[0]
    d = w_a.shape[1]
    ga0, gb0 = ga_col0 // tn, gb_col0 // tn
    return pl.pallas_call(
        _merge_body,
        grid=(m // tm, d // tn),
        in_specs=[
            pl.BlockSpec((tm, o_a.shape[1]), lambda i, j: (i, 0)),
            pl.BlockSpec((tm, o_b.shape[1]), lambda i, j: (i, 0)),
            pl.BlockSpec((tm, tn), lambda i, j: (i, ga0 + j)),
            pl.BlockSpec((tm, tn), lambda i, j: (i, gb0 + j)),
            pl.BlockSpec((w_a.shape[0], tn), lambda i, j: (0, j)),
            pl.BlockSpec((w_b.shape[0], tn), lambda i, j: (0, j)),
        ],
        out_specs=pl.BlockSpec((tm, tn), lambda i, j: (i, j)),
        out_shape=jax.ShapeDtypeStruct((m, d), BF16),
        compiler_params=_params(2, VMEM_MLP),
        name=name,
    )(o_a, o_b, z2, z2, w_a, w_b)


def _out_norm_body(m_ref, w_ref, x_ref, g_ref, o_ref):
    y = jnp.dot(m_ref[...], w_ref[...], preferred_element_type=F32)
    o_ref[...] = x_ref[...] + _rmsnorm_rows(y, g_ref[...])


def _out_norm(merged, w_out, x2d, g, *, tm, name):
    m, d = x2d.shape
    return pl.pallas_call(
        _out_norm_body,
        grid=(m // tm,),
        in_specs=[
            pl.BlockSpec((tm, d), lambda i: (i, 0)),
            pl.BlockSpec((d, d), lambda i: (0, 0)),
            pl.BlockSpec((tm, d), lambda i: (i, 0)),
            pl.BlockSpec((1, d), lambda i: (0, 0)),
        ],
        out_specs=pl.BlockSpec((tm, d), lambda i: (i, 0)),
        out_shape=jax.ShapeDtypeStruct((m, d), F32),
        compiler_params=_params(1, VMEM_MLP),
        name=name,
    )(merged, w_out, x2d, g)


def _ffn_body(x_ref, g1_ref, w1_ref, w2_ref, g2_ref, o_ref, xn_ref, *, n_steps, edge_chunks):
    f = pl.program_id(1)

    def rows_chunk(lo, hi, first, last):
        if first:
            xn = _rmsnorm_rows(x_ref[lo:hi, :], g1_ref[...]).astype(BF16)
            xn_ref[lo:hi, :] = xn
        else:
            xn = xn_ref[lo:hi, :]
        hid = jnp.dot(xn, w1_ref[...], preferred_element_type=F32)
        hid = jnp.square(jnp.maximum(hid, 0.0)).astype(BF16)
        acc = jnp.dot(hid, w2_ref[...], preferred_element_type=F32)
        if not first:
            acc = o_ref[lo:hi, :] + acc
        if last:
            acc = x_ref[lo:hi, :] + _rmsnorm_rows(acc, g2_ref[...])
        o_ref[lo:hi, :] = acc

    tm = o_ref.shape[0]
    for first, last in sorted({(k == 0, k == n_steps - 1) for k in range(n_steps)}):
        chunks = edge_chunks if (first or last) else 1

        @pl.when(((f == 0) == first) & ((f == n_steps - 1) == last))
        def _(first=first, last=last, chunks=chunks):
            for c in range(chunks):
                rows_chunk(c * tm // chunks, (c + 1) * tm // chunks, first, last)


def _ffn(x2d, g1, w1, w2, g2, *, tm, tf, name):
    m, d = x2d.shape
    d_ff = w1.shape[1]
    return pl.pallas_call(
        functools.partial(_ffn_body, n_steps=d_ff // tf, edge_chunks=ROW_CHUNKS),
        grid=(m // tm, d_ff // tf),
        in_specs=[
            pl.BlockSpec((tm, d), lambda i, f: (i, 0)),
            pl.BlockSpec((1, d), lambda i, f: (0, 0)),
            pl.BlockSpec((d, tf), lambda i, f: (0, f)),
            pl.BlockSpec((tf, d), lambda i, f: (f, 0)),
            pl.BlockSpec((1, d), lambda i, f: (0, 0)),
        ],
        out_specs=pl.BlockSpec((tm, d), lambda i, f: (i, 0)),
        out_shape=jax.ShapeDtypeStruct((m, d), F32),
        scratch_shapes=[pltpu.VMEM((tm, d), BF16)],
        compiler_params=_params(2, VMEM_MLP),
        name=name,
    )(x2d, g1, w1, w2, g2)


def _pick_tile(n, target):
    t = min(n, target)
    while n % t:
        t //= 2
    return t


def _layer(x, cache_k, cache_v, conv_state, h_state, wts, tag):
    b, t, d = x.shape
    m = b * t
    x2d = x.reshape(m, d)
    d_rnn = wts["conv_w"].shape[1]
    n_in = wts["w_in"].shape[1]
    col_k, col_v, col_xr = D_ATTN, 2 * D_ATTN, 3 * D_ATTN
    col_gr = col_xr + d_rnn
    col_ga = col_gr + d_rnn
    col_gb = col_ga + d

    tm = _pick_tile(m, ROW_TILE)
    tn = COL_TILE
    keep = min(KV_REACH, t)
    if keep == t:
        period, state_rows = 1, tm
    else:
        assert t % tm == 0 and keep <= tm
        period, state_rows = t // tm, keep
    rnn_w = (wts["conv_w"], wts["conv_b"], wts["w_gates"], wts["b_rg"], wts["b_ig"], wts["lru_lambda"])
    heads = HEADS_PER_STEP
    if cache_k is None:
        conv0 = jnp.zeros((b, SUBLANES, d_rnn), F32)
        h0 = jnp.zeros((b, 1, d_rnn), F32)
        z2, state, o_b, h_last = _in_proj_rglru(
            x2d, wts["pre_mix_g"], wts["w_in"], conv0, h0, *rnn_w, tm=tm, tn=tn, tt=_pick_tile(tm, RNN_FRAMES),
            period=period, state_rows=state_rows, state_col0=col_k, state_cols=col_gr - col_k,
            xr_col0=col_xr, gr_col0=col_gr, name=f"in_proj_rglru_{tag}")
        z3 = z2.reshape(b, t, n_in)
        offsets = [i * CHUNK for i in range(PAST_CHUNKS + 2)]
        o_a = _attention(z3, z3, z3, wts["rel_bias"], q_col0=0, k_col0=col_k, v_col0=col_v, bb=1,
                         heads=heads, offsets=offsets, name=f"attn_{tag}")
    else:
        z2, state = _in_proj(x2d, wts["pre_mix_g"], wts["w_in"], tm=tm, tn=tn, period=period,
                             state_rows=state_rows, state_col0=col_k, state_cols=col_gr - col_k,
                             name=f"in_proj_{tag}")
        z3 = z2.reshape(b, t, n_in)
        n_cached = cache_k.shape[1]
        assert t == CHUNK and n_cached == KV_REACH
        kc16, vc16 = lax.optimization_barrier((cache_k.astype(BF16), cache_v.astype(BF16)))
        o_a = _attention(z3, z3, z3, wts["rel_bias"], q_col0=0, k_col0=col_k, v_col0=col_v,
                         bb=_pick_tile(b, SAMPLE_SEQS), heads=heads, offsets=[n_cached], name=f"attn_{tag}",
                         k_cache=kc16.reshape(b, n_cached, D_ATTN),
                         v_cache=vc16.reshape(b, n_cached, D_ATTN))
        conv0 = jnp.pad(conv_state.astype(F32), ((0, 0), (SUBLANES - (CONV_WIDTH - 1), 0), (0, 0)))
        h0 = h_state.astype(F32).reshape(b, 1, d_rnn)
        o_b, h_last = _rglru(z3, conv0, h0, *rnn_w, xr_col0=col_xr, gr_col0=col_gr,
                             tt=_pick_tile(t, RNN_FRAMES), cw=COL_TILE, name=f"rglru_{tag}")
    state = state.reshape(b, keep, -1)
    new_k = state[:, :, :D_ATTN].reshape(b, keep, N_HEADS, HEAD_DIM)
    new_v = state[:, :, D_ATTN:2 * D_ATTN].reshape(b, keep, N_HEADS, HEAD_DIM)
    conv_tail = state[:, keep - (CONV_WIDTH - 1):, 2 * D_ATTN:col_gr - col_k]

    merged = _gate_merge(o_a.reshape(m, D_ATTN), o_b.reshape(m, d_rnn), z2, wts["w_attn_up"],
                         wts["w_rnn_up"], ga_col0=col_ga, gb_col0=col_gb, tm=tm, tn=tn,
                         name=f"gate_merge_{tag}")
    tm2 = _pick_tile(m, WIDE_ROW_TILE)
    x1 = _out_norm(merged, wts["w_out"], x2d, wts["post_mix_g"], tm=tm2, name=f"out_norm_{tag}")
    y = _ffn(x1, wts["pre_ffn_g"], wts["w_ff1"], wts["w_ff2"], wts["post_ffn_g"], tm=tm2, tf=FFN_TILE,
             name=f"ffn_{tag}")
    return y.reshape(b, t, d), new_k, new_v, conv_tail, h_last.reshape(b, d_rnn)


def kernel(x_prompt, x_sample, cache_k, cache_v, state_conv, state_h, pre_mix_g, w_in, rel_bias, conv_w, conv_b, w_rg, b_rg, w_ig, b_ig, lru_lambda, w_attn_up, w_rnn_up, w_out, post_mix_g, pre_ffn_g, w_ff1, w_ff2, post_ffn_g):
    depth = w_in.shape[0]
    y_p, y_s = x_prompt, x_sample
    outs_p, outs_s = [], []
    for l in range(depth):
        wts = {
            "pre_mix_g": pre_mix_g[l][None], "post_mix_g": post_mix_g[l][None],
            "pre_ffn_g": pre_ffn_g[l][None], "post_ffn_g": post_ffn_g[l][None],
            "w_in": w_in[l].astype(BF16), "rel_bias": rel_bias[l],
            "conv_w": conv_w[l], "conv_b": conv_b[l][None],
            "w_gates": jnp.concatenate([w_rg[l], w_ig[l]], axis=-1).astype(BF16),
            "b_rg": b_rg[l][None], "b_ig": b_ig[l][None], "lru_lambda": lru_lambda[l][None],
            "w_attn_up": w_attn_up[l].astype(BF16), "w_rnn_up": w_rnn_up[l].astype(BF16),
            "w_out": w_out[l].astype(BF16), "w_ff1": w_ff1[l].astype(BF16), "w_ff2": w_ff2[l].astype(BF16),
        }
        y_p, *st_p = _layer(y_p, None, None, None, None, wts, f"p{l}")
        y_s, *st_s = _layer(y_s, cache_k[l], cache_v[l], state_conv[l], state_h[l], wts, f"s{l}")
        outs_p.append(st_p)
        outs_s.append(st_s)
    stack = lambda outs, i: jnp.stack([o[i] for o in outs])
    return (y_p, y_s,
            stack(outs_p, 0), stack(outs_p, 1), stack(outs_p, 2), stack(outs_p, 3),
            stack(outs_s, 0), stack(outs_s, 1), stack(outs_s, 2), stack(outs_s, 3))
```

```python
import functools

import jax
import jax.numpy as jnp
from jax import lax
from jax.experimental import pallas as pl
from jax.experimental.pallas import tpu as pltpu

F32 = jnp.float32
BF16 = jnp.bfloat16

CHUNK = 64
PAST_CHUNKS = 8
KV_REACH = PAST_CHUNKS * CHUNK
N_HEADS = 16
HEAD_DIM = 64
D_ATTN = N_HEADS * HEAD_DIM
MAX_REL = 256
RNN_BLOCK = 128
CONV_WIDTH = 4
LRU_C = 8.0
EPS = 1e-6
NEG_INF = -1e30

LANES = 128
SUBLANES = 8
MIB = 1024 * 1024

ROW_TILE = 1024
COL_TILE = 1024
WIDE_ROW_TILE = 512
FFN_TILE = 1024
RNN_FRAMES = 256
HEADS_PER_STEP = 4
CACHED_HEADS_PER_STEP = 2
SAMPLE_SEQS = 8
KT_BLOCK = 512
ROW_CHUNKS = 4
VMEM_PROJ, VMEM_FUSED, VMEM_ATTN, VMEM_RNN, VMEM_MLP = 52, 58, 40, 40, 48


def _params(n_axes, vmem_mib):
    return pltpu.CompilerParams(
        dimension_semantics=("arbitrary",) * n_axes,
        vmem_limit_bytes=vmem_mib * MIB,
    )


def _rmsnorm_rows(x, g):
    ms = jnp.mean(x * x, axis=-1, keepdims=True)
    return x * lax.rsqrt(ms + EPS) * g


def _in_proj_body(x_ref, g_ref, w_ref, o_ref, s_ref, xn_ref, *, period, j0, j1):
    i, j = pl.program_id(0), pl.program_id(1)

    @pl.when(j == 0)
    def _():
        xn_ref[...] = _rmsnorm_rows(x_ref[...], g_ref[...]).astype(BF16)

    keep_f32 = (i % period == period - 1) & (j >= j0) & (j < j1)

    @pl.when(keep_f32)
    def _():
        acc = jnp.dot(xn_ref[...], w_ref[...], preferred_element_type=F32)
        o_ref[...] = acc.astype(o_ref.dtype)
        s_ref[...] = acc[acc.shape[0] - s_ref.shape[0]:, :]

    @pl.when(jnp.logical_not(keep_f32))
    def _():
        o_ref[...] = jnp.dot(xn_ref[...], w_ref[...], preferred_element_type=F32).astype(o_ref.dtype)


def _in_proj(x2d, g, w, *, tm, tn, period, state_rows, state_col0, state_cols, name):
    m, d = x2d.shape
    n = w.shape[1]
    j0, j1 = state_col0 // tn, (state_col0 + state_cols) // tn
    n_state_tiles = m // tm // period

    def state_block(i, j):
        live = i % period == period - 1
        return i // period, jnp.where(live, jnp.clip(j - j0, 0, j1 - j0 - 1), 0)

    return pl.pallas_call(
        functools.partial(_in_proj_body, period=period, j0=j0, j1=j1),
        grid=(m // tm, n // tn),
        in_specs=[
            pl.BlockSpec((tm, d), lambda i, j: (i, 0)),
            pl.BlockSpec((1, d), lambda i, j: (0, 0)),
            pl.BlockSpec((d, tn), lambda i, j: (0, j)),
        ],
        out_specs=[
            pl.BlockSpec((tm, tn), lambda i, j: (i, j)),
            pl.BlockSpec((state_rows, tn), state_block),
        ],
        out_shape=[
            jax.ShapeDtypeStruct((m, n), BF16),
            jax.ShapeDtypeStruct((n_state_tiles * state_rows, state_cols), F32),
        ],
        scratch_shapes=[pltpu.VMEM((tm, d), BF16)],
        compiler_params=_params(2, VMEM_PROJ),
        name=name,
    )(x2d, g, w)


def _bias_diagonals(rel_bias, width, offsets, heads):
    x = jnp.arange(width)
    rows = [rel_bias[:, jnp.clip(off + CHUNK - 1 - x, -MAX_REL, MAX_REL) + MAX_REL] for off in offsets]
    g = jnp.stack(rows, axis=1).astype(F32)
    n_groups = rel_bias.shape[0] // heads
    g = g.reshape(n_groups, heads, len(offsets), width)
    return jnp.swapaxes(g, 1, 2).reshape(n_groups, len(offsets) * heads, width)


def _attn_body(*refs, bb, n_chunks, kw, heads, offsets, unit, cached):
    if cached:
        q_ref, kc_ref, vc_ref, k_ref, v_ref, g_ref, o_ref, tab_ref, s_ref, p_ref, l_ref = refs
        band = lambda c_ref, n_ref, bi, ks: jnp.concatenate([c_ref[bi], n_ref[bi]], axis=0)
    else:
        q_ref, k_ref, v_ref, g_ref, o_ref, tab_ref, s_ref, p_ref, l_ref, kt_ref = refs
        kc_ref = vc_ref = None
        band = lambda c_ref, n_ref, bi, ks: n_ref[bi, pl.ds(ks, kw), :]
    width = g_ref.shape[2]
    hw = heads * HEAD_DIM

    @pl.when(pl.program_id(1) == 0)
    def _():
        r = lax.broadcasted_iota(jnp.int32, (CHUNK, kw), 0)
        j = lax.broadcasted_iota(jnp.int32, (CHUNK, kw), 1)
        for c, off in enumerate(offsets):
            cq = (r + off) // CHUNK
            ck = j // CHUNK
            valid = (ck <= cq) & (ck >= cq - PAST_CHUNKS)
            for h in range(heads):
                g = jnp.broadcast_to(g_ref[0, c * heads + h:c * heads + h + 1, :], (CHUNK, width))
                t = pltpu.roll(g, width - (CHUNK - 1), 1, stride=1, stride_axis=0)
                tab_ref[c, h * CHUNK:(h + 1) * CHUNK, :] = jnp.where(valid, t[:, :kw], NEG_INF)

    if not cached:
        for r0 in range(0, k_ref.shape[1], KT_BLOCK):
            kt_ref[:, r0:r0 + KT_BLOCK] = k_ref[0, r0:r0 + KT_BLOCK, :].T

    lane = lax.broadcasted_iota(jnp.int32, (1, hw), 1)
    head_lanes = [(lane >= h * HEAD_DIM) & (lane < (h + 1) * HEAD_DIM) for h in range(heads)]
    rows = heads * CHUNK
    n_units = bb * n_chunks // unit

    def locate(u, j):
        idx = u * unit + j
        bi, c = (0, idx) if bb == 1 else (idx // n_chunks, idx % n_chunks)
        q0 = pl.multiple_of(c * CHUNK, CHUNK)
        if cached:
            return bi, 0, q0, 0
        past = jnp.maximum(c - PAST_CHUNKS, 0)
        odd = past % 2
        ks = pl.multiple_of((past - odd) * CHUNK, 2 * CHUNK)
        return bi, jnp.minimum(c, PAST_CHUNKS) + odd, q0, ks

    def scores(u, slot):
        for j in range(unit):
            bi, c, q0, ks = locate(u, j)
            q = q_ref[bi, pl.ds(q0, CHUNK), :]
            qs = jnp.concatenate([jnp.where(m, q, jnp.zeros_like(q)) for m in head_lanes], axis=0)
            qs = qs * jnp.asarray(HEAD_DIM ** -0.5, q.dtype)
            if cached:
                kb = band(kc_ref, k_ref, bi, ks)
                s = lax.dot_general(qs, kb, (((1,), (1,)), ((), ())), preferred_element_type=F32)
            else:
                s = jnp.dot(qs, kt_ref[:, pl.ds(ks, kw)], preferred_element_type=F32)
            s_ref[slot, j * rows:(j + 1) * rows, :] = s + tab_ref[c]

    def softmax(slot):
        s = s_ref[slot]
        p = jnp.exp(s - jnp.max(s, axis=-1, keepdims=True))
        p_ref[slot] = p.astype(p_ref.dtype)
        l_ref[slot] = 1.0 / jnp.sum(p, axis=-1, keepdims=True)

    def output(u, slot):
        for j in range(unit):
            bi, c, q0, ks = locate(u, j)
            vb = band(vc_ref, v_ref, bi, ks)
            o_all = jnp.dot(p_ref[slot, j * rows:(j + 1) * rows, :], vb, preferred_element_type=F32)
            o_all = o_all * l_ref[slot, j * rows:(j + 1) * rows, :]
            o = o_all[0:CHUNK]
            for h in range(1, heads):
                o = jnp.where(head_lanes[h], o_all[h * CHUNK:(h + 1) * CHUNK], o)
            o_ref[bi, pl.ds(q0, CHUNK), :] = o.astype(o_ref.dtype)

    def step(i, slot, do_scores, do_softmax, do_output):
        if do_scores:
            scores(i, slot)
        if do_output:
            output(i - 2, slot)
        if do_softmax:
            softmax(1 - slot)

    for i in range(2):
        step(i, i % 2, i < n_units, 1 <= i <= n_units, False)
    n_pairs = max(n_units - 2, 0) // 2
    if n_pairs:
        def body(t, carry):
            i = 2 + 2 * t
            step(i, 0, True, True, True)
            step(i + 1, 1, True, True, True)
            return carry
        lax.fori_loop(0, n_pairs, body, 0)
    for i in range(2 + 2 * n_pairs, n_units + 2):
        step(i, i % 2, i < n_units, 1 <= i <= n_units, i >= 2)


def _attention(q_arr, k_arr, v_arr, rel_bias, *, q_col0, k_col0, v_col0, bb, heads, offsets, name,
               k_cache=None, v_cache=None):
    b, sq, _ = q_arr.shape
    sk = k_arr.shape[1]
    cached = k_cache is not None
    if cached:
        kw = KV_REACH + CHUNK
        assert sq == CHUNK and sk == CHUNK and k_cache.shape[1] == KV_REACH and len(offsets) == 1
    else:
        kw = KV_REACH + 2 * CHUNK
        assert bb == 1 and sq == sk and sk % KT_BLOCK == 0 and sk >= kw and len(offsets) == PAST_CHUNKS + 2
    hw = heads * HEAD_DIM
    n_groups = N_HEADS // heads
    n_case = len(offsets)
    width = -(-(kw + CHUNK - 1) // LANES) * LANES
    diag = _bias_diagonals(rel_bias, width, offsets, heads)
    qc, kc, vc = q_col0 // hw, k_col0 // hw, v_col0 // hw
    unit = 2
    assert (bb * sq // CHUNK) % unit == 0
    rows = unit * heads * CHUNK
    cache_specs = [pl.BlockSpec((bb, KV_REACH, hw), lambda g, i: (i, 0, g))] * 2 if cached else []
    cache_args = [k_cache, v_cache] if cached else []
    return pl.pallas_call(
        functools.partial(_attn_body, bb=bb, n_chunks=sq // CHUNK, kw=kw, heads=heads,
                          offsets=tuple(offsets), unit=unit, cached=cached),
        grid=(n_groups, b // bb),
        in_specs=[pl.BlockSpec((bb, sq, hw), lambda g, i: (i, 0, qc + g))] + cache_specs + [
            pl.BlockSpec((bb, sk, hw), lambda g, i: (i, 0, kc + g)),
            pl.BlockSpec((bb, sk, hw), lambda g, i: (i, 0, vc + g)),
            pl.BlockSpec((1, n_case * heads, width), lambda g, i: (g, 0, 0)),
        ],
        out_specs=pl.BlockSpec((bb, sq, hw), lambda g, i: (i, 0, g)),
        out_shape=jax.ShapeDtypeStruct((b, sq, D_ATTN), BF16),
        scratch_shapes=[
            pltpu.VMEM((n_case, heads * CHUNK, kw), F32),
            pltpu.VMEM((2, rows, kw), F32),
            pltpu.VMEM((2, rows, kw), BF16),
            pltpu.VMEM((2, rows, 1), F32),
        ] + ([] if cached else [pltpu.VMEM((hw, sk), BF16)]),
        compiler_params=_params(2, VMEM_ATTN),
        name=name,
    )(q_arr, *cache_args, k_arr, v_arr, diag)


def _log_sigmoid(x):
    return jnp.minimum(x, 0.0) - jnp.log1p(jnp.exp(-jnp.abs(x)))


def _shift_matrix(tt):
    t = jnp.arange(tt)
    blocks = [(t[:, None] - d) == t[None, :] for d in range(CONV_WIDTH - 1, 0, -1)]
    return jnp.concatenate(blocks, axis=0).astype(BF16)


def _rglru_head(x_bf, prev8, shift, conv_w, conv_b, wg_ref, b_rg, b_ig):
    tt, cw = x_bf.shape
    taps = CONV_WIDTH - 1
    x = x_bf.astype(F32)
    row = lax.broadcasted_iota(jnp.int32, (SUBLANES, cw), 0)
    if shift is not None:
        sh = jnp.dot(shift, x_bf, preferred_element_type=F32)
        xc = conv_b + conv_w[0:1] * sh[0:tt]
        for k in range(1, taps):
            xc = xc + conv_w[k:k + 1] * sh[k * tt:(k + 1) * tt]
        xc = xc + conv_w[taps:taps + 1] * x
        head = xc[0:SUBLANES]
        for k in range(taps):
            d = taps - k
            head = head + conv_w[k:k + 1] * jnp.where(row < d, pltpu.roll(prev8, d, 0), 0.0)
        xc = jnp.concatenate([head, xc[SUBLANES:]], axis=0)
    else:
        xc = conv_b
        for k in range(taps):
            d = taps - k
            sh = pltpu.roll(x, d, 0)
            head = jnp.where(row < d, pltpu.roll(prev8, d, 0), sh[0:SUBLANES])
            xc = xc + conv_w[k:k + 1] * jnp.concatenate([head, sh[SUBLANES:]], axis=0)
        xc = xc + conv_w[taps:taps + 1] * x

    xcb = xc.astype(BF16)
    r_parts, i_parts = [], []
    for n in range(cw // RNN_BLOCK):
        gts = jnp.dot(xcb[:, n * RNN_BLOCK:(n + 1) * RNN_BLOCK], wg_ref[n], preferred_element_type=F32)
        r_parts.append(gts[:, :RNN_BLOCK])
        i_parts.append(gts[:, RNN_BLOCK:])
    r_pre = jnp.concatenate(r_parts, axis=1) + b_rg
    i_pre = jnp.concatenate(i_parts, axis=1) + b_ig
    return xc, r_pre, i_pre, x[tt - SUBLANES:tt]


def _rglru_tail(xc, r_pre, i_pre, gr_bf, h_in, lam):
    rows, cw = xc.shape
    r = jax.nn.sigmoid(r_pre)
    i = jax.nn.sigmoid(i_pre)
    log_a = r * (LRU_C * _log_sigmoid(lam))
    a = jnp.exp(log_a)
    w = -jnp.tanh(log_a)
    one_minus_a2 = 2.0 * w / (1.0 + w)
    root = jnp.where(one_minus_a2 > 0.0, one_minus_a2 * lax.rsqrt(one_minus_a2), 0.0)
    u = root * (i * xc)

    row = lax.broadcasted_iota(jnp.int32, (SUBLANES, cw), 0)
    h = h_in
    hs = []
    for g in range(rows // SUBLANES):
        ag = a[g * SUBLANES:(g + 1) * SUBLANES]
        ug = u[g * SUBLANES:(g + 1) * SUBLANES]
        for d in (1, 2, 4):
            keep = row >= d
            a_prev = jnp.where(keep, pltpu.roll(ag, d, 0), 1.0)
            u_prev = jnp.where(keep, pltpu.roll(ug, d, 0), 0.0)
            ug = ag * u_prev + ug
            ag = ag * a_prev
        hg = ug + ag * h
        hs.append(hg)
        h = jnp.broadcast_to(hg[SUBLANES - 1:SUBLANES, :], (SUBLANES, cw))
    gate = jax.nn.gelu(gr_bf.astype(F32), approximate=True)
    return (jnp.concatenate(hs, axis=0) * gate).astype(BF16), h


def _rglru_body(xr_ref, gr_ref, c0_ref, h0_ref, sh_ref, cw_ref, cb_ref, wg_ref, brg_ref, big_ref, lam_ref,
                ob_ref, hl_ref, prev_ref, hcar_ref):
    @pl.when(pl.program_id(2) == 0)
    def _():
        hcar_ref[...] = jnp.broadcast_to(h0_ref[0], hcar_ref.shape)
        prev_ref[...] = c0_ref[0]

    xc, r_pre, i_pre, prev8 = _rglru_head(xr_ref[0], prev_ref[...], sh_ref[...], cw_ref[...], cb_ref[...],
                                          wg_ref, brg_ref[...], big_ref[...])
    out, h = _rglru_tail(xc, r_pre, i_pre, gr_ref[0], hcar_ref[...], lam_ref[...])
    ob_ref[0] = out
    prev_ref[...] = prev8
    hcar_ref[...] = h
    hl_ref[0] = h[0:1, :]


def _rglru(z3, conv0, h0, conv_w, conv_b, w_gates, b_rg, b_ig, lam, *, xr_col0, gr_col0, tt, cw, name):
    b, t, _ = z3.shape
    d_rnn = conv_w.shape[1]
    n_cg = d_rnn // cw
    blocks_per_cg = cw // RNN_BLOCK
    xc0, gc0 = xr_col0 // cw, gr_col0 // cw
    vec = pl.BlockSpec((1, cw), lambda c, i, s: (0, c))
    return pl.pallas_call(
        _rglru_body,
        grid=(n_cg, b, t // tt),
        in_specs=[
            pl.BlockSpec((1, tt, cw), lambda c, i, s: (i, s, xc0 + c)),
            pl.BlockSpec((1, tt, cw), lambda c, i, s: (i, s, gc0 + c)),
            pl.BlockSpec((1, SUBLANES, cw), lambda c, i, s: (i, 0, c)),
            pl.BlockSpec((1, 1, cw), lambda c, i, s: (i, 0, c)),
            pl.BlockSpec(((CONV_WIDTH - 1) * tt, tt), lambda c, i, s: (0, 0)),
            pl.BlockSpec((CONV_WIDTH, cw), lambda c, i, s: (0, c)),
            vec,
            pl.BlockSpec((blocks_per_cg, RNN_BLOCK, 2 * RNN_BLOCK), lambda c, i, s: (c, 0, 0)),
            vec, vec, vec,
        ],
        out_specs=[
            pl.BlockSpec((1, tt, cw), lambda c, i, s: (i, s, c)),
            pl.BlockSpec((1, 1, cw), lambda c, i, s: (i, 0, c)),
        ],
        out_shape=[
            jax.ShapeDtypeStruct((b, t, d_rnn), BF16),
            jax.ShapeDtypeStruct((b, 1, d_rnn), F32),
        ],
        scratch_shapes=[
            pltpu.VMEM((SUBLANES, cw), F32),
            pltpu.VMEM((SUBLANES, cw), F32),
        ],
        compiler_params=_params(3, VMEM_RNN),
        name=name,
    )(z3, z3, conv0, h0, _shift_matrix(tt), conv_w, conv_b, w_gates, b_rg, b_ig, lam)


def _lookup(table, j):
    out = jnp.int32(table[-1])
    for k in range(len(table) - 2, -1, -1):
        out = jnp.where(j == k, jnp.int32(table[k]), out)
    return out


def _fused_body(x_ref, g_ref, w_ref, c0_ref, h0_ref, cw_ref, cb_ref, wg_ref, brg_ref, big_ref,
                lam_ref, z_ref, s_ref, ob_ref, hl_ref, xn_ref, stash_ref, prev_ref, hcar_ref, hd_ref, gr_ref, *,
                n_tiles, period, n_cg, pieces_per_cg, tt, n_split, stash_slots):
    i, j = pl.program_id(0), pl.program_id(1)
    cw = ob_ref.shape[1]

    @pl.when((j == 0) & (i < n_tiles))
    def _():
        xn_ref[...] = _rmsnorm_rows(x_ref[...], g_ref[...]).astype(BF16)

    @pl.when((j == 0) & (i == 0))
    def _():
        prev_ref[...] = jnp.zeros_like(prev_ref)
        hcar_ref[...] = jnp.zeros_like(hcar_ref)

    tm = xn_ref.shape[0]
    rows_s = tt // n_split
    state_start = tm - s_ref.shape[0]
    first_rows = tm // 4
    rest_rows = (tm - first_rows) // n_split
    bounds = [0] + [first_rows + c * rest_rows for c in range(n_split + 1)]

    def project(c):
        lo, hi = bounds[c], bounds[c + 1]
        acc = jnp.dot(xn_ref[lo:hi, :], w_ref[...], preferred_element_type=F32)
        zb = acc.astype(BF16)
        z_ref[lo:hi, :] = zb
        if hi > state_start:
            keep = max(lo, state_start)
            s_ref[keep - state_start:hi - state_start, :] = acc[keep - lo:, :]
        stash_ref[_lookup(stash_slots, j), lo:hi, :] = zb
        return acc[hi - lo - SUBLANES:, :]

    def after(state, acc_rows):
        cg, lam, h = state
        zero = (pltpu.bitcast(acc_rows, jnp.uint32) >> 16) >> 16
        return cg, lam, pltpu.bitcast(pltpu.bitcast(h, jnp.uint32) + zero, F32)

    def recur_load():
        cg, tp = j // pieces_per_cg, j % pieces_per_cg
        r0 = pl.multiple_of(tp * tt, tt)
        first = ((i - 1) % period == 0) & (tp == 0)
        prev8 = jnp.where(first, c0_ref[0, cg], prev_ref[cg])
        h = jnp.where(first, jnp.broadcast_to(h0_ref[0, cg], (SUBLANES, cw)), hcar_ref[cg])
        gr_ref[...] = stash_ref[n_cg + cg, pl.ds(r0, tt), :]
        return cg, stash_ref[cg, pl.ds(r0, tt), :], prev8, h

    def recur_head(loaded):
        cg, x_bf, prev8, h = loaded
        xc, r_pre, i_pre, prev8 = _rglru_head(x_bf, prev8, None, cw_ref[cg], cb_ref[cg], wg_ref.at[cg],
                                              brg_ref[cg], big_ref[cg])
        hd_ref[0] = xc
        hd_ref[1] = r_pre
        hd_ref[2] = i_pre
        prev_ref[cg] = prev8
        return cg, lam_ref[cg], h

    def recur_tail(k, state):
        cg, lam, h = state
        lo = k * rows_s
        out, h = _rglru_tail(hd_ref[0, lo:lo + rows_s, :], hd_ref[1, lo:lo + rows_s, :],
                             hd_ref[2, lo:lo + rows_s, :], gr_ref[lo:lo + rows_s, :], h, lam)
        ob_ref[lo:lo + rows_s, :] = out
        return cg, lam, h

    def recur_end(state):
        cg, _, h = state
        hcar_ref[cg] = h
        hl_ref[0, 0] = h[0:1, :]

    has_proj = i < n_tiles
    has_rec = (i >= 1) & (j < n_cg * pieces_per_cg)

    @pl.when(has_proj & has_rec)
    def _():
        state = after(recur_head(recur_load()), project(0))
        for k in range(n_split):
            state = recur_tail(k, state)
            acc_rows = project(k + 1)
            if k + 1 < n_split:
                state = after(state, acc_rows)
        recur_end(state)

    @pl.when(has_proj & jnp.logical_not(has_rec))
    def _():
        for c in range(n_split + 1):
            project(c)

    @pl.when(jnp.logical_not(has_proj) & has_rec)
    def _():
        state = recur_head(recur_load())
        for k in range(n_split):
            state = recur_tail(k, state)
        recur_end(state)


def _in_proj_rglru(x2d, g, w, conv0, h0, conv_w, conv_b, w_gates, b_rg, b_ig, lam, *, tm, tn, tt, period,
                   state_rows, state_col0, state_cols, xr_col0, gr_col0, name):
    m, d = x2d.shape
    n = w.shape[1]
    d_rnn = conv_w.shape[1]
    n_tiles, n_col, n_cg = m // tm, n // tn, d_rnn // tn
    n_seq = n_tiles // period
    pieces_per_cg = tm // tt
    n_pieces = n_cg * pieces_per_cg
    xr0, gr0 = xr_col0 // tn, gr_col0 // tn
    assert n_pieces + n_cg <= n_col and tm % tt == 0 and n_tiles % period == 0

    order, stash_slots = [None] * n_col, [2 * n_cg] * n_col
    for c in range(n_cg):
        jx = (c + 1) * pieces_per_cg
        order[jx], stash_slots[jx] = xr0 + c, c
        order[jx + 1], stash_slots[jx + 1] = gr0 + c, n_cg + c
    rest = [c for c in range(n_col) if c not in order]
    for jj in range(n_col):
        if order[jj] is None:
            order[jj] = rest.pop(0)
    s0, n_state = state_col0 // tn, state_cols // tn
    state_pos = [order.index(s0 + k) for k in range(n_state)]
    assert state_pos == sorted(state_pos)
    state_blk = [sum(p < jj for p in state_pos) for jj in range(n_col)]

    last = lambda i: i == n_tiles
    row = lambda i: jnp.minimum(i, n_tiles - 1)
    col = lambda i, j: jnp.where(last(i), order[-1], _lookup(order, j))
    seq_prev = lambda i: jnp.maximum(i - 1, 0) // period
    piece = lambda i, j: jnp.where(i == 0, 0, jnp.minimum(j, n_pieces - 1))

    def state_block(i, j):
        live = i % period == period - 1
        blk = jnp.where(live, _lookup(state_blk, j), n_state + 1)
        return row(i) // period, jnp.where(last(i), n_state, blk)

    per_cg = lambda a: a.reshape(a.shape[0], n_cg, tn).swapaxes(0, 1)
    whole = lambda a: pl.BlockSpec(a.shape, lambda i, j: (0,) * a.ndim)
    n_split = 4
    assert tt % (n_split * SUBLANES) == 0 and (tm - tm // 4) % (2 * SUBLANES * n_split) == 0
    params = [per_cg(conv_w), per_cg(conv_b), w_gates.reshape(n_cg, -1, RNN_BLOCK, 2 * RNN_BLOCK),
              per_cg(b_rg), per_cg(b_ig), per_cg(lam)]
    conv0 = conv0.reshape(n_seq, SUBLANES, n_cg, tn).swapaxes(1, 2)
    h0 = h0.reshape(n_seq, 1, n_cg, tn).swapaxes(1, 2)

    z, state, o_b, h_last = pl.pallas_call(
        functools.partial(_fused_body, n_tiles=n_tiles, period=period, n_cg=n_cg,
                          pieces_per_cg=pieces_per_cg, tt=tt, n_split=n_split,
                          stash_slots=tuple(stash_slots)),
        grid=(n_tiles + 1, n_col),
        in_specs=[
            pl.BlockSpec((tm, d), lambda i, j: (row(i), 0)),
            pl.BlockSpec((1, d), lambda i, j: (0, 0)),
            pl.BlockSpec((d, tn), lambda i, j: (0, col(i, j))),
            pl.BlockSpec((1, n_cg, SUBLANES, tn), lambda i, j: (seq_prev(i), 0, 0, 0)),
            pl.BlockSpec((1, n_cg, 1, tn), lambda i, j: (seq_prev(i), 0, 0, 0)),
        ] + [whole(p) for p in params],
        out_specs=[
            pl.BlockSpec((tm, tn), lambda i, j: (row(i), col(i, j))),
            pl.BlockSpec((state_rows, tn), state_block),
            pl.BlockSpec((tt, tn), lambda i, j: (jnp.maximum(i - 1, 0) * pieces_per_cg
                                                 + piece(i, j) % pieces_per_cg, piece(i, j) // pieces_per_cg)),
            pl.BlockSpec((1, 1, 1, tn), lambda i, j: (jnp.maximum(i - 1, 0), piece(i, j) // pieces_per_cg, 0, 0)),
        ],
        out_shape=[
            jax.ShapeDtypeStruct((m, n), BF16),
            jax.ShapeDtypeStruct((n_seq * state_rows, (n_state + 2) * tn), F32),
            jax.ShapeDtypeStruct((m, d_rnn), BF16),
            jax.ShapeDtypeStruct((n_tiles, n_cg, 1, tn), F32),
        ],
        scratch_shapes=[
            pltpu.VMEM((tm, d), BF16),
            pltpu.VMEM((2 * n_cg + 1, tm, tn), BF16),
            pltpu.VMEM((n_cg, SUBLANES, tn), F32),
            pltpu.VMEM((n_cg, SUBLANES, tn), F32),
            pltpu.VMEM((3, tt, tn), F32),
            pltpu.VMEM((tt, tn), BF16),
        ],
        compiler_params=_params(2, VMEM_FUSED),
        name=name,
    )(x2d, g, w, conv0, h0, *params)
    return z, state, o_b, h_last.reshape(n_tiles, d_rnn)[period - 1::period]


def _merge_body(oa_ref, ob_ref, ga_ref, gb_ref, wa_ref, wb_ref, o_ref):
    tm = o_ref.shape[0]
    for c in range(ROW_CHUNKS):
        lo, hi = c * tm // ROW_CHUNKS, (c + 1) * tm // ROW_CHUNKS
        ya = jnp.dot(oa_ref[lo:hi, :], wa_ref[...], preferred_element_type=F32)
        yb = jnp.dot(ob_ref[lo:hi, :], wb_ref[...], preferred_element_type=F32)
        ga = jax.nn.sigmoid(ga_ref[lo:hi, :].astype(F32))
        gb = jax.nn.sigmoid(gb_ref[lo:hi, :].astype(F32))
        o_ref[lo:hi, :] = (ga * ya + gb * yb).astype(o_ref.dtype)


def _gate_merge(o_a, o_b, z2, w_a, w_b, *, ga_col0, gb_col0, tm, tn, name):
    m = o_a.shape[0]
    d = w_a.shape[1]
    ga0, gb0 = ga_col0 // tn, gb_col0 // tn
    return pl.pallas_call(
        _merge_body,
        grid=(m // tm, d // tn),
        in_specs=[
            pl.BlockSpec((tm, o_a.shape[1]), lambda i, j: (i, 0)),
            pl.BlockSpec((tm, o_b.shape[1]), lambda i, j: (i, 0)),
            pl.BlockSpec((tm, tn), lambda i, j: (i, ga0 + j)),
            pl.BlockSpec((tm, tn), lambda i, j: (i, gb0 + j)),
            pl.BlockSpec((w_a.shape[0], tn), lambda i, j: (0, j)),
            pl.BlockSpec((w_b.shape[0], tn), lambda i, j: (0, j)),
        ],
        out_specs=pl.BlockSpec((tm, tn), lambda i, j: (i, j)),
        out_shape=jax.ShapeDtypeStruct((m, d), BF16),
        compiler_params=_params(2, VMEM_MLP),
        name=name,
    )(o_a, o_b, z2, z2, w_a, w_b)


def _out_norm_body(m_ref, w_ref, x_ref, g_ref, o_ref):
    y = jnp.dot(m_ref[...], w_ref[...], preferred_element_type=F32)
    o_ref[...] = x_ref[...] + _rmsnorm_rows(y, g_ref[...])


def _out_norm(merged, w_out, x2d, g, *, tm, name):
    m, d = x2d.shape
    return pl.pallas_call(
        _out_norm_body,
        grid=(m // tm,),
        in_specs=[
            pl.BlockSpec((tm, d), lambda i: (i, 0)),
            pl.BlockSpec((d, d), lambda i: (0, 0)),
            pl.BlockSpec((tm, d), lambda i: (i, 0)),
            pl.BlockSpec((1, d), lambda i: (0, 0)),
        ],
        out_specs=pl.BlockSpec((tm, d), lambda i: (i, 0)),
        out_shape=jax.ShapeDtypeStruct((m, d), F32),
        compiler_params=_params(1, VMEM_MLP),
        name=name,
    )(merged, w_out, x2d, g)


def _ffn_body(x_ref, g1_ref, w1_ref, w2_ref, g2_ref, o_ref, xn_ref, *, n_steps, edge_chunks):
    f = pl.program_id(1)

    def rows_chunk(lo, hi, first, last):
        if first:
            xn = _rmsnorm_rows(x_ref[lo:hi, :], g1_ref[...]).astype(BF16)
            xn_ref[lo:hi, :] = xn
        else:
            xn = xn_ref[lo:hi, :]
        hid = jnp.dot(xn, w1_ref[...], preferred_element_type=F32)
        hid = jnp.square(jnp.maximum(hid, 0.0)).astype(BF16)
        acc = jnp.dot(hid, w2_ref[...], preferred_element_type=F32)
        if not first:
            acc = o_ref[lo:hi, :] + acc
        if last:
            acc = x_ref[lo:hi, :] + _rmsnorm_rows(acc, g2_ref[...])
        o_ref[lo:hi, :] = acc

    tm = o_ref.shape[0]
    for first, last in sorted({(k == 0, k == n_steps - 1) for k in range(n_steps)}):
        chunks = edge_chunks if (first or last) else 1

        @pl.when(((f == 0) == first) & ((f == n_steps - 1) == last))
        def _(first=first, last=last, chunks=chunks):
            for c in range(chunks):
                rows_chunk(c * tm // chunks, (c + 1) * tm // chunks, first, last)


def _ffn(x2d, g1, w1, w2, g2, *, tm, tf, name):
    m, d = x2d.shape
    d_ff = w1.shape[1]
    return pl.pallas_call(
        functools.partial(_ffn_body, n_steps=d_ff // tf, edge_chunks=ROW_CHUNKS),
        grid=(m // tm, d_ff // tf),
        in_specs=[
            pl.BlockSpec((tm, d), lambda i, f: (i, 0)),
            pl.BlockSpec((1, d), lambda i, f: (0, 0)),
            pl.BlockSpec((d, tf), lambda i, f: (0, f)),
            pl.BlockSpec((tf, d), lambda i, f: (f, 0)),
            pl.BlockSpec((1, d), lambda i, f: (0, 0)),
        ],
        out_specs=pl.BlockSpec((tm, d), lambda i, f: (i, 0)),
        out_shape=jax.ShapeDtypeStruct((m, d), F32),
        scratch_shapes=[pltpu.VMEM((tm, d), BF16)],
        compiler_params=_params(2, VMEM_MLP),
        name=name,
    )(x2d, g1, w1, w2, g2)


def _pick_tile(n, target):
    t = min(n, target)
    while n % t:
        t //= 2
    return t


def _layer(x, cache_k, cache_v, conv_state, h_state, wts, tag):
    b, t, d = x.shape
    m = b * t
    x2d = x.reshape(m, d)
    d_rnn = wts["conv_w"].shape[1]
    n_in = wts["w_in"].shape[1]
    col_k, col_v, col_xr = D_ATTN, 2 * D_ATTN, 3 * D_ATTN
    col_gr = col_xr + d_rnn
    col_ga = col_gr + d_rnn
    col_gb = col_ga + d

    tm = _pick_tile(m, ROW_TILE)
    tn = COL_TILE
    keep = min(KV_REACH, t)
    if keep == t:
        period, state_rows = 1, tm
    else:
        assert t % tm == 0 and keep <= tm
        period, state_rows = t // tm, keep
    rnn_w = (wts["conv_w"], wts["conv_b"], wts["w_gates"], wts["b_rg"], wts["b_ig"], wts["lru_lambda"])
    if cache_k is None:
        conv0 = jnp.zeros((b, SUBLANES, d_rnn), F32)
        h0 = jnp.zeros((b, 1, d_rnn), F32)
        z2, state, o_b, h_last = _in_proj_rglru(
            x2d, wts["pre_mix_g"], wts["w_in"], conv0, h0, *rnn_w, tm=tm, tn=tn, tt=_pick_tile(tm, RNN_FRAMES),
            period=period, state_rows=state_rows, state_col0=col_k, state_cols=col_gr - col_k,
            xr_col0=col_xr, gr_col0=col_gr, name=f"in_proj_rglru_{tag}")
        z3 = z2.reshape(b, t, n_in)
        offsets = [i * CHUNK for i in range(PAST_CHUNKS + 2)]
        o_a = _attention(z3, z3, z3, wts["rel_bias"], q_col0=0, k_col0=col_k, v_col0=col_v, bb=1,
                         heads=HEADS_PER_STEP, offsets=offsets, name=f"attn_{tag}")
    else:
        z2, state = _in_proj(x2d, wts["pre_mix_g"], wts["w_in"], tm=tm, tn=tn, period=period,
                             state_rows=state_rows, state_col0=col_k, state_cols=col_gr - col_k,
                             name=f"in_proj_{tag}")
        z3 = z2.reshape(b, t, n_in)
        n_cached = cache_k.shape[1]
        assert t == CHUNK and n_cached == KV_REACH
        o_a = _attention(z3, z3, z3, wts["rel_bias"], q_col0=0, k_col0=col_k, v_col0=col_v,
                         bb=_pick_tile(b, SAMPLE_SEQS), heads=CACHED_HEADS_PER_STEP, offsets=[n_cached],
                         name=f"attn_{tag}",
                         k_cache=cache_k.astype(BF16).reshape(b, n_cached, D_ATTN),
                         v_cache=cache_v.astype(BF16).reshape(b, n_cached, D_ATTN))
        conv0 = jnp.pad(conv_state.astype(F32), ((0, 0), (SUBLANES - (CONV_WIDTH - 1), 0), (0, 0)))
        h0 = h_state.astype(F32).reshape(b, 1, d_rnn)
        o_b, h_last = _rglru(z3, conv0, h0, *rnn_w, xr_col0=col_xr, gr_col0=col_gr,
                             tt=_pick_tile(t, RNN_FRAMES), cw=COL_TILE, name=f"rglru_{tag}")
    state = state.reshape(b, keep, -1)
    new_k = state[:, :, :D_ATTN].reshape(b, keep, N_HEADS, HEAD_DIM)
    new_v = state[:, :, D_ATTN:2 * D_ATTN].reshape(b, keep, N_HEADS, HEAD_DIM)
    conv_tail = state[:, keep - (CONV_WIDTH - 1):, 2 * D_ATTN:col_gr - col_k]

    merged = _gate_merge(o_a.reshape(m, D_ATTN), o_b.reshape(m, d_rnn), z2, wts["w_attn_up"],
                         wts["w_rnn_up"], ga_col0=col_ga, gb_col0=col_gb, tm=tm, tn=tn,
                         name=f"gate_merge_{tag}")
    tm2 = _pick_tile(m, WIDE_ROW_TILE)
    x1 = _out_norm(merged, wts["w_out"], x2d, wts["post_mix_g"], tm=tm2, name=f"out_norm_{tag}")
    y = _ffn(x1, wts["pre_ffn_g"], wts["w_ff1"], wts["w_ff2"], wts["post_ffn_g"], tm=tm2, tf=FFN_TILE,
             name=f"ffn_{tag}")
    return y.reshape(b, t, d), new_k, new_v, conv_tail, h_last.reshape(b, d_rnn)


def kernel(x_prompt, x_sample, cache_k, cache_v, state_conv, state_h, pre_mix_g, w_in, rel_bias, conv_w, conv_b, w_rg, b_rg, w_ig, b_ig, lru_lambda, w_attn_up, w_rnn_up, w_out, post_mix_g, pre_ffn_g, w_ff1, w_ff2, post_ffn_g):
    depth = w_in.shape[0]
    y_p, y_s = x_prompt, x_sample
    outs_p, outs_s = [], []
    for l in range(depth):
        wts = {
            "pre_mix_g": pre_mix_g[l][None], "post_mix_g": post_mix_g[l][None],
            "pre_ffn_g": pre_ffn_g[l][None], "post_ffn_g": post_ffn_g[l][None],
            "w_in": w_in[l].astype(BF16), "rel_bias": rel_bias[l],
            "conv_w": conv_w[l], "conv_b": conv_b[l][None],
            "w_gates": jnp.concatenate([w_rg[l], w_ig[l]], axis=-1).astype(BF16),
            "b_rg": b_rg[l][None], "b_ig": b_ig[l][None], "lru_lambda": lru_lambda[l][None],
            "w_attn_up": w_attn_up[l].astype(BF16), "w_rnn_up": w_rnn_up[l].astype(BF16),
            "w_out": w_out[l].astype(BF16), "w_ff1": w_ff1[l].astype(BF16), "w_ff2": w_ff2[l].astype(BF16),
        }
        y_p, *st_p = _layer(y_p, None, None, None, None, wts, f"p{l}")
        y_s, *st_s = _layer(y_s, cache_k[l], cache_v[l], state_conv[l], state_h[l], wts, f"s{l}")
        outs_p.append(st_p)
        outs_s.append(st_s)
    stack = lambda outs, i: jnp.stack([o[i] for o in outs])
    return (y_p, y_s,
            stack(outs_p, 0), stack(outs_p, 1), stack(outs_p, 2), stack(outs_p, 3),
            stack(outs_s, 0), stack(outs_s, 1), stack(outs_s, 2), stack(outs_s, 3))
```

```python
import functools

import jax
import jax.numpy as jnp
from jax import lax
from jax.experimental import pallas as pl
from jax.experimental.pallas import tpu as pltpu

F32 = jnp.float32
BF16 = jnp.bfloat16

CHUNK = 64
PAST_CHUNKS = 8
KV_REACH = PAST_CHUNKS * CHUNK
N_HEADS = 16
HEAD_DIM = 64
D_ATTN = N_HEADS * HEAD_DIM
MAX_REL = 256
RNN_BLOCK = 128
CONV_WIDTH = 4
LRU_C = 8.0
EPS = 1e-6
NEG_INF = -1e30

LANES = 128
SUBLANES = 8
MIB = 1024 * 1024

ROW_TILE = 1024
COL_TILE = 1024
WIDE_ROW_TILE = 512
FFN_TILE = 1024
RNN_FRAMES = 256
HEADS_PER_STEP = 4
CACHED_HEADS_PER_STEP = 2
SAMPLE_SEQS = 8
KT_BLOCK = 512
MERGE_CHUNKS = 4
FFN_EDGE_CHUNKS = 2
VMEM_PROJ, VMEM_FUSED, VMEM_ATTN, VMEM_RNN, VMEM_MLP = 52, 58, 40, 40, 48


def _params(n_axes, vmem_mib):
    return pltpu.CompilerParams(
        dimension_semantics=("arbitrary",) * n_axes,
        vmem_limit_bytes=vmem_mib * MIB,
    )


def _rmsnorm_rows(x, g):
    ms = jnp.mean(x * x, axis=-1, keepdims=True)
    return x * lax.rsqrt(ms + EPS) * g


def _in_proj_body(x_ref, g_ref, w_ref, o_ref, s_ref, xn_ref, *, period, j0, j1):
    i, j = pl.program_id(0), pl.program_id(1)

    @pl.when(j == 0)
    def _():
        xn_ref[...] = _rmsnorm_rows(x_ref[...], g_ref[...]).astype(BF16)

    keep_f32 = (i % period == period - 1) & (j >= j0) & (j < j1)

    @pl.when(keep_f32)
    def _():
        acc = jnp.dot(xn_ref[...], w_ref[...], preferred_element_type=F32)
        o_ref[...] = acc.astype(o_ref.dtype)
        s_ref[...] = acc[acc.shape[0] - s_ref.shape[0]:, :]

    @pl.when(jnp.logical_not(keep_f32))
    def _():
        o_ref[...] = jnp.dot(xn_ref[...], w_ref[...], preferred_element_type=F32).astype(o_ref.dtype)


def _in_proj(x2d, g, w, *, tm, tn, period, state_rows, state_col0, state_cols, name):
    m, d = x2d.shape
    n = w.shape[1]
    j0, j1 = state_col0 // tn, (state_col0 + state_cols) // tn
    n_state_tiles = m // tm // period

    def state_block(i, j):
        live = i % period == period - 1
        return i // period, jnp.where(live, jnp.clip(j - j0, 0, j1 - j0 - 1), 0)

    return pl.pallas_call(
        functools.partial(_in_proj_body, period=period, j0=j0, j1=j1),
        grid=(m // tm, n // tn),
        in_specs=[
            pl.BlockSpec((tm, d), lambda i, j: (i, 0)),
            pl.BlockSpec((1, d), lambda i, j: (0, 0)),
            pl.BlockSpec((d, tn), lambda i, j: (0, j)),
        ],
        out_specs=[
            pl.BlockSpec((tm, tn), lambda i, j: (i, j)),
            pl.BlockSpec((state_rows, tn), state_block),
        ],
        out_shape=[
            jax.ShapeDtypeStruct((m, n), BF16),
            jax.ShapeDtypeStruct((n_state_tiles * state_rows, state_cols), F32),
        ],
        scratch_shapes=[pltpu.VMEM((tm, d), BF16)],
        compiler_params=_params(2, VMEM_PROJ),
        name=name,
    )(x2d, g, w)


def _bias_diagonals(rel_bias, width, offsets, heads):
    x = jnp.arange(width)
    rows = [rel_bias[:, jnp.clip(off + CHUNK - 1 - x, -MAX_REL, MAX_REL) + MAX_REL] for off in offsets]
    g = jnp.stack(rows, axis=1).astype(F32)
    n_groups = rel_bias.shape[0] // heads
    g = g.reshape(n_groups, heads, len(offsets), width)
    return jnp.swapaxes(g, 1, 2).reshape(n_groups, len(offsets) * heads, width)


def _attn_body(*refs, bb, n_chunks, kw, heads, offsets, unit, cached):
    if cached:
        q_ref, kc_ref, vc_ref, k_ref, v_ref, g_ref, o_ref, tab_ref, s_ref, p_ref, l_ref = refs
        band = lambda c_ref, n_ref, bi, ks: jnp.concatenate([c_ref[bi], n_ref[bi]], axis=0)
    else:
        q_ref, k_ref, v_ref, g_ref, o_ref, tab_ref, s_ref, p_ref, l_ref, kt_ref = refs
        kc_ref = vc_ref = None
        band = lambda c_ref, n_ref, bi, ks: n_ref[bi, pl.ds(ks, kw), :]
    width = g_ref.shape[2]
    hw = heads * HEAD_DIM

    @pl.when(pl.program_id(1) == 0)
    def _():
        r = lax.broadcasted_iota(jnp.int32, (CHUNK, kw), 0)
        j = lax.broadcasted_iota(jnp.int32, (CHUNK, kw), 1)
        for c, off in enumerate(offsets):
            cq = (r + off) // CHUNK
            ck = j // CHUNK
            valid = (ck <= cq) & (ck >= cq - PAST_CHUNKS)
            for h in range(heads):
                g = jnp.broadcast_to(g_ref[0, c * heads + h:c * heads + h + 1, :], (CHUNK, width))
                t = pltpu.roll(g, width - (CHUNK - 1), 1, stride=1, stride_axis=0)
                tab_ref[c, h * CHUNK:(h + 1) * CHUNK, :] = jnp.where(valid, t[:, :kw], NEG_INF)

    if not cached:
        for r0 in range(0, k_ref.shape[1], KT_BLOCK):
            kt_ref[:, r0:r0 + KT_BLOCK] = k_ref[0, r0:r0 + KT_BLOCK, :].T

    lane = lax.broadcasted_iota(jnp.int32, (1, hw), 1)
    head_lanes = [(lane >= h * HEAD_DIM) & (lane < (h + 1) * HEAD_DIM) for h in range(heads)]
    rows = heads * CHUNK
    n_units = bb * n_chunks // unit

    def locate(u, j):
        idx = u * unit + j
        bi, c = (0, idx) if bb == 1 else (idx // n_chunks, idx % n_chunks)
        q0 = pl.multiple_of(c * CHUNK, CHUNK)
        if cached:
            return bi, 0, q0, 0
        past = jnp.maximum(c - PAST_CHUNKS, 0)
        odd = past % 2
        ks = pl.multiple_of((past - odd) * CHUNK, 2 * CHUNK)
        return bi, jnp.minimum(c, PAST_CHUNKS) + odd, q0, ks

    def scores(u, slot):
        for j in range(unit):
            bi, c, q0, ks = locate(u, j)
            q = q_ref[bi, pl.ds(q0, CHUNK), :]
            qs = jnp.concatenate([jnp.where(m, q, jnp.zeros_like(q)) for m in head_lanes], axis=0)
            qs = qs * jnp.asarray(HEAD_DIM ** -0.5, q.dtype)
            if cached:
                kb = band(kc_ref, k_ref, bi, ks)
                s = lax.dot_general(qs, kb, (((1,), (1,)), ((), ())), preferred_element_type=F32)
            else:
                s = jnp.dot(qs, kt_ref[:, pl.ds(ks, kw)], preferred_element_type=F32)
            s_ref[slot, j * rows:(j + 1) * rows, :] = s + tab_ref[c]

    def softmax(slot):
        s = s_ref[slot]
        p = jnp.exp(s - jnp.max(s, axis=-1, keepdims=True))
        p_ref[slot] = p.astype(p_ref.dtype)
        l_ref[slot] = 1.0 / jnp.sum(p, axis=-1, keepdims=True)

    def output(u, slot):
        for j in range(unit):
            bi, c, q0, ks = locate(u, j)
            vb = band(vc_ref, v_ref, bi, ks)
            o_all = jnp.dot(p_ref[slot, j * rows:(j + 1) * rows, :], vb, preferred_element_type=F32)
            o_all = o_all * l_ref[slot, j * rows:(j + 1) * rows, :]
            o = o_all[0:CHUNK]
            for h in range(1, heads):
                o = jnp.where(head_lanes[h], o_all[h * CHUNK:(h + 1) * CHUNK], o)
            o_ref[bi, pl.ds(q0, CHUNK), :] = o.astype(o_ref.dtype)

    def step(i, slot, do_scores, do_softmax, do_output):
        if do_scores:
            scores(i, slot)
        if do_output:
            output(i - 2, slot)
        if do_softmax:
            softmax(1 - slot)

    for i in range(2):
        step(i, i % 2, i < n_units, 1 <= i <= n_units, False)
    n_pairs = max(n_units - 2, 0) // 2
    if n_pairs:
        def body(t, carry):
            i = 2 + 2 * t
            step(i, 0, True, True, True)
            step(i + 1, 1, True, True, True)
            return carry
        lax.fori_loop(0, n_pairs, body, 0)
    for i in range(2 + 2 * n_pairs, n_units + 2):
        step(i, i % 2, i < n_units, 1 <= i <= n_units, i >= 2)


def _attention(q_arr, k_arr, v_arr, rel_bias, *, q_col0, k_col0, v_col0, bb, heads, offsets, name,
               k_cache=None, v_cache=None):
    b, sq, _ = q_arr.shape
    sk = k_arr.shape[1]
    cached = k_cache is not None
    if cached:
        kw = KV_REACH + CHUNK
        assert sq == CHUNK and sk == CHUNK and k_cache.shape[1] == KV_REACH and len(offsets) == 1
    else:
        kw = KV_REACH + 2 * CHUNK
        assert bb == 1 and sq == sk and sk % KT_BLOCK == 0 and sk >= kw and len(offsets) == PAST_CHUNKS + 2
    hw = heads * HEAD_DIM
    n_groups = N_HEADS // heads
    n_case = len(offsets)
    width = -(-(kw + CHUNK - 1) // LANES) * LANES
    diag = _bias_diagonals(rel_bias, width, offsets, heads)
    qc, kc, vc = q_col0 // hw, k_col0 // hw, v_col0 // hw
    unit = 2
    assert (bb * sq // CHUNK) % unit == 0
    rows = unit * heads * CHUNK
    cache_specs = [pl.BlockSpec((bb, KV_REACH, hw), lambda g, i: (i, 0, g))] * 2 if cached else []
    cache_args = [k_cache, v_cache] if cached else []
    return pl.pallas_call(
        functools.partial(_attn_body, bb=bb, n_chunks=sq // CHUNK, kw=kw, heads=heads,
                          offsets=tuple(offsets), unit=unit, cached=cached),
        grid=(n_groups, b // bb),
        in_specs=[pl.BlockSpec((bb, sq, hw), lambda g, i: (i, 0, qc + g))] + cache_specs + [
            pl.BlockSpec((bb, sk, hw), lambda g, i: (i, 0, kc + g)),
            pl.BlockSpec((bb, sk, hw), lambda g, i: (i, 0, vc + g)),
            pl.BlockSpec((1, n_case * heads, width), lambda g, i: (g, 0, 0)),
        ],
        out_specs=pl.BlockSpec((bb, sq, hw), lambda g, i: (i, 0, g)),
        out_shape=jax.ShapeDtypeStruct((b, sq, D_ATTN), BF16),
        scratch_shapes=[
            pltpu.VMEM((n_case, heads * CHUNK, kw), F32),
            pltpu.VMEM((2, rows, kw), F32),
            pltpu.VMEM((2, rows, kw), BF16),
            pltpu.VMEM((2, rows, 1), F32),
        ] + ([] if cached else [pltpu.VMEM((hw, sk), BF16)]),
        compiler_params=_params(2, VMEM_ATTN),
        name=name,
    )(q_arr, *cache_args, k_arr, v_arr, diag)


def _log_sigmoid(x):
    return jnp.minimum(x, 0.0) - jnp.log1p(jnp.exp(-jnp.abs(x)))


def _shift_matrix(tt):
    t = jnp.arange(tt)
    blocks = [(t[:, None] - d) == t[None, :] for d in range(CONV_WIDTH - 1, 0, -1)]
    return jnp.concatenate(blocks, axis=0).astype(BF16)


def _rglru_head(x_bf, prev8, shift, conv_w, conv_b, wg_ref, b_rg, b_ig):
    tt, cw = x_bf.shape
    taps = CONV_WIDTH - 1
    x = x_bf.astype(F32)
    row = lax.broadcasted_iota(jnp.int32, (SUBLANES, cw), 0)
    if shift is not None:
        sh = jnp.dot(shift, x_bf, preferred_element_type=F32)
        xc = conv_b + conv_w[0:1] * sh[0:tt]
        for k in range(1, taps):
            xc = xc + conv_w[k:k + 1] * sh[k * tt:(k + 1) * tt]
        xc = xc + conv_w[taps:taps + 1] * x
        head = xc[0:SUBLANES]
        for k in range(taps):
            d = taps - k
            head = head + conv_w[k:k + 1] * jnp.where(row < d, pltpu.roll(prev8, d, 0), 0.0)
        xc = jnp.concatenate([head, xc[SUBLANES:]], axis=0)
    else:
        xc = conv_b
        for k in range(taps):
            d = taps - k
            sh = pltpu.roll(x, d, 0)
            head = jnp.where(row < d, pltpu.roll(prev8, d, 0), sh[0:SUBLANES])
            xc = xc + conv_w[k:k + 1] * jnp.concatenate([head, sh[SUBLANES:]], axis=0)
        xc = xc + conv_w[taps:taps + 1] * x

    xcb = xc.astype(BF16)
    r_parts, i_parts = [], []
    for n in range(cw // RNN_BLOCK):
        gts = jnp.dot(xcb[:, n * RNN_BLOCK:(n + 1) * RNN_BLOCK], wg_ref[n], preferred_element_type=F32)
        r_parts.append(gts[:, :RNN_BLOCK])
        i_parts.append(gts[:, RNN_BLOCK:])
    r_pre = jnp.concatenate(r_parts, axis=1) + b_rg
    i_pre = jnp.concatenate(i_parts, axis=1) + b_ig
    return xc, r_pre, i_pre, x[tt - SUBLANES:tt]


def _rglru_tail(xc, r_pre, i_pre, gr_bf, h_in, lam):
    rows, cw = xc.shape
    r = jax.nn.sigmoid(r_pre)
    i = jax.nn.sigmoid(i_pre)
    log_a = r * (LRU_C * _log_sigmoid(lam))
    a = jnp.exp(log_a)
    w = -jnp.tanh(log_a)
    one_minus_a2 = 2.0 * w / (1.0 + w)
    root = jnp.where(one_minus_a2 > 0.0, one_minus_a2 * lax.rsqrt(one_minus_a2), 0.0)
    u = root * (i * xc)

    row = lax.broadcasted_iota(jnp.int32, (SUBLANES, cw), 0)
    h = h_in
    hs = []
    for g in range(rows // SUBLANES):
        ag = a[g * SUBLANES:(g + 1) * SUBLANES]
        ug = u[g * SUBLANES:(g + 1) * SUBLANES]
        for d in (1, 2, 4):
            keep = row >= d
            a_prev = jnp.where(keep, pltpu.roll(ag, d, 0), 1.0)
            u_prev = jnp.where(keep, pltpu.roll(ug, d, 0), 0.0)
            ug = ag * u_prev + ug
            ag = ag * a_prev
        hg = ug + ag * h
        hs.append(hg)
        h = jnp.broadcast_to(hg[SUBLANES - 1:SUBLANES, :], (SUBLANES, cw))
    gate = jax.nn.gelu(gr_bf.astype(F32), approximate=True)
    return (jnp.concatenate(hs, axis=0) * gate).astype(BF16), h


def _rglru_body(xr_ref, gr_ref, c0_ref, h0_ref, sh_ref, cw_ref, cb_ref, wg_ref, brg_ref, big_ref, lam_ref,
                ob_ref, hl_ref, prev_ref, hcar_ref):
    @pl.when(pl.program_id(2) == 0)
    def _():
        hcar_ref[...] = jnp.broadcast_to(h0_ref[0], hcar_ref.shape)
        prev_ref[...] = c0_ref[0]

    xc, r_pre, i_pre, prev8 = _rglru_head(xr_ref[0], prev_ref[...], sh_ref[...], cw_ref[...], cb_ref[...],
                                          wg_ref, brg_ref[...], big_ref[...])
    out, h = _rglru_tail(xc, r_pre, i_pre, gr_ref[0], hcar_ref[...], lam_ref[...])
    ob_ref[0] = out
    prev_ref[...] = prev8
    hcar_ref[...] = h
    hl_ref[0] = h[0:1, :]


def _rglru(z3, conv0, h0, conv_w, conv_b, w_gates, b_rg, b_ig, lam, *, xr_col0, gr_col0, tt, cw, name):
    b, t, _ = z3.shape
    d_rnn = conv_w.shape[1]
    n_cg = d_rnn // cw
    blocks_per_cg = cw // RNN_BLOCK
    xc0, gc0 = xr_col0 // cw, gr_col0 // cw
    vec = pl.BlockSpec((1, cw), lambda c, i, s: (0, c))
    return pl.pallas_call(
        _rglru_body,
        grid=(n_cg, b, t // tt),
        in_specs=[
            pl.BlockSpec((1, tt, cw), lambda c, i, s: (i, s, xc0 + c)),
            pl.BlockSpec((1, tt, cw), lambda c, i, s: (i, s, gc0 + c)),
            pl.BlockSpec((1, SUBLANES, cw), lambda c, i, s: (i, 0, c)),
            pl.BlockSpec((1, 1, cw), lambda c, i, s: (i, 0, c)),
            pl.BlockSpec(((CONV_WIDTH - 1) * tt, tt), lambda c, i, s: (0, 0)),
            pl.BlockSpec((CONV_WIDTH, cw), lambda c, i, s: (0, c)),
            vec,
            pl.BlockSpec((blocks_per_cg, RNN_BLOCK, 2 * RNN_BLOCK), lambda c, i, s: (c, 0, 0)),
            vec, vec, vec,
        ],
        out_specs=[
            pl.BlockSpec((1, tt, cw), lambda c, i, s: (i, s, c)),
            pl.BlockSpec((1, 1, cw), lambda c, i, s: (i, 0, c)),
        ],
        out_shape=[
            jax.ShapeDtypeStruct((b, t, d_rnn), BF16),
            jax.ShapeDtypeStruct((b, 1, d_rnn), F32),
        ],
        scratch_shapes=[
            pltpu.VMEM((SUBLANES, cw), F32),
            pltpu.VMEM((SUBLANES, cw), F32),
        ],
        compiler_params=_params(3, VMEM_RNN),
        name=name,
    )(z3, z3, conv0, h0, _shift_matrix(tt), conv_w, conv_b, w_gates, b_rg, b_ig, lam)


def _lookup(table, j):
    out = jnp.int32(table[-1])
    for k in range(len(table) - 2, -1, -1):
        out = jnp.where(j == k, jnp.int32(table[k]), out)
    return out


def _fused_body(x_ref, g_ref, w_ref, c0_ref, h0_ref, cw_ref, cb_ref, wg_ref, brg_ref, big_ref,
                lam_ref, z_ref, s_ref, ob_ref, hl_ref, xn_ref, stash_ref, prev_ref, hcar_ref, hd_ref, gr_ref, *,
                n_tiles, period, n_cg, pieces_per_cg, tt, n_split, stash_slots):
    i, j = pl.program_id(0), pl.program_id(1)
    cw = ob_ref.shape[1]

    @pl.when((j == 0) & (i < n_tiles))
    def _():
        xn_ref[...] = _rmsnorm_rows(x_ref[...], g_ref[...]).astype(BF16)

    @pl.when((j == 0) & (i == 0))
    def _():
        prev_ref[...] = jnp.zeros_like(prev_ref)
        hcar_ref[...] = jnp.zeros_like(hcar_ref)

    tm = xn_ref.shape[0]
    rows_s = tt // n_split
    state_start = tm - s_ref.shape[0]
    first_rows = tm // 4
    rest_rows = (tm - first_rows) // n_split
    bounds = [0] + [first_rows + c * rest_rows for c in range(n_split + 1)]

    def project(c):
        lo, hi = bounds[c], bounds[c + 1]
        acc = jnp.dot(xn_ref[lo:hi, :], w_ref[...], preferred_element_type=F32)
        zb = acc.astype(BF16)
        z_ref[lo:hi, :] = zb
        if hi > state_start:
            keep = max(lo, state_start)
            s_ref[keep - state_start:hi - state_start, :] = acc[keep - lo:, :]
        stash_ref[_lookup(stash_slots, j), lo:hi, :] = zb
        return acc[hi - lo - SUBLANES:, :]

    def after(state, acc_rows):
        cg, lam, h = state
        zero = (pltpu.bitcast(acc_rows, jnp.uint32) >> 16) >> 16
        return cg, lam, pltpu.bitcast(pltpu.bitcast(h, jnp.uint32) + zero, F32)

    def recur_load():
        cg, tp = j // pieces_per_cg, j % pieces_per_cg
        r0 = pl.multiple_of(tp * tt, tt)
        first = ((i - 1) % period == 0) & (tp == 0)
        prev8 = jnp.where(first, c0_ref[0, cg], prev_ref[cg])
        h = jnp.where(first, jnp.broadcast_to(h0_ref[0, cg], (SUBLANES, cw)), hcar_ref[cg])
        gr_ref[...] = stash_ref[n_cg + cg, pl.ds(r0, tt), :]
        return cg, stash_ref[cg, pl.ds(r0, tt), :], prev8, h

    def recur_head(loaded):
        cg, x_bf, prev8, h = loaded
        xc, r_pre, i_pre, prev8 = _rglru_head(x_bf, prev8, None, cw_ref[cg], cb_ref[cg], wg_ref.at[cg],
                                              brg_ref[cg], big_ref[cg])
        hd_ref[0] = xc
        hd_ref[1] = r_pre
        hd_ref[2] = i_pre
        prev_ref[cg] = prev8
        return cg, lam_ref[cg], h

    def recur_tail(k, state):
        cg, lam, h = state
        lo = k * rows_s
        out, h = _rglru_tail(hd_ref[0, lo:lo + rows_s, :], hd_ref[1, lo:lo + rows_s, :],
                             hd_ref[2, lo:lo + rows_s, :], gr_ref[lo:lo + rows_s, :], h, lam)
        ob_ref[lo:lo + rows_s, :] = out
        return cg, lam, h

    def recur_end(state):
        cg, _, h = state
        hcar_ref[cg] = h
        hl_ref[0, 0] = h[0:1, :]

    has_proj = i < n_tiles
    has_rec = (i >= 1) & (j < n_cg * pieces_per_cg)

    @pl.when(has_proj & has_rec)
    def _():
        state = after(recur_head(recur_load()), project(0))
        for k in range(n_split):
            state = recur_tail(k, state)
            acc_rows = project(k + 1)
            if k + 1 < n_split:
                state = after(state, acc_rows)
        recur_end(state)

    @pl.when(has_proj & jnp.logical_not(has_rec))
    def _():
        for c in range(n_split + 1):
            project(c)

    @pl.when(jnp.logical_not(has_proj) & has_rec)
    def _():
        state = recur_head(recur_load())
        for k in range(n_split):
            state = recur_tail(k, state)
        recur_end(state)


def _in_proj_rglru(x2d, g, w, conv0, h0, conv_w, conv_b, w_gates, b_rg, b_ig, lam, *, tm, tn, tt, period,
                   state_rows, state_col0, state_cols, xr_col0, gr_col0, name):
    m, d = x2d.shape
    n = w.shape[1]
    d_rnn = conv_w.shape[1]
    n_tiles, n_col, n_cg = m // tm, n // tn, d_rnn // tn
    n_seq = n_tiles // period
    pieces_per_cg = tm // tt
    n_pieces = n_cg * pieces_per_cg
    xr0, gr0 = xr_col0 // tn, gr_col0 // tn
    assert n_pieces + n_cg <= n_col and tm % tt == 0 and n_tiles % period == 0

    order, stash_slots = [None] * n_col, [2 * n_cg] * n_col
    for c in range(n_cg):
        jx = (c + 1) * pieces_per_cg
        order[jx], stash_slots[jx] = xr0 + c, c
        order[jx + 1], stash_slots[jx + 1] = gr0 + c, n_cg + c
    rest = [c for c in range(n_col) if c not in order]
    for jj in range(n_col):
        if order[jj] is None:
            order[jj] = rest.pop(0)
    s0, n_state = state_col0 // tn, state_cols // tn
    state_pos = [order.index(s0 + k) for k in range(n_state)]
    assert state_pos == sorted(state_pos)
    state_blk = [sum(p < jj for p in state_pos) for jj in range(n_col)]

    last = lambda i: i == n_tiles
    row = lambda i: jnp.minimum(i, n_tiles - 1)
    col = lambda i, j: jnp.where(last(i), order[-1], _lookup(order, j))
    seq_prev = lambda i: jnp.maximum(i - 1, 0) // period
    piece = lambda i, j: jnp.where(i == 0, 0, jnp.minimum(j, n_pieces - 1))

    def state_block(i, j):
        live = i % period == period - 1
        blk = jnp.where(live, _lookup(state_blk, j), n_state + 1)
        return row(i) // period, jnp.where(last(i), n_state, blk)

    per_cg = lambda a: a.reshape(a.shape[0], n_cg, tn).swapaxes(0, 1)
    whole = lambda a: pl.BlockSpec(a.shape, lambda i, j: (0,) * a.ndim)
    n_split = 4
    assert tt % (n_split * SUBLANES) == 0 and (tm - tm // 4) % (2 * SUBLANES * n_split) == 0
    params = [per_cg(conv_w), per_cg(conv_b), w_gates.reshape(n_cg, -1, RNN_BLOCK, 2 * RNN_BLOCK),
              per_cg(b_rg), per_cg(b_ig), per_cg(lam)]
    conv0 = conv0.reshape(n_seq, SUBLANES, n_cg, tn).swapaxes(1, 2)
    h0 = h0.reshape(n_seq, 1, n_cg, tn).swapaxes(1, 2)

    z, state, o_b, h_last = pl.pallas_call(
        functools.partial(_fused_body, n_tiles=n_tiles, period=period, n_cg=n_cg,
                          pieces_per_cg=pieces_per_cg, tt=tt, n_split=n_split,
                          stash_slots=tuple(stash_slots)),
        grid=(n_tiles + 1, n_col),
        in_specs=[
            pl.BlockSpec((tm, d), lambda i, j: (row(i), 0)),
            pl.BlockSpec((1, d), lambda i, j: (0, 0)),
            pl.BlockSpec((d, tn), lambda i, j: (0, col(i, j))),
            pl.BlockSpec((1, n_cg, SUBLANES, tn), lambda i, j: (seq_prev(i), 0, 0, 0)),
            pl.BlockSpec((1, n_cg, 1, tn), lambda i, j: (seq_prev(i), 0, 0, 0)),
        ] + [whole(p) for p in params],
        out_specs=[
            pl.BlockSpec((tm, tn), lambda i, j: (row(i), col(i, j))),
            pl.BlockSpec((state_rows, tn), state_block),
            pl.BlockSpec((tt, tn), lambda i, j: (jnp.maximum(i - 1, 0) * pieces_per_cg
                                                 + piece(i, j) % pieces_per_cg, piece(i, j) // pieces_per_cg)),
            pl.BlockSpec((1, 1, 1, tn), lambda i, j: (jnp.maximum(i - 1, 0), piece(i, j) // pieces_per_cg, 0, 0)),
        ],
        out_shape=[
            jax.ShapeDtypeStruct((m, n), BF16),
            jax.ShapeDtypeStruct((n_seq * state_rows, (n_state + 2) * tn), F32),
            jax.ShapeDtypeStruct((m, d_rnn), BF16),
            jax.ShapeDtypeStruct((n_tiles, n_cg, 1, tn), F32),
        ],
        scratch_shapes=[
            pltpu.VMEM((tm, d), BF16),
            pltpu.VMEM((2 * n_cg + 1, tm, tn), BF16),
            pltpu.VMEM((n_cg, SUBLANES, tn), F32),
            pltpu.VMEM((n_cg, SUBLANES, tn), F32),
            pltpu.VMEM((3, tt, tn), F32),
            pltpu.VMEM((tt, tn), BF16),
        ],
        compiler_params=_params(2, VMEM_FUSED),
        name=name,
    )(x2d, g, w, conv0, h0, *params)
    return z, state, o_b, h_last.reshape(n_tiles, d_rnn)[period - 1::period]


def _merge_body(oa_ref, ob_ref, ga_ref, gb_ref, wa_ref, wb_ref, o_ref):
    tm = o_ref.shape[0]
    for c in range(MERGE_CHUNKS):
        lo, hi = c * tm // MERGE_CHUNKS, (c + 1) * tm // MERGE_CHUNKS
        ya = jnp.dot(oa_ref[lo:hi, :], wa_ref[...], preferred_element_type=F32)
        yb = jnp.dot(ob_ref[lo:hi, :], wb_ref[...], preferred_element_type=F32)
        ga = jax.nn.sigmoid(ga_ref[lo:hi, :].astype(F32))
        gb = jax.nn.sigmoid(gb_ref[lo:hi, :].astype(F32))
        o_ref[lo:hi, :] = (ga * ya + gb * yb).astype(o_ref.dtype)


def _gate_merge(o_a, o_b, z2, w_a, w_b, *, ga_col0, gb_col0, tm, tn, name):
    m = o_a.shape[0]
    d = w_a.shape[1]
    ga0, gb0 = ga_col0 // tn, gb_col0 // tn
    return pl.pallas_call(
        _merge_body,
        grid=(m // tm, d // tn),
        in_specs=[
            pl.BlockSpec((tm, o_a.shape[1]), lambda i, j: (i, 0)),
            pl.BlockSpec((tm, o_b.shape[1]), lambda i, j: (i, 0)),
            pl.BlockSpec((tm, tn), lambda i, j: (i, ga0 + j)),
            pl.BlockSpec((tm, tn), lambda i, j: (i, gb0 + j)),
            pl.BlockSpec((w_a.shape[0], tn), lambda i, j: (0, j)),
            pl.BlockSpec((w_b.shape[0], tn), lambda i, j: (0, j)),
        ],
        out_specs=pl.BlockSpec((tm, tn), lambda i, j: (i, j)),
        out_shape=jax.ShapeDtypeStruct((m, d), BF16),
        compiler_params=_params(2, VMEM_MLP),
        name=name,
    )(o_a, o_b, z2, z2, w_a, w_b)


def _out_norm_body(m_ref, w_ref, x_ref, g_ref, o_ref):
    y = jnp.dot(m_ref[...], w_ref[...], preferred_element_type=F32)
    o_ref[...] = x_ref[...] + _rmsnorm_rows(y, g_ref[...])


def _out_norm(merged, w_out, x2d, g, *, tm, name):
    m, d = x2d.shape
    return pl.pallas_call(
        _out_norm_body,
        grid=(m // tm,),
        in_specs=[
            pl.BlockSpec((tm, d), lambda i: (i, 0)),
            pl.BlockSpec((d, d), lambda i: (0, 0)),
            pl.BlockSpec((tm, d), lambda i: (i, 0)),
            pl.BlockSpec((1, d), lambda i: (0, 0)),
        ],
        out_specs=pl.BlockSpec((tm, d), lambda i: (i, 0)),
        out_shape=jax.ShapeDtypeStruct((m, d), F32),
        compiler_params=_params(1, VMEM_MLP),
        name=name,
    )(merged, w_out, x2d, g)


def _ffn_body(x_ref, g1_ref, w1_ref, w2_ref, g2_ref, o_ref, xn_ref, *, n_steps, edge_chunks):
    f = pl.program_id(1)

    def rows_chunk(lo, hi, first, last):
        if first:
            xn = _rmsnorm_rows(x_ref[lo:hi, :], g1_ref[...]).astype(BF16)
            xn_ref[lo:hi, :] = xn
        else:
            xn = xn_ref[lo:hi, :]
        hid = jnp.dot(xn, w1_ref[...], preferred_element_type=F32)
        hid = jnp.square(jnp.maximum(hid, 0.0)).astype(BF16)
        acc = jnp.dot(hid, w2_ref[...], preferred_element_type=F32)
        if not first:
            acc = o_ref[lo:hi, :] + acc
        if last:
            acc = x_ref[lo:hi, :] + _rmsnorm_rows(acc, g2_ref[...])
        o_ref[lo:hi, :] = acc

    tm = o_ref.shape[0]
    for first, last in sorted({(k == 0, k == n_steps - 1) for k in range(n_steps)}):
        chunks = edge_chunks if (first or last) else 1

        @pl.when(((f == 0) == first) & ((f == n_steps - 1) == last))
        def _(first=first, last=last, chunks=chunks):
            for c in range(chunks):
                rows_chunk(c * tm // chunks, (c + 1) * tm // chunks, first, last)


def _ffn(x2d, g1, w1, w2, g2, *, tm, tf, name):
    m, d = x2d.shape
    d_ff = w1.shape[1]
    return pl.pallas_call(
        functools.partial(_ffn_body, n_steps=d_ff // tf, edge_chunks=FFN_EDGE_CHUNKS),
        grid=(m // tm, d_ff // tf),
        in_specs=[
            pl.BlockSpec((tm, d), lambda i, f: (i, 0)),
            pl.BlockSpec((1, d), lambda i, f: (0, 0)),
            pl.BlockSpec((d, tf), lambda i, f: (0, f)),
            pl.BlockSpec((tf, d), lambda i, f: (f, 0)),
            pl.BlockSpec((1, d), lambda i, f: (0, 0)),
        ],
        out_specs=pl.BlockSpec((tm, d), lambda i, f: (i, 0)),
        out_shape=jax.ShapeDtypeStruct((m, d), F32),
        scratch_shapes=[pltpu.VMEM((tm, d), BF16)],
        compiler_params=_params(2, VMEM_MLP),
        name=name,
    )(x2d, g1, w1, w2, g2)


def _pick_tile(n, target):
    t = min(n, target)
    while n % t:
        t //= 2
    return t


def _layer(x, cache_k, cache_v, conv_state, h_state, wts, tag):
    b, t, d = x.shape
    m = b * t
    x2d = x.reshape(m, d)
    d_rnn = wts["conv_w"].shape[1]
    n_in = wts["w_in"].shape[1]
    col_k, col_v, col_xr = D_ATTN, 2 * D_ATTN, 3 * D_ATTN
    col_gr = col_xr + d_rnn
    col_ga = col_gr + d_rnn
    col_gb = col_ga + d

    tm = _pick_tile(m, ROW_TILE)
    tn = COL_TILE
    keep = min(KV_REACH, t)
    if keep == t:
        period, state_rows = 1, tm
    else:
        assert t % tm == 0 and keep <= tm
        period, state_rows = t // tm, keep
    rnn_w = (wts["conv_w"], wts["conv_b"], wts["w_gates"], wts["b_rg"], wts["b_ig"], wts["lru_lambda"])
    if cache_k is None:
        conv0 = jnp.zeros((b, SUBLANES, d_rnn), F32)
        h0 = jnp.zeros((b, 1, d_rnn), F32)
        z2, state, o_b, h_last = _in_proj_rglru(
            x2d, wts["pre_mix_g"], wts["w_in"], conv0, h0, *rnn_w, tm=tm, tn=tn, tt=_pick_tile(tm, RNN_FRAMES),
            period=period, state_rows=state_rows, state_col0=col_k, state_cols=col_gr - col_k,
            xr_col0=col_xr, gr_col0=col_gr, name=f"in_proj_rglru_{tag}")
        z3 = z2.reshape(b, t, n_in)
        offsets = [i * CHUNK for i in range(PAST_CHUNKS + 2)]
        o_a = _attention(z3, z3, z3, wts["rel_bias"], q_col0=0, k_col0=col_k, v_col0=col_v, bb=1,
                         heads=HEADS_PER_STEP, offsets=offsets, name=f"attn_{tag}")
    else:
        z2, state = _in_proj(x2d, wts["pre_mix_g"], wts["w_in"], tm=tm, tn=tn, period=period,
                             state_rows=state_rows, state_col0=col_k, state_cols=col_gr - col_k,
                             name=f"in_proj_{tag}")
        z3 = z2.reshape(b, t, n_in)
        n_cached = cache_k.shape[1]
        assert t == CHUNK and n_cached == KV_REACH
        o_a = _attention(z3, z3, z3, wts["rel_bias"], q_col0=0, k_col0=col_k, v_col0=col_v,
                         bb=_pick_tile(b, SAMPLE_SEQS), heads=CACHED_HEADS_PER_STEP, offsets=[n_cached],
                         name=f"attn_{tag}",
                         k_cache=cache_k.astype(BF16).reshape(b, n_cached, D_ATTN),
                         v_cache=cache_v.astype(BF16).reshape(b, n_cached, D_ATTN))
        conv0 = jnp.pad(conv_state.astype(F32), ((0, 0), (SUBLANES - (CONV_WIDTH - 1), 0), (0, 0)))
        h0 = h_state.astype(F32).reshape(b, 1, d_rnn)
        o_b, h_last = _rglru(z3, conv0, h0, *rnn_w, xr_col0=col_xr, gr_col0=col_gr,
                             tt=_pick_tile(t, RNN_FRAMES), cw=COL_TILE, name=f"rglru_{tag}")
    state = state.reshape(b, keep, -1)
    new_k = state[:, :, :D_ATTN].reshape(b, keep, N_HEADS, HEAD_DIM)
    new_v = state[:, :, D_ATTN:2 * D_ATTN].reshape(b, keep, N_HEADS, HEAD_DIM)
    conv_tail = state[:, keep - (CONV_WIDTH - 1):, 2 * D_ATTN:col_gr - col_k]

    merged = _gate_merge(o_a.reshape(m, D_ATTN), o_b.reshape(m, d_rnn), z2, wts["w_attn_up"],
                         wts["w_rnn_up"], ga_col0=col_ga, gb_col0=col_gb, tm=tm, tn=tn,
                         name=f"gate_merge_{tag}")
    tm2 = _pick_tile(m, WIDE_ROW_TILE)
    x1 = _out_norm(merged, wts["w_out"], x2d, wts["post_mix_g"], tm=tm2, name=f"out_norm_{tag}")
    y = _ffn(x1, wts["pre_ffn_g"], wts["w_ff1"], wts["w_ff2"], wts["post_ffn_g"], tm=tm2, tf=FFN_TILE,
             name=f"ffn_{tag}")
    return y.reshape(b, t, d), new_k, new_v, conv_tail, h_last.reshape(b, d_rnn)


def kernel(x_prompt, x_sample, cache_k, cache_v, state_conv, state_h, pre_mix_g, w_in, rel_bias, conv_w, conv_b, w_rg, b_rg, w_ig, b_ig, lru_lambda, w_attn_up, w_rnn_up, w_out, post_mix_g, pre_ffn_g, w_ff1, w_ff2, post_ffn_g):
    depth = w_in.shape[0]
    y_p, y_s = x_prompt, x_sample
    outs_p, outs_s = [], []
    for l in range(depth):
        wts = {
            "pre_mix_g": pre_mix_g[l][None], "post_mix_g": post_mix_g[l][None],
            "pre_ffn_g": pre_ffn_g[l][None], "post_ffn_g": post_ffn_g[l][None],
            "w_in": w_in[l].astype(BF16), "rel_bias": rel_bias[l],
            "conv_w": conv_w[l], "conv_b": conv_b[l][None],
            "w_gates": jnp.concatenate([w_rg[l], w_ig[l]], axis=-1).astype(BF16),
            "b_rg": b_rg[l][None], "b_ig": b_ig[l][None], "lru_lambda": lru_lambda[l][None],
            "w_attn_up": w_attn_up[l].astype(BF16), "w_rnn_up": w_rnn_up[l].astype(BF16),
            "w_out": w_out[l].astype(BF16), "w_ff1": w_ff1[l].astype(BF16), "w_ff2": w_ff2[l].astype(BF16),
        }
        y_p, *st_p = _layer(y_p, None, None, None, None, wts, f"p{l}")
        y_s, *st_s = _layer(y_s, cache_k[l], cache_v[l], state_conv[l], state_h[l], wts, f"s{l}")
        outs_p.append(st_p)
        outs_s.append(st_s)
    stack = lambda outs, i: jnp.stack([o[i] for o in outs])
    return (y_p, y_s,
            stack(outs_p, 0), stack(outs_p, 1), stack(outs_p, 2), stack(outs_p, 3),
            stack(outs_s, 0), stack(outs_s, 1), stack(outs_s, 2), stack(outs_s, 3))
```

```python
import functools

import jax
import jax.numpy as jnp
from jax import lax
from jax.experimental import pallas as pl
from jax.experimental.pallas import tpu as pltpu

F32 = jnp.float32
BF16 = jnp.bfloat16

CHUNK = 64
PAST_CHUNKS = 8
KV_REACH = PAST_CHUNKS * CHUNK
N_HEADS = 16
HEAD_DIM = 64
D_ATTN = N_HEADS * HEAD_DIM
MAX_REL = 256
RNN_BLOCK = 128
CONV_WIDTH = 4
LRU_C = 8.0
EPS = 1e-6
NEG_INF = -1e30

LANES = 128
SUBLANES = 8
MIB = 1024 * 1024

ROW_TILE = 1024
COL_TILE = 1024
WIDE_ROW_TILE = 512
FFN_TILE = 2048
FFN_WIDE_MIN_ROW_TILES = 16
RNN_FRAMES = 256
HEADS_PER_STEP = 4
CACHED_HEADS_PER_STEP = 2
SAMPLE_SEQS = 8
KT_BLOCK = 512
MERGE_CHUNKS = 4
FFN_EDGE_CHUNKS = 2
VMEM_PROJ, VMEM_FUSED, VMEM_ATTN, VMEM_RNN, VMEM_MLP, VMEM_FFN = 52, 58, 40, 40, 48, 60


def _params(n_axes, vmem_mib):
    return pltpu.CompilerParams(
        dimension_semantics=("arbitrary",) * n_axes,
        vmem_limit_bytes=vmem_mib * MIB,
    )


def _rmsnorm_rows(x, g):
    ms = jnp.mean(x * x, axis=-1, keepdims=True)
    return x * lax.rsqrt(ms + EPS) * g


def _in_proj_body(x_ref, g_ref, w_ref, o_ref, s_ref, xn_ref, *, period, j0, j1):
    i, j = pl.program_id(0), pl.program_id(1)

    @pl.when(j == 0)
    def _():
        xn_ref[...] = _rmsnorm_rows(x_ref[...], g_ref[...]).astype(BF16)

    keep_f32 = (i % period == period - 1) & (j >= j0) & (j < j1)

    @pl.when(keep_f32)
    def _():
        acc = jnp.dot(xn_ref[...], w_ref[...], preferred_element_type=F32)
        o_ref[...] = acc.astype(o_ref.dtype)
        s_ref[...] = acc[acc.shape[0] - s_ref.shape[0]:, :]

    @pl.when(jnp.logical_not(keep_f32))
    def _():
        o_ref[...] = jnp.dot(xn_ref[...], w_ref[...], preferred_element_type=F32).astype(o_ref.dtype)


def _in_proj(x2d, g, w, *, tm, tn, period, state_rows, state_col0, state_cols, name):
    m, d = x2d.shape
    n = w.shape[1]
    j0, j1 = state_col0 // tn, (state_col0 + state_cols) // tn
    n_state_tiles = m // tm // period

    def state_block(i, j):
        live = i % period == period - 1
        return i // period, jnp.where(live, jnp.clip(j - j0, 0, j1 - j0 - 1), 0)

    return pl.pallas_call(
        functools.partial(_in_proj_body, period=period, j0=j0, j1=j1),
        grid=(m // tm, n // tn),
        in_specs=[
            pl.BlockSpec((tm, d), lambda i, j: (i, 0)),
            pl.BlockSpec((1, d), lambda i, j: (0, 0)),
            pl.BlockSpec((d, tn), lambda i, j: (0, j)),
        ],
        out_specs=[
            pl.BlockSpec((tm, tn), lambda i, j: (i, j)),
            pl.BlockSpec((state_rows, tn), state_block),
        ],
        out_shape=[
            jax.ShapeDtypeStruct((m, n), BF16),
            jax.ShapeDtypeStruct((n_state_tiles * state_rows, state_cols), F32),
        ],
        scratch_shapes=[pltpu.VMEM((tm, d), BF16)],
        compiler_params=_params(2, VMEM_PROJ),
        name=name,
    )(x2d, g, w)


def _bias_diagonals(rel_bias, width, offsets, heads):
    x = jnp.arange(width)
    rows = [rel_bias[:, jnp.clip(off + CHUNK - 1 - x, -MAX_REL, MAX_REL) + MAX_REL] for off in offsets]
    g = jnp.stack(rows, axis=1).astype(F32)
    n_groups = rel_bias.shape[0] // heads
    g = g.reshape(n_groups, heads, len(offsets), width)
    return jnp.swapaxes(g, 1, 2).reshape(n_groups, len(offsets) * heads, width)


def _attn_body(*refs, bb, n_chunks, kw, heads, offsets, unit, cached):
    if cached:
        q_ref, kc_ref, vc_ref, k_ref, v_ref, g_ref, o_ref, tab_ref, s_ref, p_ref, l_ref = refs
        band = lambda c_ref, n_ref, bi, ks: jnp.concatenate([c_ref[bi], n_ref[bi]], axis=0)
    else:
        q_ref, k_ref, v_ref, g_ref, o_ref, tab_ref, s_ref, p_ref, l_ref, kt_ref = refs
        kc_ref = vc_ref = None
        band = lambda c_ref, n_ref, bi, ks: n_ref[bi, pl.ds(ks, kw), :]
    width = g_ref.shape[2]
    hw = heads * HEAD_DIM

    @pl.when(pl.program_id(1) == 0)
    def _():
        r = lax.broadcasted_iota(jnp.int32, (CHUNK, kw), 0)
        j = lax.broadcasted_iota(jnp.int32, (CHUNK, kw), 1)
        for c, off in enumerate(offsets):
            cq = (r + off) // CHUNK
            ck = j // CHUNK
            valid = (ck <= cq) & (ck >= cq - PAST_CHUNKS)
            for h in range(heads):
                g = jnp.broadcast_to(g_ref[0, c * heads + h:c * heads + h + 1, :], (CHUNK, width))
                t = pltpu.roll(g, width - (CHUNK - 1), 1, stride=1, stride_axis=0)
                tab_ref[c, h * CHUNK:(h + 1) * CHUNK, :] = jnp.where(valid, t[:, :kw], NEG_INF)

    if not cached:
        for r0 in range(0, k_ref.shape[1], KT_BLOCK):
            kt_ref[:, r0:r0 + KT_BLOCK] = k_ref[0, r0:r0 + KT_BLOCK, :].T

    lane = lax.broadcasted_iota(jnp.int32, (1, hw), 1)
    head_lanes = [(lane >= h * HEAD_DIM) & (lane < (h + 1) * HEAD_DIM) for h in range(heads)]
    rows = heads * CHUNK
    n_units = bb * n_chunks // unit

    def locate(u, j):
        idx = u * unit + j
        bi, c = (0, idx) if bb == 1 else (idx // n_chunks, idx % n_chunks)
        q0 = pl.multiple_of(c * CHUNK, CHUNK)
        if cached:
            return bi, 0, q0, 0
        past = jnp.maximum(c - PAST_CHUNKS, 0)
        odd = past % 2
        ks = pl.multiple_of((past - odd) * CHUNK, 2 * CHUNK)
        return bi, jnp.minimum(c, PAST_CHUNKS) + odd, q0, ks

    def scores(u, slot):
        for j in range(unit):
            bi, c, q0, ks = locate(u, j)
            q = q_ref[bi, pl.ds(q0, CHUNK), :]
            qs = jnp.concatenate([jnp.where(m, q, jnp.zeros_like(q)) for m in head_lanes], axis=0)
            qs = qs * jnp.asarray(HEAD_DIM ** -0.5, q.dtype)
            if cached:
                kb = band(kc_ref, k_ref, bi, ks)
                s = lax.dot_general(qs, kb, (((1,), (1,)), ((), ())), preferred_element_type=F32)
            else:
                s = jnp.dot(qs, kt_ref[:, pl.ds(ks, kw)], preferred_element_type=F32)
            s_ref[slot, j * rows:(j + 1) * rows, :] = s + tab_ref[c]

    def softmax(slot):
        s = s_ref[slot]
        p = jnp.exp(s - jnp.max(s, axis=-1, keepdims=True))
        p_ref[slot] = p.astype(p_ref.dtype)
        l_ref[slot] = 1.0 / jnp.sum(p, axis=-1, keepdims=True)

    def output(u, slot):
        for j in range(unit):
            bi, c, q0, ks = locate(u, j)
            vb = band(vc_ref, v_ref, bi, ks)
            o_all = jnp.dot(p_ref[slot, j * rows:(j + 1) * rows, :], vb, preferred_element_type=F32)
            o_all = o_all * l_ref[slot, j * rows:(j + 1) * rows, :]
            o = o_all[0:CHUNK]
            for h in range(1, heads):
                o = jnp.where(head_lanes[h], o_all[h * CHUNK:(h + 1) * CHUNK], o)
            o_ref[bi, pl.ds(q0, CHUNK), :] = o.astype(o_ref.dtype)

    def step(i, slot, do_scores, do_softmax, do_output):
        if do_scores:
            scores(i, slot)
        if do_output:
            output(i - 2, slot)
        if do_softmax:
            softmax(1 - slot)

    for i in range(2):
        step(i, i % 2, i < n_units, 1 <= i <= n_units, False)
    n_pairs = max(n_units - 2, 0) // 2
    if n_pairs:
        def body(t, carry):
            i = 2 + 2 * t
            step(i, 0, True, True, True)
            step(i + 1, 1, True, True, True)
            return carry
        lax.fori_loop(0, n_pairs, body, 0)
    for i in range(2 + 2 * n_pairs, n_units + 2):
        step(i, i % 2, i < n_units, 1 <= i <= n_units, i >= 2)


def _attention(q_arr, k_arr, v_arr, rel_bias, *, q_col0, k_col0, v_col0, bb, heads, offsets, name,
               k_cache=None, v_cache=None):
    b, sq, _ = q_arr.shape
    sk = k_arr.shape[1]
    cached = k_cache is not None
    if cached:
        kw = KV_REACH + CHUNK
        assert sq == CHUNK and sk == CHUNK and k_cache.shape[1] == KV_REACH and len(offsets) == 1
    else:
        kw = KV_REACH + 2 * CHUNK
        assert bb == 1 and sq == sk and sk % KT_BLOCK == 0 and sk >= kw and len(offsets) == PAST_CHUNKS + 2
    hw = heads * HEAD_DIM
    n_groups = N_HEADS // heads
    n_case = len(offsets)
    width = -(-(kw + CHUNK - 1) // LANES) * LANES
    diag = _bias_diagonals(rel_bias, width, offsets, heads)
    qc, kc, vc = q_col0 // hw, k_col0 // hw, v_col0 // hw
    unit = 2
    assert (bb * sq // CHUNK) % unit == 0
    rows = unit * heads * CHUNK
    cache_specs = [pl.BlockSpec((bb, KV_REACH, hw), lambda g, i: (i, 0, g))] * 2 if cached else []
    cache_args = [k_cache, v_cache] if cached else []
    return pl.pallas_call(
        functools.partial(_attn_body, bb=bb, n_chunks=sq // CHUNK, kw=kw, heads=heads,
                          offsets=tuple(offsets), unit=unit, cached=cached),
        grid=(n_groups, b // bb),
        in_specs=[pl.BlockSpec((bb, sq, hw), lambda g, i: (i, 0, qc + g))] + cache_specs + [
            pl.BlockSpec((bb, sk, hw), lambda g, i: (i, 0, kc + g)),
            pl.BlockSpec((bb, sk, hw), lambda g, i: (i, 0, vc + g)),
            pl.BlockSpec((1, n_case * heads, width), lambda g, i: (g, 0, 0)),
        ],
        out_specs=pl.BlockSpec((bb, sq, hw), lambda g, i: (i, 0, g)),
        out_shape=jax.ShapeDtypeStruct((b, sq, D_ATTN), BF16),
        scratch_shapes=[
            pltpu.VMEM((n_case, heads * CHUNK, kw), F32),
            pltpu.VMEM((2, rows, kw), F32),
            pltpu.VMEM((2, rows, kw), BF16),
            pltpu.VMEM((2, rows, 1), F32),
        ] + ([] if cached else [pltpu.VMEM((hw, sk), BF16)]),
        compiler_params=_params(2, VMEM_ATTN),
        name=name,
    )(q_arr, *cache_args, k_arr, v_arr, diag)


def _log_sigmoid(x):
    return jnp.minimum(x, 0.0) - jnp.log1p(jnp.exp(-jnp.abs(x)))


def _shift_matrix(tt):
    t = jnp.arange(tt)
    blocks = [(t[:, None] - d) == t[None, :] for d in range(CONV_WIDTH - 1, 0, -1)]
    return jnp.concatenate(blocks, axis=0).astype(BF16)


def _rglru_head(x_bf, prev8, shift, conv_w, conv_b, wg_ref, b_rg, b_ig):
    tt, cw = x_bf.shape
    taps = CONV_WIDTH - 1
    x = x_bf.astype(F32)
    row = lax.broadcasted_iota(jnp.int32, (SUBLANES, cw), 0)
    if shift is not None:
        sh = jnp.dot(shift, x_bf, preferred_element_type=F32)
        xc = conv_b + conv_w[0:1] * sh[0:tt]
        for k in range(1, taps):
            xc = xc + conv_w[k:k + 1] * sh[k * tt:(k + 1) * tt]
        xc = xc + conv_w[taps:taps + 1] * x
        head = xc[0:SUBLANES]
        for k in range(taps):
            d = taps - k
            head = head + conv_w[k:k + 1] * jnp.where(row < d, pltpu.roll(prev8, d, 0), 0.0)
        xc = jnp.concatenate([head, xc[SUBLANES:]], axis=0)
    else:
        xc = conv_b
        for k in range(taps):
            d = taps - k
            sh = pltpu.roll(x, d, 0)
            head = jnp.where(row < d, pltpu.roll(prev8, d, 0), sh[0:SUBLANES])
            xc = xc + conv_w[k:k + 1] * jnp.concatenate([head, sh[SUBLANES:]], axis=0)
        xc = xc + conv_w[taps:taps + 1] * x

    xcb = xc.astype(BF16)
    r_parts, i_parts = [], []
    for n in range(cw // RNN_BLOCK):
        gts = jnp.dot(xcb[:, n * RNN_BLOCK:(n + 1) * RNN_BLOCK], wg_ref[n], preferred_element_type=F32)
        r_parts.append(gts[:, :RNN_BLOCK])
        i_parts.append(gts[:, RNN_BLOCK:])
    r_pre = jnp.concatenate(r_parts, axis=1) + b_rg
    i_pre = jnp.concatenate(i_parts, axis=1) + b_ig
    return xc, r_pre, i_pre, x[tt - SUBLANES:tt]


def _rglru_tail(xc, r_pre, i_pre, gr_bf, h_in, lam):
    rows, cw = xc.shape
    r = jax.nn.sigmoid(r_pre)
    i = jax.nn.sigmoid(i_pre)
    log_a = r * (LRU_C * _log_sigmoid(lam))
    a = jnp.exp(log_a)
    w = -jnp.tanh(log_a)
    one_minus_a2 = 2.0 * w / (1.0 + w)
    root = jnp.where(one_minus_a2 > 0.0, one_minus_a2 * lax.rsqrt(one_minus_a2), 0.0)
    u = root * (i * xc)

    row = lax.broadcasted_iota(jnp.int32, (SUBLANES, cw), 0)
    h = h_in
    hs = []
    for g in range(rows // SUBLANES):
        ag = a[g * SUBLANES:(g + 1) * SUBLANES]
        ug = u[g * SUBLANES:(g + 1) * SUBLANES]
        for d in (1, 2, 4):
            keep = row >= d
            a_prev = jnp.where(keep, pltpu.roll(ag, d, 0), 1.0)
            u_prev = jnp.where(keep, pltpu.roll(ug, d, 0), 0.0)
            ug = ag * u_prev + ug
            ag = ag * a_prev
        hg = ug + ag * h
        hs.append(hg)
        h = jnp.broadcast_to(hg[SUBLANES - 1:SUBLANES, :], (SUBLANES, cw))
    gate = jax.nn.gelu(gr_bf.astype(F32), approximate=True)
    return (jnp.concatenate(hs, axis=0) * gate).astype(BF16), h


def _rglru_body(xr_ref, gr_ref, c0_ref, h0_ref, sh_ref, cw_ref, cb_ref, wg_ref, brg_ref, big_ref, lam_ref,
                ob_ref, hl_ref, prev_ref, hcar_ref):
    @pl.when(pl.program_id(2) == 0)
    def _():
        hcar_ref[...] = jnp.broadcast_to(h0_ref[0], hcar_ref.shape)
        prev_ref[...] = c0_ref[0]

    xc, r_pre, i_pre, prev8 = _rglru_head(xr_ref[0], prev_ref[...], sh_ref[...], cw_ref[...], cb_ref[...],
                                          wg_ref, brg_ref[...], big_ref[...])
    out, h = _rglru_tail(xc, r_pre, i_pre, gr_ref[0], hcar_ref[...], lam_ref[...])
    ob_ref[0] = out
    prev_ref[...] = prev8
    hcar_ref[...] = h
    hl_ref[0] = h[0:1, :]


def _rglru(z3, conv0, h0, conv_w, conv_b, w_gates, b_rg, b_ig, lam, *, xr_col0, gr_col0, tt, cw, name):
    b, t, _ = z3.shape
    d_rnn = conv_w.shape[1]
    n_cg = d_rnn // cw
    blocks_per_cg = cw // RNN_BLOCK
    xc0, gc0 = xr_col0 // cw, gr_col0 // cw
    vec = pl.BlockSpec((1, cw), lambda c, i, s: (0, c))
    return pl.pallas_call(
        _rglru_body,
        grid=(n_cg, b, t // tt),
        in_specs=[
            pl.BlockSpec((1, tt, cw), lambda c, i, s: (i, s, xc0 + c)),
            pl.BlockSpec((1, tt, cw), lambda c, i, s: (i, s, gc0 + c)),
            pl.BlockSpec((1, SUBLANES, cw), lambda c, i, s: (i, 0, c)),
            pl.BlockSpec((1, 1, cw), lambda c, i, s: (i, 0, c)),
            pl.BlockSpec(((CONV_WIDTH - 1) * tt, tt), lambda c, i, s: (0, 0)),
            pl.BlockSpec((CONV_WIDTH, cw), lambda c, i, s: (0, c)),
            vec,
            pl.BlockSpec((blocks_per_cg, RNN_BLOCK, 2 * RNN_BLOCK), lambda c, i, s: (c, 0, 0)),
            vec, vec, vec,
        ],
        out_specs=[
            pl.BlockSpec((1, tt, cw), lambda c, i, s: (i, s, c)),
            pl.BlockSpec((1, 1, cw), lambda c, i, s: (i, 0, c)),
        ],
        out_shape=[
            jax.ShapeDtypeStruct((b, t, d_rnn), BF16),
            jax.ShapeDtypeStruct((b, 1, d_rnn), F32),
        ],
        scratch_shapes=[
            pltpu.VMEM((SUBLANES, cw), F32),
            pltpu.VMEM((SUBLANES, cw), F32),
        ],
        compiler_params=_params(3, VMEM_RNN),
        name=name,
    )(z3, z3, conv0, h0, _shift_matrix(tt), conv_w, conv_b, w_gates, b_rg, b_ig, lam)


def _lookup(table, j):
    out = jnp.int32(table[-1])
    for k in range(len(table) - 2, -1, -1):
        out = jnp.where(j == k, jnp.int32(table[k]), out)
    return out


def _fused_body(x_ref, g_ref, w_ref, c0_ref, h0_ref, cw_ref, cb_ref, wg_ref, brg_ref, big_ref,
                lam_ref, z_ref, s_ref, ob_ref, hl_ref, xn_ref, stash_ref, prev_ref, hcar_ref, hd_ref, gr_ref, *,
                n_tiles, period, n_cg, pieces_per_cg, tt, n_split, stash_slots):
    i, j = pl.program_id(0), pl.program_id(1)
    cw = ob_ref.shape[1]

    @pl.when((j == 0) & (i < n_tiles))
    def _():
        xn_ref[...] = _rmsnorm_rows(x_ref[...], g_ref[...]).astype(BF16)

    @pl.when((j == 0) & (i == 0))
    def _():
        prev_ref[...] = jnp.zeros_like(prev_ref)
        hcar_ref[...] = jnp.zeros_like(hcar_ref)

    tm = xn_ref.shape[0]
    rows_s = tt // n_split
    state_start = tm - s_ref.shape[0]
    first_rows = tm // 4
    rest_rows = (tm - first_rows) // n_split
    bounds = [0] + [first_rows + c * rest_rows for c in range(n_split + 1)]

    def project(c):
        lo, hi = bounds[c], bounds[c + 1]
        acc = jnp.dot(xn_ref[lo:hi, :], w_ref[...], preferred_element_type=F32)
        zb = acc.astype(BF16)
        z_ref[lo:hi, :] = zb
        if hi > state_start:
            keep = max(lo, state_start)
            s_ref[keep - state_start:hi - state_start, :] = acc[keep - lo:, :]
        stash_ref[_lookup(stash_slots, j), lo:hi, :] = zb
        return acc[hi - lo - SUBLANES:, :]

    def after(state, acc_rows):
        cg, lam, h = state
        zero = (pltpu.bitcast(acc_rows, jnp.uint32) >> 16) >> 16
        return cg, lam, pltpu.bitcast(pltpu.bitcast(h, jnp.uint32) + zero, F32)

    def recur_load():
        cg, tp = j // pieces_per_cg, j % pieces_per_cg
        r0 = pl.multiple_of(tp * tt, tt)
        first = ((i - 1) % period == 0) & (tp == 0)
        prev8 = jnp.where(first, c0_ref[0, cg], prev_ref[cg])
        h = jnp.where(first, jnp.broadcast_to(h0_ref[0, cg], (SUBLANES, cw)), hcar_ref[cg])
        gr_ref[...] = stash_ref[n_cg + cg, pl.ds(r0, tt), :]
        return cg, stash_ref[cg, pl.ds(r0, tt), :], prev8, h

    def recur_head(loaded):
        cg, x_bf, prev8, h = loaded
        xc, r_pre, i_pre, prev8 = _rglru_head(x_bf, prev8, None, cw_ref[cg], cb_ref[cg], wg_ref.at[cg],
                                              brg_ref[cg], big_ref[cg])
        hd_ref[0] = xc
        hd_ref[1] = r_pre
        hd_ref[2] = i_pre
        prev_ref[cg] = prev8
        return cg, lam_ref[cg], h

    def recur_tail(k, state):
        cg, lam, h = state
        lo = k * rows_s
        out, h = _rglru_tail(hd_ref[0, lo:lo + rows_s, :], hd_ref[1, lo:lo + rows_s, :],
                             hd_ref[2, lo:lo + rows_s, :], gr_ref[lo:lo + rows_s, :], h, lam)
        ob_ref[lo:lo + rows_s, :] = out
        return cg, lam, h

    def recur_end(state):
        cg, _, h = state
        hcar_ref[cg] = h
        hl_ref[0, 0] = h[0:1, :]

    has_proj = i < n_tiles
    has_rec = (i >= 1) & (j < n_cg * pieces_per_cg)

    @pl.when(has_proj & has_rec)
    def _():
        state = after(recur_head(recur_load()), project(0))
        for k in range(n_split):
            state = recur_tail(k, state)
            acc_rows = project(k + 1)
            if k + 1 < n_split:
                state = after(state, acc_rows)
        recur_end(state)

    @pl.when(has_proj & jnp.logical_not(has_rec))
    def _():
        for c in range(n_split + 1):
            project(c)

    @pl.when(jnp.logical_not(has_proj) & has_rec)
    def _():
        state = recur_head(recur_load())
        for k in range(n_split):
            state = recur_tail(k, state)
        recur_end(state)


def _in_proj_rglru(x2d, g, w, conv0, h0, conv_w, conv_b, w_gates, b_rg, b_ig, lam, *, tm, tn, tt, period,
                   state_rows, state_col0, state_cols, xr_col0, gr_col0, name):
    m, d = x2d.shape
    n = w.shape[1]
    d_rnn = conv_w.shape[1]
    n_tiles, n_col, n_cg = m // tm, n // tn, d_rnn // tn
    n_seq = n_tiles // period
    pieces_per_cg = tm // tt
    n_pieces = n_cg * pieces_per_cg
    xr0, gr0 = xr_col0 // tn, gr_col0 // tn
    assert n_pieces + n_cg <= n_col and tm % tt == 0 and n_tiles % period == 0

    order, stash_slots = [None] * n_col, [2 * n_cg] * n_col
    for c in range(n_cg):
        jx = (c + 1) * pieces_per_cg
        order[jx], stash_slots[jx] = xr0 + c, c
        order[jx + 1], stash_slots[jx + 1] = gr0 + c, n_cg + c
    rest = [c for c in range(n_col) if c not in order]
    for jj in range(n_col):
        if order[jj] is None:
            order[jj] = rest.pop(0)
    s0, n_state = state_col0 // tn, state_cols // tn
    state_pos = [order.index(s0 + k) for k in range(n_state)]
    assert state_pos == sorted(state_pos)
    state_blk = [sum(p < jj for p in state_pos) for jj in range(n_col)]

    last = lambda i: i == n_tiles
    row = lambda i: jnp.minimum(i, n_tiles - 1)
    col = lambda i, j: jnp.where(last(i), order[-1], _lookup(order, j))
    seq_prev = lambda i: jnp.maximum(i - 1, 0) // period
    piece = lambda i, j: jnp.where(i == 0, 0, jnp.minimum(j, n_pieces - 1))

    def state_block(i, j):
        live = i % period == period - 1
        blk = jnp.where(live, _lookup(state_blk, j), n_state + 1)
        return row(i) // period, jnp.where(last(i), n_state, blk)

    per_cg = lambda a: a.reshape(a.shape[0], n_cg, tn).swapaxes(0, 1)
    whole = lambda a: pl.BlockSpec(a.shape, lambda i, j: (0,) * a.ndim)
    n_split = 4
    assert tt % (n_split * SUBLANES) == 0 and (tm - tm // 4) % (2 * SUBLANES * n_split) == 0
    params = [per_cg(conv_w), per_cg(conv_b), w_gates.reshape(n_cg, -1, RNN_BLOCK, 2 * RNN_BLOCK),
              per_cg(b_rg), per_cg(b_ig), per_cg(lam)]
    conv0 = conv0.reshape(n_seq, SUBLANES, n_cg, tn).swapaxes(1, 2)
    h0 = h0.reshape(n_seq, 1, n_cg, tn).swapaxes(1, 2)

    z, state, o_b, h_last = pl.pallas_call(
        functools.partial(_fused_body, n_tiles=n_tiles, period=period, n_cg=n_cg,
                          pieces_per_cg=pieces_per_cg, tt=tt, n_split=n_split,
                          stash_slots=tuple(stash_slots)),
        grid=(n_tiles + 1, n_col),
        in_specs=[
            pl.BlockSpec((tm, d), lambda i, j: (row(i), 0)),
            pl.BlockSpec((1, d), lambda i, j: (0, 0)),
            pl.BlockSpec((d, tn), lambda i, j: (0, col(i, j))),
            pl.BlockSpec((1, n_cg, SUBLANES, tn), lambda i, j: (seq_prev(i), 0, 0, 0)),
            pl.BlockSpec((1, n_cg, 1, tn), lambda i, j: (seq_prev(i), 0, 0, 0)),
        ] + [whole(p) for p in params],
        out_specs=[
            pl.BlockSpec((tm, tn), lambda i, j: (row(i), col(i, j))),
            pl.BlockSpec((state_rows, tn), state_block),
            pl.BlockSpec((tt, tn), lambda i, j: (jnp.maximum(i - 1, 0) * pieces_per_cg
                                                 + piece(i, j) % pieces_per_cg, piece(i, j) // pieces_per_cg)),
            pl.BlockSpec((1, 1, 1, tn), lambda i, j: (jnp.maximum(i - 1, 0), piece(i, j) // pieces_per_cg, 0, 0)),
        ],
        out_shape=[
            jax.ShapeDtypeStruct((m, n), BF16),
            jax.ShapeDtypeStruct((n_seq * state_rows, (n_state + 2) * tn), F32),
            jax.ShapeDtypeStruct((m, d_rnn), BF16),
            jax.ShapeDtypeStruct((n_tiles, n_cg, 1, tn), F32),
        ],
        scratch_shapes=[
            pltpu.VMEM((tm, d), BF16),
            pltpu.VMEM((2 * n_cg + 1, tm, tn), BF16),
            pltpu.VMEM((n_cg, SUBLANES, tn), F32),
            pltpu.VMEM((n_cg, SUBLANES, tn), F32),
            pltpu.VMEM((3, tt, tn), F32),
            pltpu.VMEM((tt, tn), BF16),
        ],
        compiler_params=_params(2, VMEM_FUSED),
        name=name,
    )(x2d, g, w, conv0, h0, *params)
    return z, state, o_b, h_last.reshape(n_tiles, d_rnn)[period - 1::period]


def _merge_body(oa_ref, ob_ref, ga_ref, gb_ref, wa_ref, wb_ref, o_ref):
    tm = o_ref.shape[0]
    for c in range(MERGE_CHUNKS):
        lo, hi = c * tm // MERGE_CHUNKS, (c + 1) * tm // MERGE_CHUNKS
        ya = jnp.dot(oa_ref[lo:hi, :], wa_ref[...], preferred_element_type=F32)
        yb = jnp.dot(ob_ref[lo:hi, :], wb_ref[...], preferred_element_type=F32)
        ga = jax.nn.sigmoid(ga_ref[lo:hi, :].astype(F32))
        gb = jax.nn.sigmoid(gb_ref[lo:hi, :].astype(F32))
        o_ref[lo:hi, :] = (ga * ya + gb * yb).astype(o_ref.dtype)


def _gate_merge(o_a, o_b, z2, w_a, w_b, *, ga_col0, gb_col0, tm, tn, name):
    m = o_a.shape[0]
    d = w_a.shape[1]
    ga0, gb0 = ga_col0 // tn, gb_col0 // tn
    return pl.pallas_call(
        _merge_body,
        grid=(m // tm, d // tn),
        in_specs=[
            pl.BlockSpec((tm, o_a.shape[1]), lambda i, j: (i, 0)),
            pl.BlockSpec((tm, o_b.shape[1]), lambda i, j: (i, 0)),
            pl.BlockSpec((tm, tn), lambda i, j: (i, ga0 + j)),
            pl.BlockSpec((tm, tn), lambda i, j: (i, gb0 + j)),
            pl.BlockSpec((w_a.shape[0], tn), lambda i, j: (0, j)),
            pl.BlockSpec((w_b.shape[0], tn), lambda i, j: (0, j)),
        ],
        out_specs=pl.BlockSpec((tm, tn), lambda i, j: (i, j)),
        out_shape=jax.ShapeDtypeStruct((m, d), BF16),
        compiler_params=_params(2, VMEM_MLP),
        name=name,
    )(o_a, o_b, z2, z2, w_a, w_b)


def _out_norm_body(m_ref, w_ref, x_ref, g_ref, o_ref):
    y = jnp.dot(m_ref[...], w_ref[...], preferred_element_type=F32)
    o_ref[...] = x_ref[...] + _rmsnorm_rows(y, g_ref[...])


def _out_norm(merged, w_out, x2d, g, *, tm, name):
    m, d = x2d.shape
    return pl.pallas_call(
        _out_norm_body,
        grid=(m // tm,),
        in_specs=[
            pl.BlockSpec((tm, d), lambda i: (i, 0)),
            pl.BlockSpec((d, d), lambda i: (0, 0)),
            pl.BlockSpec((tm, d), lambda i: (i, 0)),
            pl.BlockSpec((1, d), lambda i: (0, 0)),
        ],
        out_specs=pl.BlockSpec((tm, d), lambda i: (i, 0)),
        out_shape=jax.ShapeDtypeStruct((m, d), F32),
        compiler_params=_params(1, VMEM_MLP),
        name=name,
    )(merged, w_out, x2d, g)


def _ffn_body(x_ref, g1_ref, w1_ref, w2_ref, g2_ref, o_ref, xn_ref, *, n_steps, edge_chunks):
    f = pl.program_id(1)

    def rows_chunk(lo, hi, first, last):
        if first:
            xn = _rmsnorm_rows(x_ref[lo:hi, :], g1_ref[...]).astype(BF16)
            xn_ref[lo:hi, :] = xn
        else:
            xn = xn_ref[lo:hi, :]
        hid = jnp.dot(xn, w1_ref[...], preferred_element_type=F32)
        hid = jnp.square(jnp.maximum(hid, 0.0)).astype(BF16)
        acc = jnp.dot(hid, w2_ref[...], preferred_element_type=F32)
        if not first:
            acc = o_ref[lo:hi, :] + acc
        if last:
            acc = x_ref[lo:hi, :] + _rmsnorm_rows(acc, g2_ref[...])
        o_ref[lo:hi, :] = acc

    tm = o_ref.shape[0]
    for first, last in sorted({(k == 0, k == n_steps - 1) for k in range(n_steps)}):
        chunks = edge_chunks if (first or last) else 1

        @pl.when(((f == 0) == first) & ((f == n_steps - 1) == last))
        def _(first=first, last=last, chunks=chunks):
            for c in range(chunks):
                rows_chunk(c * tm // chunks, (c + 1) * tm // chunks, first, last)


def _ffn(x2d, g1, w1, w2, g2, *, tm, tf, name):
    m, d = x2d.shape
    d_ff = w1.shape[1]
    return pl.pallas_call(
        functools.partial(_ffn_body, n_steps=d_ff // tf, edge_chunks=FFN_EDGE_CHUNKS),
        grid=(m // tm, d_ff // tf),
        in_specs=[
            pl.BlockSpec((tm, d), lambda i, f: (i, 0)),
            pl.BlockSpec((1, d), lambda i, f: (0, 0)),
            pl.BlockSpec((d, tf), lambda i, f: (0, f)),
            pl.BlockSpec((tf, d), lambda i, f: (f, 0)),
            pl.BlockSpec((1, d), lambda i, f: (0, 0)),
        ],
        out_specs=pl.BlockSpec((tm, d), lambda i, f: (i, 0)),
        out_shape=jax.ShapeDtypeStruct((m, d), F32),
        scratch_shapes=[pltpu.VMEM((tm, d), BF16)],
        compiler_params=_params(2, VMEM_FFN),
        name=name,
    )(x2d, g1, w1, w2, g2)


def _pick_tile(n, target):
    t = min(n, target)
    while n % t:
        t //= 2
    return t


def _layer(x, cache_k, cache_v, conv_state, h_state, wts, tag):
    b, t, d = x.shape
    m = b * t
    x2d = x.reshape(m, d)
    d_rnn = wts["conv_w"].shape[1]
    n_in = wts["w_in"].shape[1]
    col_k, col_v, col_xr = D_ATTN, 2 * D_ATTN, 3 * D_ATTN
    col_gr = col_xr + d_rnn
    col_ga = col_gr + d_rnn
    col_gb = col_ga + d

    tm = _pick_tile(m, ROW_TILE)
    tn = COL_TILE
    keep = min(KV_REACH, t)
    if keep == t:
        period, state_rows = 1, tm
    else:
        assert t % tm == 0 and keep <= tm
        period, state_rows = t // tm, keep
    rnn_w = (wts["conv_w"], wts["conv_b"], wts["w_gates"], wts["b_rg"], wts["b_ig"], wts["lru_lambda"])
    if cache_k is None:
        conv0 = jnp.zeros((b, SUBLANES, d_rnn), F32)
        h0 = jnp.zeros((b, 1, d_rnn), F32)
        z2, state, o_b, h_last = _in_proj_rglru(
            x2d, wts["pre_mix_g"], wts["w_in"], conv0, h0, *rnn_w, tm=tm, tn=tn, tt=_pick_tile(tm, RNN_FRAMES),
            period=period, state_rows=state_rows, state_col0=col_k, state_cols=col_gr - col_k,
            xr_col0=col_xr, gr_col0=col_gr, name=f"in_proj_rglru_{tag}")
        z3 = z2.reshape(b, t, n_in)
        offsets = [i * CHUNK for i in range(PAST_CHUNKS + 2)]
        o_a = _attention(z3, z3, z3, wts["rel_bias"], q_col0=0, k_col0=col_k, v_col0=col_v, bb=1,
                         heads=HEADS_PER_STEP, offsets=offsets, name=f"attn_{tag}")
    else:
        z2, state = _in_proj(x2d, wts["pre_mix_g"], wts["w_in"], tm=tm, tn=tn, period=period,
                             state_rows=state_rows, state_col0=col_k, state_cols=col_gr - col_k,
                             name=f"in_proj_{tag}")
        z3 = z2.reshape(b, t, n_in)
        n_cached = cache_k.shape[1]
        assert t == CHUNK and n_cached == KV_REACH
        o_a = _attention(z3, z3, z3, wts["rel_bias"], q_col0=0, k_col0=col_k, v_col0=col_v,
                         bb=_pick_tile(b, SAMPLE_SEQS), heads=CACHED_HEADS_PER_STEP, offsets=[n_cached],
                         name=f"attn_{tag}",
                         k_cache=cache_k.astype(BF16).reshape(b, n_cached, D_ATTN),
                         v_cache=cache_v.astype(BF16).reshape(b, n_cached, D_ATTN))
        conv0 = jnp.pad(conv_state.astype(F32), ((0, 0), (SUBLANES - (CONV_WIDTH - 1), 0), (0, 0)))
        h0 = h_state.astype(F32).reshape(b, 1, d_rnn)
        o_b, h_last = _rglru(z3, conv0, h0, *rnn_w, xr_col0=col_xr, gr_col0=col_gr,
                             tt=_pick_tile(t, RNN_FRAMES), cw=COL_TILE, name=f"rglru_{tag}")
    state = state.reshape(b, keep, -1)
    new_k = state[:, :, :D_ATTN].reshape(b, keep, N_HEADS, HEAD_DIM)
    new_v = state[:, :, D_ATTN:2 * D_ATTN].reshape(b, keep, N_HEADS, HEAD_DIM)
    conv_tail = state[:, keep - (CONV_WIDTH - 1):, 2 * D_ATTN:col_gr - col_k]

    merged = _gate_merge(o_a.reshape(m, D_ATTN), o_b.reshape(m, d_rnn), z2, wts["w_attn_up"],
                         wts["w_rnn_up"], ga_col0=col_ga, gb_col0=col_gb, tm=tm, tn=tn,
                         name=f"gate_merge_{tag}")
    tm2 = _pick_tile(m, WIDE_ROW_TILE)
    x1 = _out_norm(merged, wts["w_out"], x2d, wts["post_mix_g"], tm=tm2, name=f"out_norm_{tag}")
    tf = FFN_TILE if m // tm2 >= FFN_WIDE_MIN_ROW_TILES else FFN_TILE // 2
    y = _ffn(x1, wts["pre_ffn_g"], wts["w_ff1"], wts["w_ff2"], wts["post_ffn_g"], tm=tm2, tf=tf,
             name=f"ffn_{tag}")
    return y.reshape(b, t, d), new_k, new_v, conv_tail, h_last.reshape(b, d_rnn)


def kernel(x_prompt, x_sample, cache_k, cache_v, state_conv, state_h, pre_mix_g, w_in, rel_bias, conv_w, conv_b, w_rg, b_rg, w_ig, b_ig, lru_lambda, w_attn_up, w_rnn_up, w_out, post_mix_g, pre_ffn_g, w_ff1, w_ff2, post_ffn_g):
    depth = w_in.shape[0]
    y_p, y_s = x_prompt, x_sample
    outs_p, outs_s = [], []
    for l in range(depth):
        wts = {
            "pre_mix_g": pre_mix_g[l][None], "post_mix_g": post_mix_g[l][None],
            "pre_ffn_g": pre_ffn_g[l][None], "post_ffn_g": post_ffn_g[l][None],
            "w_in": w_in[l].astype(BF16), "rel_bias": rel_bias[l],
            "conv_w": conv_w[l], "conv_b": conv_b[l][None],
            "w_gates": jnp.concatenate([w_rg[l], w_ig[l]], axis=-1).astype(BF16),
            "b_rg": b_rg[l][None], "b_ig": b_ig[l][None], "lru_lambda": lru_lambda[l][None],
            "w_attn_up": w_attn_up[l].astype(BF16), "w_rnn_up": w_rnn_up[l].astype(BF16),
            "w_out": w_out[l].astype(BF16), "w_ff1": w_ff1[l].astype(BF16), "w_ff2": w_ff2[l].astype(BF16),
        }
        y_p, *st_p = _layer(y_p, None, None, None, None, wts, f"p{l}")
        y_s, *st_s = _layer(y_s, cache_k[l], cache_v[l], state_conv[l], state_h[l], wts, f"s{l}")
        outs_p.append(st_p)
        outs_s.append(st_s)
    stack = lambda outs, i: jnp.stack([o[i] for o in outs])
    return (y_p, y_s,
            stack(outs_p, 0), stack(outs_p, 1), stack(outs_p, 2), stack(outs_p, 3),
            stack(outs_s, 0), stack(outs_s, 1), stack(outs_s, 2), stack(outs_s, 3))
```

```python
import functools

import jax
import jax.numpy as jnp
from jax import lax
from jax.experimental import pallas as pl
from jax.experimental.pallas import tpu as pltpu

F32 = jnp.float32
BF16 = jnp.bfloat16

CHUNK = 64
PAST_CHUNKS = 8
KV_REACH = PAST_CHUNKS * CHUNK
N_HEADS = 16
HEAD_DIM = 64
D_ATTN = N_HEADS * HEAD_DIM
MAX_REL = 256
RNN_BLOCK = 128
CONV_WIDTH = 4
LRU_C = 8.0
EPS = 1e-6
NEG_INF = -1e30

LANES = 128
SUBLANES = 8
MIB = 1024 * 1024

ROW_TILE = 1024
COL_TILE = 1024
WIDE_ROW_TILE = 512
FFN_TILE = 2048
FFN_WIDE_MIN_ROW_TILES = 16
RNN_FRAMES = 256
HEADS_PER_STEP = 4
CACHED_HEADS_PER_STEP = 2
SAMPLE_SEQS = 8
KT_BLOCK = 512
MERGE_CHUNKS = 4
FFN_EDGE_CHUNKS = 2
VMEM_PROJ, VMEM_FUSED, VMEM_ATTN, VMEM_RNN, VMEM_MLP, VMEM_FFN = 52, 58, 40, 40, 48, 60


def _params(n_axes, vmem_mib):
    return pltpu.CompilerParams(
        dimension_semantics=("arbitrary",) * n_axes,
        vmem_limit_bytes=vmem_mib * MIB,
    )


def _rmsnorm_rows(x, g):
    ms = jnp.mean(x * x, axis=-1, keepdims=True)
    return x * lax.rsqrt(ms + EPS) * g


def _in_proj_body(x_ref, g_ref, w_ref, o_ref, s_ref, xn_ref, *, period, j0, j1):
    i, j = pl.program_id(0), pl.program_id(1)

    @pl.when(j == 0)
    def _():
        xn_ref[...] = _rmsnorm_rows(x_ref[...], g_ref[...]).astype(BF16)

    keep_f32 = (i % period == period - 1) & (j >= j0) & (j < j1)

    @pl.when(keep_f32)
    def _():
        acc = jnp.dot(xn_ref[...], w_ref[...], preferred_element_type=F32)
        o_ref[...] = acc.astype(o_ref.dtype)
        s_ref[...] = acc[acc.shape[0] - s_ref.shape[0]:, :]

    @pl.when(jnp.logical_not(keep_f32))
    def _():
        o_ref[...] = jnp.dot(xn_ref[...], w_ref[...], preferred_element_type=F32).astype(o_ref.dtype)


def _in_proj(x2d, g, w, *, tm, tn, period, state_rows, state_col0, state_cols, name):
    m, d = x2d.shape
    n = w.shape[1]
    j0, j1 = state_col0 // tn, (state_col0 + state_cols) // tn
    n_state_tiles = m // tm // period

    def state_block(i, j):
        live = i % period == period - 1
        return i // period, jnp.where(live, jnp.clip(j - j0, 0, j1 - j0 - 1), 0)

    return pl.pallas_call(
        functools.partial(_in_proj_body, period=period, j0=j0, j1=j1),
        grid=(m // tm, n // tn),
        in_specs=[
            pl.BlockSpec((tm, d), lambda i, j: (i, 0)),
            pl.BlockSpec((1, d), lambda i, j: (0, 0)),
            pl.BlockSpec((d, tn), lambda i, j: (0, j)),
        ],
        out_specs=[
            pl.BlockSpec((tm, tn), lambda i, j: (i, j)),
            pl.BlockSpec((state_rows, tn), state_block),
        ],
        out_shape=[
            jax.ShapeDtypeStruct((m, n), BF16),
            jax.ShapeDtypeStruct((n_state_tiles * state_rows, state_cols), F32),
        ],
        scratch_shapes=[pltpu.VMEM((tm, d), BF16)],
        compiler_params=_params(2, VMEM_PROJ),
        name=name,
    )(x2d, g, w)


def _bias_diagonals(rel_bias, width, offsets, heads):
    x = jnp.arange(width)
    rows = [rel_bias[:, jnp.clip(off + CHUNK - 1 - x, -MAX_REL, MAX_REL) + MAX_REL] for off in offsets]
    g = jnp.stack(rows, axis=1).astype(F32)
    n_groups = rel_bias.shape[0] // heads
    g = g.reshape(n_groups, heads, len(offsets), width)
    return jnp.swapaxes(g, 1, 2).reshape(n_groups, len(offsets) * heads, width)


def _attn_body(*refs, bb, n_chunks, kw, heads, offsets, unit, cached):
    if cached:
        q_ref, kc_ref, vc_ref, k_ref, v_ref, g_ref, o_ref, tab_ref, s_ref, p_ref, l_ref = refs
        band = lambda c_ref, n_ref, bi, ks: jnp.concatenate([c_ref[bi], n_ref[bi]], axis=0)
    else:
        q_ref, k_ref, v_ref, g_ref, o_ref, tab_ref, s_ref, p_ref, l_ref, kt_ref = refs
        kc_ref = vc_ref = None
        band = lambda c_ref, n_ref, bi, ks: n_ref[bi, pl.ds(ks, kw), :]
    width = g_ref.shape[2]
    hw = heads * HEAD_DIM

    @pl.when(pl.program_id(1) == 0)
    def _():
        r = lax.broadcasted_iota(jnp.int32, (CHUNK, kw), 0)
        j = lax.broadcasted_iota(jnp.int32, (CHUNK, kw), 1)
        for c, off in enumerate(offsets):
            cq = (r + off) // CHUNK
            ck = j // CHUNK
            valid = (ck <= cq) & (ck >= cq - PAST_CHUNKS)
            for h in range(heads):
                g = jnp.broadcast_to(g_ref[0, c * heads + h:c * heads + h + 1, :], (CHUNK, width))
                t = pltpu.roll(g, width - (CHUNK - 1), 1, stride=1, stride_axis=0)
                tab_ref[c, h * CHUNK:(h + 1) * CHUNK, :] = jnp.where(valid, t[:, :kw], NEG_INF)

    if not cached:
        for r0 in range(0, k_ref.shape[1], KT_BLOCK):
            kt_ref[:, r0:r0 + KT_BLOCK] = k_ref[0, r0:r0 + KT_BLOCK, :].T

    lane = lax.broadcasted_iota(jnp.int32, (1, hw), 1)
    head_lanes = [(lane >= h * HEAD_DIM) & (lane < (h + 1) * HEAD_DIM) for h in range(heads)]
    rows = heads * CHUNK
    n_units = bb * n_chunks // unit

    def locate(u, j):
        idx = u * unit + j
        bi, c = (0, idx) if bb == 1 else (idx // n_chunks, idx % n_chunks)
        q0 = pl.multiple_of(c * CHUNK, CHUNK)
        if cached:
            return bi, 0, q0, 0
        past = jnp.maximum(c - PAST_CHUNKS, 0)
        odd = past % 2
        ks = pl.multiple_of((past - odd) * CHUNK, 2 * CHUNK)
        return bi, jnp.minimum(c, PAST_CHUNKS) + odd, q0, ks

    def scores(u, slot):
        for j in range(unit):
            bi, c, q0, ks = locate(u, j)
            q = q_ref[bi, pl.ds(q0, CHUNK), :]
            qs = jnp.concatenate([jnp.where(m, q, jnp.zeros_like(q)) for m in head_lanes], axis=0)
            qs = qs * jnp.asarray(HEAD_DIM ** -0.5, q.dtype)
            if cached:
                kb = band(kc_ref, k_ref, bi, ks)
                s = lax.dot_general(qs, kb, (((1,), (1,)), ((), ())), preferred_element_type=F32)
            else:
                s = jnp.dot(qs, kt_ref[:, pl.ds(ks, kw)], preferred_element_type=F32)
            s_ref[slot, j * rows:(j + 1) * rows, :] = s + tab_ref[c]

    def softmax(slot):
        for r0 in range(0, unit * rows, CHUNK):
            s = s_ref[slot, r0:r0 + CHUNK, :]
            p = jnp.exp(s - jnp.max(s, axis=-1, keepdims=True))
            p_ref[slot, r0:r0 + CHUNK, :] = p.astype(p_ref.dtype)
            l_ref[slot, r0:r0 + CHUNK, :] = 1.0 / jnp.sum(p, axis=-1, keepdims=True)

    def output(u, slot):
        for j in range(unit):
            bi, c, q0, ks = locate(u, j)
            vb = band(vc_ref, v_ref, bi, ks)
            o_all = jnp.dot(p_ref[slot, j * rows:(j + 1) * rows, :], vb, preferred_element_type=F32)
            o_all = o_all * l_ref[slot, j * rows:(j + 1) * rows, :]
            o = o_all[0:CHUNK]
            for h in range(1, heads):
                o = jnp.where(head_lanes[h], o_all[h * CHUNK:(h + 1) * CHUNK], o)
            o_ref[bi, pl.ds(q0, CHUNK), :] = o.astype(o_ref.dtype)

    def step(i, slot, do_scores, do_softmax, do_output):
        if do_scores:
            scores(i, slot)
        if do_output:
            output(i - 2, slot)
        if do_softmax:
            softmax(1 - slot)

    for i in range(2):
        step(i, i % 2, i < n_units, 1 <= i <= n_units, False)
    n_pairs = max(n_units - 2, 0) // 2
    if n_pairs:
        def body(t, carry):
            i = 2 + 2 * t
            step(i, 0, True, True, True)
            step(i + 1, 1, True, True, True)
            return carry
        lax.fori_loop(0, n_pairs, body, 0)
    for i in range(2 + 2 * n_pairs, n_units + 2):
        step(i, i % 2, i < n_units, 1 <= i <= n_units, i >= 2)


def _attention(q_arr, k_arr, v_arr, rel_bias, *, q_col0, k_col0, v_col0, bb, heads, offsets, name,
               k_cache=None, v_cache=None):
    b, sq, _ = q_arr.shape
    sk = k_arr.shape[1]
    cached = k_cache is not None
    if cached:
        kw = KV_REACH + CHUNK
        assert sq == CHUNK and sk == CHUNK and k_cache.shape[1] == KV_REACH and len(offsets) == 1
    else:
        kw = KV_REACH + 2 * CHUNK
        assert bb == 1 and sq == sk and sk % KT_BLOCK == 0 and sk >= kw and len(offsets) == PAST_CHUNKS + 2
    hw = heads * HEAD_DIM
    n_groups = N_HEADS // heads
    n_case = len(offsets)
    width = -(-(kw + CHUNK - 1) // LANES) * LANES
    diag = _bias_diagonals(rel_bias, width, offsets, heads)
    qc, kc, vc = q_col0 // hw, k_col0 // hw, v_col0 // hw
    unit = 2
    assert (bb * sq // CHUNK) % unit == 0
    rows = unit * heads * CHUNK
    cache_specs = [pl.BlockSpec((bb, KV_REACH, hw), lambda g, i: (i, 0, g))] * 2 if cached else []
    cache_args = [k_cache, v_cache] if cached else []
    return pl.pallas_call(
        functools.partial(_attn_body, bb=bb, n_chunks=sq // CHUNK, kw=kw, heads=heads,
                          offsets=tuple(offsets), unit=unit, cached=cached),
        grid=(n_groups, b // bb),
        in_specs=[pl.BlockSpec((bb, sq, hw), lambda g, i: (i, 0, qc + g))] + cache_specs + [
            pl.BlockSpec((bb, sk, hw), lambda g, i: (i, 0, kc + g)),
            pl.BlockSpec((bb, sk, hw), lambda g, i: (i, 0, vc + g)),
            pl.BlockSpec((1, n_case * heads, width), lambda g, i: (g, 0, 0)),
        ],
        out_specs=pl.BlockSpec((bb, sq, hw), lambda g, i: (i, 0, g)),
        out_shape=jax.ShapeDtypeStruct((b, sq, D_ATTN), BF16),
        scratch_shapes=[
            pltpu.VMEM((n_case, heads * CHUNK, kw), F32),
            pltpu.VMEM((2, rows, kw), F32),
            pltpu.VMEM((2, rows, kw), BF16),
            pltpu.VMEM((2, rows, 1), F32),
        ] + ([] if cached else [pltpu.VMEM((hw, sk), BF16)]),
        compiler_params=_params(2, VMEM_ATTN),
        name=name,
    )(q_arr, *cache_args, k_arr, v_arr, diag)


def _log_sigmoid(x):
    return jnp.minimum(x, 0.0) - jnp.log1p(jnp.exp(-jnp.abs(x)))


def _shift_matrix(tt):
    t = jnp.arange(tt)
    blocks = [(t[:, None] - d) == t[None, :] for d in range(CONV_WIDTH - 1, 0, -1)]
    return jnp.concatenate(blocks, axis=0).astype(BF16)


def _rglru_head(x_bf, prev8, shift, conv_w, conv_b, wg_ref, b_rg, b_ig):
    tt, cw = x_bf.shape
    taps = CONV_WIDTH - 1
    x = x_bf.astype(F32)
    row = lax.broadcasted_iota(jnp.int32, (SUBLANES, cw), 0)
    if shift is not None:
        sh = jnp.dot(shift, x_bf, preferred_element_type=F32)
        xc = conv_b + conv_w[0:1] * sh[0:tt]
        for k in range(1, taps):
            xc = xc + conv_w[k:k + 1] * sh[k * tt:(k + 1) * tt]
        xc = xc + conv_w[taps:taps + 1] * x
        head = xc[0:SUBLANES]
        for k in range(taps):
            d = taps - k
            head = head + conv_w[k:k + 1] * jnp.where(row < d, pltpu.roll(prev8, d, 0), 0.0)
        xc = jnp.concatenate([head, xc[SUBLANES:]], axis=0)
    else:
        xc = conv_b
        for k in range(taps):
            d = taps - k
            sh = pltpu.roll(x, d, 0)
            head = jnp.where(row < d, pltpu.roll(prev8, d, 0), sh[0:SUBLANES])
            xc = xc + conv_w[k:k + 1] * jnp.concatenate([head, sh[SUBLANES:]], axis=0)
        xc = xc + conv_w[taps:taps + 1] * x

    xcb = xc.astype(BF16)
    r_parts, i_parts = [], []
    for n in range(cw // RNN_BLOCK):
        gts = jnp.dot(xcb[:, n * RNN_BLOCK:(n + 1) * RNN_BLOCK], wg_ref[n], preferred_element_type=F32)
        r_parts.append(gts[:, :RNN_BLOCK])
        i_parts.append(gts[:, RNN_BLOCK:])
    r_pre = jnp.concatenate(r_parts, axis=1) + b_rg
    i_pre = jnp.concatenate(i_parts, axis=1) + b_ig
    return xc, r_pre, i_pre, x[tt - SUBLANES:tt]


def _rglru_tail(xc, r_pre, i_pre, gr_bf, h_in, lam):
    rows, cw = xc.shape
    r = jax.nn.sigmoid(r_pre)
    i = jax.nn.sigmoid(i_pre)
    log_a = r * (LRU_C * _log_sigmoid(lam))
    a = jnp.exp(log_a)
    w = -jnp.tanh(log_a)
    one_minus_a2 = 2.0 * w / (1.0 + w)
    root = jnp.where(one_minus_a2 > 0.0, one_minus_a2 * lax.rsqrt(one_minus_a2), 0.0)
    u = root * (i * xc)

    row = lax.broadcasted_iota(jnp.int32, (SUBLANES, cw), 0)
    h = h_in
    hs = []
    for g in range(rows // SUBLANES):
        ag = a[g * SUBLANES:(g + 1) * SUBLANES]
        ug = u[g * SUBLANES:(g + 1) * SUBLANES]
        for d in (1, 2, 4):
            keep = row >= d
            a_prev = jnp.where(keep, pltpu.roll(ag, d, 0), 1.0)
            u_prev = jnp.where(keep, pltpu.roll(ug, d, 0), 0.0)
            ug = ag * u_prev + ug
            ag = ag * a_prev
        hg = ug + ag * h
        hs.append(hg)
        h = jnp.broadcast_to(hg[SUBLANES - 1:SUBLANES, :], (SUBLANES, cw))
    gate = jax.nn.gelu(gr_bf.astype(F32), approximate=True)
    return (jnp.concatenate(hs, axis=0) * gate).astype(BF16), h


def _rglru_body(xr_ref, gr_ref, c0_ref, h0_ref, sh_ref, cw_ref, cb_ref, wg_ref, brg_ref, big_ref, lam_ref,
                ob_ref, hl_ref, prev_ref, hcar_ref):
    @pl.when(pl.program_id(2) == 0)
    def _():
        hcar_ref[...] = jnp.broadcast_to(h0_ref[0], hcar_ref.shape)
        prev_ref[...] = c0_ref[0]

    xc, r_pre, i_pre, prev8 = _rglru_head(xr_ref[0], prev_ref[...], sh_ref[...], cw_ref[...], cb_ref[...],
                                          wg_ref, brg_ref[...], big_ref[...])
    out, h = _rglru_tail(xc, r_pre, i_pre, gr_ref[0], hcar_ref[...], lam_ref[...])
    ob_ref[0] = out
    prev_ref[...] = prev8
    hcar_ref[...] = h
    hl_ref[0] = h[0:1, :]


def _rglru(z3, conv0, h0, conv_w, conv_b, w_gates, b_rg, b_ig, lam, *, xr_col0, gr_col0, tt, cw, name):
    b, t, _ = z3.shape
    d_rnn = conv_w.shape[1]
    n_cg = d_rnn // cw
    blocks_per_cg = cw // RNN_BLOCK
    xc0, gc0 = xr_col0 // cw, gr_col0 // cw
    vec = pl.BlockSpec((1, cw), lambda c, i, s: (0, c))
    return pl.pallas_call(
        _rglru_body,
        grid=(n_cg, b, t // tt),
        in_specs=[
            pl.BlockSpec((1, tt, cw), lambda c, i, s: (i, s, xc0 + c)),
            pl.BlockSpec((1, tt, cw), lambda c, i, s: (i, s, gc0 + c)),
            pl.BlockSpec((1, SUBLANES, cw), lambda c, i, s: (i, 0, c)),
            pl.BlockSpec((1, 1, cw), lambda c, i, s: (i, 0, c)),
            pl.BlockSpec(((CONV_WIDTH - 1) * tt, tt), lambda c, i, s: (0, 0)),
            pl.BlockSpec((CONV_WIDTH, cw), lambda c, i, s: (0, c)),
            vec,
            pl.BlockSpec((blocks_per_cg, RNN_BLOCK, 2 * RNN_BLOCK), lambda c, i, s: (c, 0, 0)),
            vec, vec, vec,
        ],
        out_specs=[
            pl.BlockSpec((1, tt, cw), lambda c, i, s: (i, s, c)),
            pl.BlockSpec((1, 1, cw), lambda c, i, s: (i, 0, c)),
        ],
        out_shape=[
            jax.ShapeDtypeStruct((b, t, d_rnn), BF16),
            jax.ShapeDtypeStruct((b, 1, d_rnn), F32),
        ],
        scratch_shapes=[
            pltpu.VMEM((SUBLANES, cw), F32),
            pltpu.VMEM((SUBLANES, cw), F32),
        ],
        compiler_params=_params(3, VMEM_RNN),
        name=name,
    )(z3, z3, conv0, h0, _shift_matrix(tt), conv_w, conv_b, w_gates, b_rg, b_ig, lam)


def _lookup(table, j):
    out = jnp.int32(table[-1])
    for k in range(len(table) - 2, -1, -1):
        out = jnp.where(j == k, jnp.int32(table[k]), out)
    return out


def _fused_body(x_ref, g_ref, w_ref, c0_ref, h0_ref, cw_ref, cb_ref, wg_ref, brg_ref, big_ref,
                lam_ref, z_ref, s_ref, ob_ref, hl_ref, xn_ref, stash_ref, prev_ref, hcar_ref, hd_ref, gr_ref, *,
                n_tiles, period, n_cg, pieces_per_cg, tt, n_split, stash_slots):
    i, j = pl.program_id(0), pl.program_id(1)
    cw = ob_ref.shape[1]

    @pl.when((j == 0) & (i < n_tiles))
    def _():
        xn_ref[...] = _rmsnorm_rows(x_ref[...], g_ref[...]).astype(BF16)

    @pl.when((j == 0) & (i == 0))
    def _():
        prev_ref[...] = jnp.zeros_like(prev_ref)
        hcar_ref[...] = jnp.zeros_like(hcar_ref)

    tm = xn_ref.shape[0]
    rows_s = tt // n_split
    state_start = tm - s_ref.shape[0]
    first_rows = tm // 4
    rest_rows = (tm - first_rows) // n_split
    bounds = [0] + [first_rows + c * rest_rows for c in range(n_split + 1)]

    def project(c):
        lo, hi = bounds[c], bounds[c + 1]
        acc = jnp.dot(xn_ref[lo:hi, :], w_ref[...], preferred_element_type=F32)
        zb = acc.astype(BF16)
        z_ref[lo:hi, :] = zb
        if hi > state_start:
            keep = max(lo, state_start)
            s_ref[keep - state_start:hi - state_start, :] = acc[keep - lo:, :]
        stash_ref[_lookup(stash_slots, j), lo:hi, :] = zb
        return acc[hi - lo - SUBLANES:, :]

    def after(state, acc_rows):
        cg, lam, h = state
        zero = (pltpu.bitcast(acc_rows, jnp.uint32) >> 16) >> 16
        return cg, lam, pltpu.bitcast(pltpu.bitcast(h, jnp.uint32) + zero, F32)

    def recur_load():
        cg, tp = j // pieces_per_cg, j % pieces_per_cg
        r0 = pl.multiple_of(tp * tt, tt)
        first = ((i - 1) % period == 0) & (tp == 0)
        prev8 = jnp.where(first, c0_ref[0, cg], prev_ref[cg])
        h = jnp.where(first, jnp.broadcast_to(h0_ref[0, cg], (SUBLANES, cw)), hcar_ref[cg])
        gr_ref[...] = stash_ref[n_cg + cg, pl.ds(r0, tt), :]
        return cg, stash_ref[cg, pl.ds(r0, tt), :], prev8, h

    def recur_head(loaded):
        cg, x_bf, prev8, h = loaded
        xc, r_pre, i_pre, prev8 = _rglru_head(x_bf, prev8, None, cw_ref[cg], cb_ref[cg], wg_ref.at[cg],
                                              brg_ref[cg], big_ref[cg])
        hd_ref[0] = xc
        hd_ref[1] = r_pre
        hd_ref[2] = i_pre
        prev_ref[cg] = prev8
        return cg, lam_ref[cg], h

    def recur_tail(k, state):
        cg, lam, h = state
        lo = k * rows_s
        out, h = _rglru_tail(hd_ref[0, lo:lo + rows_s, :], hd_ref[1, lo:lo + rows_s, :],
                             hd_ref[2, lo:lo + rows_s, :], gr_ref[lo:lo + rows_s, :], h, lam)
        ob_ref[lo:lo + rows_s, :] = out
        return cg, lam, h

    def recur_end(state):
        cg, _, h = state
        hcar_ref[cg] = h
        hl_ref[0, 0] = h[0:1, :]

    has_proj = i < n_tiles
    has_rec = (i >= 1) & (j < n_cg * pieces_per_cg)

    @pl.when(has_proj & has_rec)
    def _():
        state = after(recur_head(recur_load()), project(0))
        for k in range(n_split):
            state = recur_tail(k, state)
            acc_rows = project(k + 1)
            if k + 1 < n_split:
                state = after(state, acc_rows)
        recur_end(state)

    @pl.when(has_proj & jnp.logical_not(has_rec))
    def _():
        for c in range(n_split + 1):
            project(c)

    @pl.when(jnp.logical_not(has_proj) & has_rec)
    def _():
        state = recur_head(recur_load())
        for k in range(n_split):
            state = recur_tail(k, state)
        recur_end(state)


def _in_proj_rglru(x2d, g, w, conv0, h0, conv_w, conv_b, w_gates, b_rg, b_ig, lam, *, tm, tn, tt, period,
                   state_rows, state_col0, state_cols, xr_col0, gr_col0, name):
    m, d = x2d.shape
    n = w.shape[1]
    d_rnn = conv_w.shape[1]
    n_tiles, n_col, n_cg = m // tm, n // tn, d_rnn // tn
    n_seq = n_tiles // period
    pieces_per_cg = tm // tt
    n_pieces = n_cg * pieces_per_cg
    xr0, gr0 = xr_col0 // tn, gr_col0 // tn
    assert n_pieces + n_cg <= n_col and tm % tt == 0 and n_tiles % period == 0

    order, stash_slots = [None] * n_col, [2 * n_cg] * n_col
    for c in range(n_cg):
        jx = (c + 1) * pieces_per_cg
        order[jx], stash_slots[jx] = xr0 + c, c
        order[jx + 1], stash_slots[jx + 1] = gr0 + c, n_cg + c
    rest = [c for c in range(n_col) if c not in order]
    for jj in range(n_col):
        if order[jj] is None:
            order[jj] = rest.pop(0)
    s0, n_state = state_col0 // tn, state_cols // tn
    state_pos = [order.index(s0 + k) for k in range(n_state)]
    assert state_pos == sorted(state_pos)
    state_blk = [sum(p < jj for p in state_pos) for jj in range(n_col)]

    last = lambda i: i == n_tiles
    row = lambda i: jnp.minimum(i, n_tiles - 1)
    col = lambda i, j: jnp.where(last(i), order[-1], _lookup(order, j))
    seq_prev = lambda i: jnp.maximum(i - 1, 0) // period
    piece = lambda i, j: jnp.where(i == 0, 0, jnp.minimum(j, n_pieces - 1))

    def state_block(i, j):
        live = i % period == period - 1
        blk = jnp.where(live, _lookup(state_blk, j), n_state + 1)
        return row(i) // period, jnp.where(last(i), n_state, blk)

    per_cg = lambda a: a.reshape(a.shape[0], n_cg, tn).swapaxes(0, 1)
    whole = lambda a: pl.BlockSpec(a.shape, lambda i, j: (0,) * a.ndim)
    n_split = 4
    assert tt % (n_split * SUBLANES) == 0 and (tm - tm // 4) % (2 * SUBLANES * n_split) == 0
    params = [per_cg(conv_w), per_cg(conv_b), w_gates.reshape(n_cg, -1, RNN_BLOCK, 2 * RNN_BLOCK),
              per_cg(b_rg), per_cg(b_ig), per_cg(lam)]
    conv0 = conv0.reshape(n_seq, SUBLANES, n_cg, tn).swapaxes(1, 2)
    h0 = h0.reshape(n_seq, 1, n_cg, tn).swapaxes(1, 2)

    z, state, o_b, h_last = pl.pallas_call(
        functools.partial(_fused_body, n_tiles=n_tiles, period=period, n_cg=n_cg,
                          pieces_per_cg=pieces_per_cg, tt=tt, n_split=n_split,
                          stash_slots=tuple(stash_slots)),
        grid=(n_tiles + 1, n_col),
        in_specs=[
            pl.BlockSpec((tm, d), lambda i, j: (row(i), 0)),
            pl.BlockSpec((1, d), lambda i, j: (0, 0)),
            pl.BlockSpec((d, tn), lambda i, j: (0, col(i, j))),
            pl.BlockSpec((1, n_cg, SUBLANES, tn), lambda i, j: (seq_prev(i), 0, 0, 0)),
            pl.BlockSpec((1, n_cg, 1, tn), lambda i, j: (seq_prev(i), 0, 0, 0)),
        ] + [whole(p) for p in params],
        out_specs=[
            pl.BlockSpec((tm, tn), lambda i, j: (row(i), col(i, j))),
            pl.BlockSpec((state_rows, tn), state_block),
            pl.BlockSpec((tt, tn), lambda i, j: (jnp.maximum(i - 1, 0) * pieces_per_cg
                                                 + piece(i, j) % pieces_per_cg, piece(i, j) // pieces_per_cg)),
            pl.BlockSpec((1, 1, 1, tn), lambda i, j: (jnp.maximum(i - 1, 0), piece(i, j) // pieces_per_cg, 0, 0)),
        ],
        out_shape=[
            jax.ShapeDtypeStruct((m, n), BF16),
            jax.ShapeDtypeStruct((n_seq * state_rows, (n_state + 2) * tn), F32),
            jax.ShapeDtypeStruct((m, d_rnn), BF16),
            jax.ShapeDtypeStruct((n_tiles, n_cg, 1, tn), F32),
        ],
        scratch_shapes=[
            pltpu.VMEM((tm, d), BF16),
            pltpu.VMEM((2 * n_cg + 1, tm, tn), BF16),
            pltpu.VMEM((n_cg, SUBLANES, tn), F32),
            pltpu.VMEM((n_cg, SUBLANES, tn), F32),
            pltpu.VMEM((3, tt, tn), F32),
            pltpu.VMEM((tt, tn), BF16),
        ],
        compiler_params=_params(2, VMEM_FUSED),
        name=name,
    )(x2d, g, w, conv0, h0, *params)
    return z, state, o_b, h_last.reshape(n_tiles, d_rnn)[period - 1::period]


def _merge_body(oa_ref, ob_ref, ga_ref, gb_ref, wa_ref, wb_ref, o_ref):
    tm = o_ref.shape[0]
    for c in range(MERGE_CHUNKS):
        lo, hi = c * tm // MERGE_CHUNKS, (c + 1) * tm // MERGE_CHUNKS
        ya = jnp.dot(oa_ref[lo:hi, :], wa_ref[...], preferred_element_type=F32)
        yb = jnp.dot(ob_ref[lo:hi, :], wb_ref[...], preferred_element_type=F32)
        ga = jax.nn.sigmoid(ga_ref[lo:hi, :].astype(F32))
        gb = jax.nn.sigmoid(gb_ref[lo:hi, :].astype(F32))
        o_ref[lo:hi, :] = (ga * ya + gb * yb).astype(o_ref.dtype)


def _gate_merge(o_a, o_b, z2, w_a, w_b, *, ga_col0, gb_col0, tm, tn, name):
    m = o_a.shape[0]
    d = w_a.shape[1]
    ga0, gb0 = ga_col0 // tn, gb_col0 // tn
    return pl.pallas_call(
        _merge_body,
        grid=(m // tm, d // tn),
        in_specs=[
            pl.BlockSpec((tm, o_a.shape[1]), lambda i, j: (i, 0)),
            pl.BlockSpec((tm, o_b.shape[1]), lambda i, j: (i, 0)),
            pl.BlockSpec((tm, tn), lambda i, j: (i, ga0 + j)),
            pl.BlockSpec((tm, tn), lambda i, j: (i, gb0 + j)),
            pl.BlockSpec((w_a.shape[0], tn), lambda i, j: (0, j)),
            pl.BlockSpec((w_b.shape[0], tn), lambda i, j: (0, j)),
        ],
        out_specs=pl.BlockSpec((tm, tn), lambda i, j: (i, j)),
        out_shape=jax.ShapeDtypeStruct((m, d), BF16),
        compiler_params=_params(2, VMEM_MLP),
        name=name,
    )(o_a, o_b, z2, z2, w_a, w_b)


def _out_norm_body(m_ref, w_ref, x_ref, g_ref, o_ref):
    y = jnp.dot(m_ref[...], w_ref[...], preferred_element_type=F32)
    o_ref[...] = x_ref[...] + _rmsnorm_rows(y, g_ref[...])


def _out_norm(merged, w_out, x2d, g, *, tm, name):
    m, d = x2d.shape
    return pl.pallas_call(
        _out_norm_body,
        grid=(m // tm,),
        in_specs=[
            pl.BlockSpec((tm, d), lambda i: (i, 0)),
            pl.BlockSpec((d, d), lambda i: (0, 0)),
            pl.BlockSpec((tm, d), lambda i: (i, 0)),
            pl.BlockSpec((1, d), lambda i: (0, 0)),
        ],
        out_specs=pl.BlockSpec((tm, d), lambda i: (i, 0)),
        out_shape=jax.ShapeDtypeStruct((m, d), F32),
        compiler_params=_params(1, VMEM_MLP),
        name=name,
    )(merged, w_out, x2d, g)


def _ffn_body(x_ref, g1_ref, w1_ref, w2_ref, g2_ref, o_ref, xn_ref, *, n_steps, edge_chunks):
    f = pl.program_id(1)

    def rows_chunk(lo, hi, first, last):
        if first:
            xn = _rmsnorm_rows(x_ref[lo:hi, :], g1_ref[...]).astype(BF16)
            xn_ref[lo:hi, :] = xn
        else:
            xn = xn_ref[lo:hi, :]
        hid = jnp.dot(xn, w1_ref[...], preferred_element_type=F32)
        hid = jnp.square(jnp.maximum(hid, 0.0)).astype(BF16)
        acc = jnp.dot(hid, w2_ref[...], preferred_element_type=F32)
        if not first:
            acc = o_ref[lo:hi, :] + acc
        if last:
            acc = x_ref[lo:hi, :] + _rmsnorm_rows(acc, g2_ref[...])
        o_ref[lo:hi, :] = acc

    tm = o_ref.shape[0]
    for first, last in sorted({(k == 0, k == n_steps - 1) for k in range(n_steps)}):
        chunks = edge_chunks if (first or last) else 1

        @pl.when(((f == 0) == first) & ((f == n_steps - 1) == last))
        def _(first=first, last=last, chunks=chunks):
            for c in range(chunks):
                rows_chunk(c * tm // chunks, (c + 1) * tm // chunks, first, last)


def _ffn(x2d, g1, w1, w2, g2, *, tm, tf, name):
    m, d = x2d.shape
    d_ff = w1.shape[1]
    return pl.pallas_call(
        functools.partial(_ffn_body, n_steps=d_ff // tf, edge_chunks=FFN_EDGE_CHUNKS),
        grid=(m // tm, d_ff // tf),
        in_specs=[
            pl.BlockSpec((tm, d), lambda i, f: (i, 0)),
            pl.BlockSpec((1, d), lambda i, f: (0, 0)),
            pl.BlockSpec((d, tf), lambda i, f: (0, f)),
            pl.BlockSpec((tf, d), lambda i, f: (f, 0)),
            pl.BlockSpec((1, d), lambda i, f: (0, 0)),
        ],
        out_specs=pl.BlockSpec((tm, d), lambda i, f: (i, 0)),
        out_shape=jax.ShapeDtypeStruct((m, d), F32),
        scratch_shapes=[pltpu.VMEM((tm, d), BF16)],
        compiler_params=_params(2, VMEM_FFN),
        name=name,
    )(x2d, g1, w1, w2, g2)


def _pick_tile(n, target):
    t = min(n, target)
    while n % t:
        t //= 2
    return t


def _layer(x, cache_k, cache_v, conv_state, h_state, wts, tag):
    b, t, d = x.shape
    m = b * t
    x2d = x.reshape(m, d)
    d_rnn = wts["conv_w"].shape[1]
    n_in = wts["w_in"].shape[1]
    col_k, col_v, col_xr = D_ATTN, 2 * D_ATTN, 3 * D_ATTN
    col_gr = col_xr + d_rnn
    col_ga = col_gr + d_rnn
    col_gb = col_ga + d

    tm = _pick_tile(m, ROW_TILE)
    tn = COL_TILE
    keep = min(KV_REACH, t)
    if keep == t:
        period, state_rows = 1, tm
    else:
        assert t % tm == 0 and keep <= tm
        period, state_rows = t // tm, keep
    rnn_w = (wts["conv_w"], wts["conv_b"], wts["w_gates"], wts["b_rg"], wts["b_ig"], wts["lru_lambda"])
    if cache_k is None:
        conv0 = jnp.zeros((b, SUBLANES, d_rnn), F32)
        h0 = jnp.zeros((b, 1, d_rnn), F32)
        z2, state, o_b, h_last = _in_proj_rglru(
            x2d, wts["pre_mix_g"], wts["w_in"], conv0, h0, *rnn_w, tm=tm, tn=tn, tt=_pick_tile(tm, RNN_FRAMES),
            period=period, state_rows=state_rows, state_col0=col_k, state_cols=col_gr - col_k,
            xr_col0=col_xr, gr_col0=col_gr, name=f"in_proj_rglru_{tag}")
        z3 = z2.reshape(b, t, n_in)
        offsets = [i * CHUNK for i in range(PAST_CHUNKS + 2)]
        o_a = _attention(z3, z3, z3, wts["rel_bias"], q_col0=0, k_col0=col_k, v_col0=col_v, bb=1,
                         heads=HEADS_PER_STEP, offsets=offsets, name=f"attn_{tag}")
    else:
        z2, state = _in_proj(x2d, wts["pre_mix_g"], wts["w_in"], tm=tm, tn=tn, period=period,
                             state_rows=state_rows, state_col0=col_k, state_cols=col_gr - col_k,
                             name=f"in_proj_{tag}")
        z3 = z2.reshape(b, t, n_in)
        n_cached = cache_k.shape[1]
        assert t == CHUNK and n_cached == KV_REACH
        o_a = _attention(z3, z3, z3, wts["rel_bias"], q_col0=0, k_col0=col_k, v_col0=col_v,
                         bb=_pick_tile(b, SAMPLE_SEQS), heads=CACHED_HEADS_PER_STEP, offsets=[n_cached],
                         name=f"attn_{tag}",
                         k_cache=cache_k.astype(BF16).reshape(b, n_cached, D_ATTN),
                         v_cache=cache_v.astype(BF16).reshape(b, n_cached, D_ATTN))
        conv0 = jnp.pad(conv_state.astype(F32), ((0, 0), (SUBLANES - (CONV_WIDTH - 1), 0), (0, 0)))
        h0 = h_state.astype(F32).reshape(b, 1, d_rnn)
        o_b, h_last = _rglru(z3, conv0, h0, *rnn_w, xr_col0=col_xr, gr_col0=col_gr,
                             tt=_pick_tile(t, RNN_FRAMES), cw=COL_TILE, name=f"rglru_{tag}")
    state = state.reshape(b, keep, -1)
    new_k = state[:, :, :D_ATTN].reshape(b, keep, N_HEADS, HEAD_DIM)
    new_v = state[:, :, D_ATTN:2 * D_ATTN].reshape(b, keep, N_HEADS, HEAD_DIM)
    conv_tail = state[:, keep - (CONV_WIDTH - 1):, 2 * D_ATTN:col_gr - col_k]

    merged = _gate_merge(o_a.reshape(m, D_ATTN), o_b.reshape(m, d_rnn), z2, wts["w_attn_up"],
                         wts["w_rnn_up"], ga_col0=col_ga, gb_col0=col_gb, tm=tm, tn=tn,
                         name=f"gate_merge_{tag}")
    tm2 = _pick_tile(m, WIDE_ROW_TILE)
    x1 = _out_norm(merged, wts["w_out"], x2d, wts["post_mix_g"], tm=tm2, name=f"out_norm_{tag}")
    tf = FFN_TILE if m // tm2 >= FFN_WIDE_MIN_ROW_TILES else FFN_TILE // 2
    y = _ffn(x1, wts["pre_ffn_g"], wts["w_ff1"], wts["w_ff2"], wts["post_ffn_g"], tm=tm2, tf=tf,
             name=f"ffn_{tag}")
    return y.reshape(b, t, d), new_k, new_v, conv_tail, h_last.reshape(b, d_rnn)


def kernel(x_prompt, x_sample, cache_k, cache_v, state_conv, state_h, pre_mix_g, w_in, rel_bias, conv_w, conv_b, w_rg, b_rg, w_ig, b_ig, lru_lambda, w_attn_up, w_rnn_up, w_out, post_mix_g, pre_ffn_g, w_ff1, w_ff2, post_ffn_g):
    depth = w_in.shape[0]
    y_p, y_s = x_prompt, x_sample
    outs_p, outs_s = [], []
    for l in range(depth):
        wts = {
            "pre_mix_g": pre_mix_g[l][None], "post_mix_g": post_mix_g[l][None],
            "pre_ffn_g": pre_ffn_g[l][None], "post_ffn_g": post_ffn_g[l][None],
            "w_in": w_in[l].astype(BF16), "rel_bias": rel_bias[l],
            "conv_w": conv_w[l], "conv_b": conv_b[l][None],
            "w_gates": jnp.concatenate([w_rg[l], w_ig[l]], axis=-1).astype(BF16),
            "b_rg": b_rg[l][None], "b_ig": b_ig[l][None], "lru_lambda": lru_lambda[l][None],
            "w_attn_up": w_attn_up[l].astype(BF16), "w_rnn_up": w_rnn_up[l].astype(BF16),
            "w_out": w_out[l].astype(BF16), "w_ff1": w_ff1[l].astype(BF16), "w_ff2": w_ff2[l].astype(BF16),
        }
        y_p, *st_p = _layer(y_p, None, None, None, None, wts, f"p{l}")
        y_s, *st_s = _layer(y_s, cache_k[l], cache_v[l], state_conv[l], state_h[l], wts, f"s{l}")
        outs_p.append(st_p)
        outs_s.append(st_s)
    stack = lambda outs, i: jnp.stack([o[i] for o in outs])
    return (y_p, y_s,
            stack(outs_p, 0), stack(outs_p, 1), stack(outs_p, 2), stack(outs_p, 3),
            stack(outs_s, 0), stack(outs_s, 1), stack(outs_s, 2), stack(outs_s, 3))
```
